```python
import jax, jax.numpy as jnp
from jax import lax
import numpy as np

D_MODEL = 1024
BATCH = 2
SEQ = 16384
DEPTH = 2

HEAD_DIM = 64
FOX_HEADS = 8
NSA_HEADS = 8
NSA_KV_GROUPS = 2
NSA_GROUP_SIZE = NSA_HEADS // NSA_KV_GROUPS
CMP_BLOCK = 32
CMP_STRIDE = 16
SLC_BLOCK = 64
N_SELECT = 16
WINDOW = 512
Q_BLOCK = 128
POOL_WINDOWS = (2, 4, 8, 16)
POOL_GROUP = D_MODEL // len(POOL_WINDOWS)
FFN_HIDDEN = -(-8 * D_MODEL // (3 * 256)) * 256
RMS_EPS = 1e-6
NEG_INF = -1e30
BIG = 1e30
N_EVEN = (DEPTH + 1) // 2
N_ODD = DEPTH // 2
MIX_WIDTH = (FOX_HEADS + NSA_HEADS) * HEAD_DIM
KV_DIM = NSA_KV_GROUPS * HEAD_DIM
IN_SIZES = (FOX_HEADS * HEAD_DIM, FOX_HEADS * HEAD_DIM, FOX_HEADS * HEAD_DIM, FOX_HEADS,
            NSA_HEADS * HEAD_DIM, KV_DIM, KV_DIM, KV_DIM, KV_DIM, KV_DIM, KV_DIM, 3 * NSA_HEADS)
IN_COLS = sum(IN_SIZES)

kernel_name = "fox_nsa_pool_hybrid_trunk"


def rmsnorm(x, g):
    xf = x.astype(jnp.float32)
    y = xf * lax.rsqrt(jnp.mean(xf * xf, axis=-1, keepdims=True) + RMS_EPS)
    return (y * g.astype(jnp.float32)).astype(x.dtype)


def masked_softmax(s, mask):
    s = jnp.where(mask, s, NEG_INF)
    m = jnp.max(s, axis=-1, keepdims=True)
    e = jnp.where(mask, jnp.exp(s - m), 0.0)
    return e / jnp.maximum(jnp.sum(e, axis=-1, keepdims=True), 1e-30)


def alibi_slopes(n):
    return jnp.asarray(2.0 ** (-8.0 * (np.arange(n, dtype=np.float32) + 1.0) / n), dtype=jnp.float32)


def fox_attention(q, k, v, log_f):
    B, S, H, dh = q.shape
    c = jnp.transpose(jnp.cumsum(log_f, axis=1), (0, 2, 1))
    kpos = jnp.arange(S)
    scale = dh ** -0.5

    def block(qb):
        start = qb * Q_BLOCK
        t = start + jnp.arange(Q_BLOCK)
        qi = lax.dynamic_slice_in_dim(q, start, Q_BLOCK, axis=1)
        ci = lax.dynamic_slice_in_dim(c, start, Q_BLOCK, axis=2)
        s = jnp.einsum('bqhd,bkhd->bhqk', qi, k).astype(jnp.float32) * scale
        s = s + ci[..., None] - c[:, :, None, :]
        mask = kpos[None, :] <= t[:, None]
        p = masked_softmax(s, mask)
        return jnp.einsum('bhqk,bkhd->bqhd', p.astype(v.dtype), v)

    out = lax.map(block, jnp.arange(S // Q_BLOCK))
    return jnp.transpose(out, (1, 0, 2, 3, 4)).reshape(B, S, H * dh)


def nsa_compress(tok, pe, w1, w2):
    B, S, G, dh = tok.shape
    r = CMP_BLOCK // CMP_STRIDE
    n_chunks = S // CMP_STRIDE
    n_c = n_chunks - r + 1
    chunks = tok.reshape(B, n_chunks, CMP_STRIDE, G, dh)
    blocks = jnp.concatenate([chunks[:, j:j + n_c] for j in range(r)], axis=2)
    blocks = blocks + pe[None, None, :, None, :]
    flat = jnp.transpose(blocks, (0, 1, 3, 2, 4)).reshape(B, n_c, G, CMP_BLOCK * dh)
    return jax.nn.gelu(flat @ w1) @ w2


def nsa_attention(q, k_c, v_c, k_s, v_s, k_w, v_w, gates, pe, w1, w2):
    B, S, H, dh = q.shape
    G = k_c.shape[2]
    R = H // G
    scale = dh ** -0.5
    slopes = alibi_slopes(H).reshape(G, R)
    dt = q.dtype
    r = CMP_BLOCK // CMP_STRIDE
    ratio = SLC_BLOCK // CMP_STRIDE
    n_s = S // SLC_BLOCK
    n_sel = min(N_SELECT, n_s)

    kc = nsa_compress(k_c, pe[0], w1[0], w2[0])
    vc = nsa_compress(v_c, pe[1], w1[1], w2[1])
    n_c = kc.shape[1]
    cmp_end = jnp.arange(n_c) * CMP_STRIDE + CMP_BLOCK - 1

    q_g = q.reshape(B, S, G, R, dh)
    k_blocks = jnp.transpose(k_s.reshape(B, n_s, SLC_BLOCK, G, dh), (0, 3, 1, 2, 4)).reshape(B, G, n_s, SLC_BLOCK * dh)
    v_blocks = jnp.transpose(v_s.reshape(B, n_s, SLC_BLOCK, G, dh), (0, 3, 1, 2, 4)).reshape(B, G, n_s, SLC_BLOCK * dh)
    k_w_pad = jnp.pad(k_w, ((0, 0), (WINDOW, 0), (0, 0), (0, 0)))
    v_w_pad = jnp.pad(v_w, ((0, 0), (WINDOW, 0), (0, 0), (0, 0)))
    blk = jnp.arange(n_s)

    def block(qb):
        start = qb * Q_BLOCK
        t = start + jnp.arange(Q_BLOCK)
        tf = t.astype(jnp.float32)
        qi = lax.dynamic_slice_in_dim(q_g, start, Q_BLOCK, axis=1)
        gi = lax.dynamic_slice_in_dim(gates, start, Q_BLOCK, axis=1)

        d_c = tf[:, None] - cmp_end[None, :].astype(jnp.float32)
        s_c = jnp.einsum('bqgrd,bngd->bgrqn', qi, kc).astype(jnp.float32) * scale
        s_c = s_c - slopes[None, :, :, None, None] * d_c
        p_c = masked_softmax(s_c, d_c >= 0)
        o_c = jnp.einsum('bgrqn,bngd->bqgrd', p_c.astype(dt), vc)

        imp = jnp.sum(p_c, axis=2)
        imp = jnp.pad(imp, ((0, 0), (0, 0), (0, 0), (r - 1, ratio * n_s + r - 1 - n_c - (r - 1))))
        p_slc = imp[..., 0:ratio * n_s:ratio]
        for o in range(1, ratio + r - 1):
            p_slc = p_slc + imp[..., o:o + ratio * n_s:ratio]
        cur = t // SLC_BLOCK
        forced = (blk[None, :] == 0) | (blk[None, :] == cur[:, None]) | (blk[None, :] == cur[:, None] - 1)
        future = blk[None, :] > cur[:, None]
        p_slc = jnp.where(forced, BIG, jnp.where(future, NEG_INF, p_slc))
        _, idx = lax.top_k(p_slc, n_sel)

        flat_idx = idx.reshape(B, G, Q_BLOCK * n_sel)[..., None]
        kg = jnp.take_along_axis(k_blocks, flat_idx, axis=2).reshape(B, G, Q_BLOCK, n_sel * SLC_BLOCK, dh)
        vg = jnp.take_along_axis(v_blocks, flat_idx, axis=2).reshape(B, G, Q_BLOCK, n_sel * SLC_BLOCK, dh)
        kpos = (idx[..., None] * SLC_BLOCK + jnp.arange(SLC_BLOCK)).reshape(B, G, Q_BLOCK, n_sel * SLC_BLOCK)
        d_s = (t[None, None, :, None] - kpos).astype(jnp.float32)
        s_s = jnp.einsum('bqgrd,bgqkd->bgrqk', qi, kg).astype(jnp.float32) * scale
        s_s = s_s - slopes[None, :, :, None, None] * d_s[:, :, None]
        p_s = masked_softmax(s_s, (d_s >= 0)[:, :, None])
        o_s = jnp.einsum('bgrqk,bgqkd->bqgrd', p_s.astype(dt), vg)

        kw = lax.dynamic_slice_in_dim(k_w_pad, start, WINDOW + Q_BLOCK, axis=1)
        vw = lax.dynamic_slice_in_dim(v_w_pad, start, WINDOW + Q_BLOCK, axis=1)
        wpos = start - WINDOW + jnp.arange(WINDOW + Q_BLOCK)
        d_w = t[:, None] - wpos[None, :]
        m_w = (d_w >= 0) & (d_w < WINDOW) & (wpos[None, :] >= 0)
        s_w = jnp.einsum('bqgrd,bkgd->bgrqk', qi, kw).astype(jnp.float32) * scale
        s_w = s_w - slopes[None, :, :, None, None] * d_w.astype(jnp.float32)
        p_w = masked_softmax(s_w, m_w)
        o_w = jnp.einsum('bgrqk,bkgd->bqgrd', p_w.astype(dt), vw)

        return gi[..., 0:1] * o_c + gi[..., 1:2] * o_s + gi[..., 2:3] * o_w

    out = lax.map(block, jnp.arange(S // Q_BLOCK))
    return jnp.transpose(out, (1, 0, 2, 3, 4, 5)).reshape(B, S, H * dh)


def mixer_fox_nsa(h, w_in, b_f, cmp_pe, cmp_w1, cmp_w2, w_out):
    B, S, _ = h.shape
    proj = h @ w_in
    offs = [0]
    for n in IN_SIZES:
        offs.append(offs[-1] + n)
    fq, fk, fv, ff, nq, kc, vc, ks, vs, kw, vw, ng = [proj[..., offs[i]:offs[i + 1]] for i in range(len(IN_SIZES))]
    hd = (B, S, FOX_HEADS, HEAD_DIM)
    log_f = jax.nn.log_sigmoid((ff + b_f).astype(jnp.float32))
    o_fox = fox_attention(fq.reshape(hd), fk.reshape(hd), fv.reshape(hd), log_f)
    kvd = (B, S, NSA_KV_GROUPS, HEAD_DIM)
    gates = jax.nn.sigmoid(ng.astype(jnp.float32)).astype(h.dtype).reshape(B, S, NSA_KV_GROUPS, NSA_GROUP_SIZE, 3)
    o_nsa = nsa_attention(nq.reshape(B, S, NSA_HEADS, HEAD_DIM), kc.reshape(kvd), vc.reshape(kvd),
                          ks.reshape(kvd), vs.reshape(kvd), kw.reshape(kvd), vw.reshape(kvd),
                          gates, cmp_pe, cmp_w1, cmp_w2)
    return jnp.concatenate([o_fox, o_nsa], axis=-1) @ w_out


def mixer_pool(h, w_groups, scale):
    B, S, D = h.shape
    hf = h.astype(jnp.float32)
    c = jnp.cumsum(hf, axis=1)
    count = jnp.arange(1, S + 1, dtype=jnp.float32)
    outs = []
    for gi, w in enumerate(POOL_WINDOWS):
        sl = slice(gi * POOL_GROUP, (gi + 1) * POOL_GROUP)
        cg = c[..., sl]
        lag = jnp.pad(cg, ((0, 0), (w, 0), (0, 0)))[:, :S]
        mean = (cg - lag) / jnp.minimum(count, float(w))[None, :, None]
        outs.append(mean - hf[..., sl])
    pooled = jnp.stack(outs, axis=2).astype(h.dtype)
    y = jnp.einsum('bsgc,gcd->bsgd', pooled, w_groups).reshape(B, S, D)
    return y * scale


def swiglu(h, w_gate, w_up, w_down):
    return (jax.nn.silu(h @ w_gate) * (h @ w_up)) @ w_down


def setup_inputs(seed: int = 0) -> dict:
    key = jax.random.key(seed)
    ks = jax.random.split(key, 13)
    f32 = jnp.float32
    dh = HEAD_DIM
    x = jax.random.normal(ks[0], (BATCH, SEQ, D_MODEL), f32)
    norm_g = 1.0 + 0.05 * jax.random.normal(ks[1], (DEPTH, 4, D_MODEL), f32)
    attn_w_in = jax.random.normal(ks[2], (N_EVEN, D_MODEL, IN_COLS), f32) * D_MODEL ** -0.5
    fox_b_f = jax.random.uniform(ks[3], (N_EVEN, FOX_HEADS), f32, 1.0, 4.0)
    nsa_cmp_pe = 0.1 * jax.random.normal(ks[4], (N_EVEN, 2, CMP_BLOCK, dh), f32)
    nsa_cmp_w1 = jax.random.normal(ks[5], (N_EVEN, 2, CMP_BLOCK * dh, dh), f32) * (CMP_BLOCK * dh) ** -0.5
    nsa_cmp_w2 = jax.random.normal(ks[6], (N_EVEN, 2, dh, dh), f32) * dh ** -0.5
    attn_w_out = jax.random.normal(ks[7], (N_EVEN, MIX_WIDTH, D_MODEL), f32) * MIX_WIDTH ** -0.5
    pool_w = jax.random.normal(ks[8], (N_ODD, len(POOL_WINDOWS), POOL_GROUP, POOL_GROUP), f32) * POOL_GROUP ** -0.5
    pool_scale = 1.0 + 0.1 * jax.random.normal(ks[9], (N_ODD, D_MODEL), f32)
    ffn_w_gate = jax.random.normal(ks[10], (DEPTH, D_MODEL, FFN_HIDDEN), f32) * D_MODEL ** -0.5
    ffn_w_up = jax.random.normal(ks[11], (DEPTH, D_MODEL, FFN_HIDDEN), f32) * D_MODEL ** -0.5
    ffn_w_down = jax.random.normal(ks[12], (DEPTH, FFN_HIDDEN, D_MODEL), f32) * FFN_HIDDEN ** -0.5
    return {"x": x, "norm_g": norm_g, "attn_w_in": attn_w_in, "fox_b_f": fox_b_f,
            "nsa_cmp_pe": nsa_cmp_pe, "nsa_cmp_w1": nsa_cmp_w1, "nsa_cmp_w2": nsa_cmp_w2,
            "attn_w_out": attn_w_out, "pool_w": pool_w, "pool_scale": pool_scale,
            "ffn_w_gate": ffn_w_gate, "ffn_w_up": ffn_w_up, "ffn_w_down": ffn_w_down}


def reference(x, norm_g, attn_w_in, fox_b_f, nsa_cmp_pe, nsa_cmp_w1, nsa_cmp_w2, attn_w_out,
              pool_w, pool_scale, ffn_w_gate, ffn_w_up, ffn_w_down):
    h = x
    for layer in range(DEPTH):
        g = norm_g[layer]
        i = layer // 2
        u = rmsnorm(h, g[0])
        if layer % 2 == 0:
            m = mixer_fox_nsa(u, attn_w_in[i], fox_b_f[i], nsa_cmp_pe[i], nsa_cmp_w1[i], nsa_cmp_w2[i], attn_w_out[i])
        else:
            m = mixer_pool(u, pool_w[i], pool_scale[i])
        h = h + rmsnorm(m, g[1])
        u = rmsnorm(h, g[2])
        h = h + rmsnorm(swiglu(u, ffn_w_gate[layer], ffn_w_up[layer], ffn_w_down[layer]), g[3])
    return h
```

```python
import functools

import numpy as np
import jax
import jax.numpy as jnp
from jax import lax
from jax.experimental import pallas as pl
from jax.experimental.pallas import tpu as pltpu

F32 = jnp.float32
BF16 = jnp.bfloat16

HEAD_DIM = 64
FOX_HEADS = 8
NSA_HEADS = 8
NSA_GROUPS = 2
NSA_GROUP_SIZE = NSA_HEADS // NSA_GROUPS
CMP_BLOCK = 32
CMP_STRIDE = 16
SLC_BLOCK = 64
N_SELECT = 16
WINDOW = 512
POOL_WINDOWS = (2, 4, 8, 16)
RMS_EPS = 1e-6
NEG_INF = -1e30
BIG = 1e30
REMOVED = -3e38
LANES = 128
PAIR = 2 * HEAD_DIM
VMEM_LIMIT = 56 * 1024 * 1024

ALIBI_SLOPES = tuple(float(2.0 ** (-8.0 * (i + 1.0) / NSA_HEADS)) for i in range(NSA_HEADS))


def _cparams(*sem):
    return pltpu.CompilerParams(dimension_semantics=sem, vmem_limit_bytes=VMEM_LIMIT)


def _rms(x, g):
    return x * lax.rsqrt(jnp.mean(x * x, axis=-1, keepdims=True) + RMS_EPS) * g


def _dot(a, b):
    return jnp.dot(a, b, preferred_element_type=F32)


def _dot_nt(a, b):
    return lax.dot_general(a, b, (((1,), (1,)), ((), ())), preferred_element_type=F32)


def _inproj_kernel(x_ref, g_ref, w_ref, wc_ref, ws_ref, o_ref, oc_ref, os_ref, *, col_chunk):
    u = _rms(x_ref[...], g_ref[...])
    ub = u.astype(BF16)
    for c in range(0, o_ref.shape[1], col_chunk):
        o_ref[:, c:c + col_chunk] = _dot(ub, w_ref[:, c:c + col_chunk]).astype(BF16)
    oc_ref[...] = _dot(ub, wc_ref[...]).astype(BF16)
    os_ref[...] = jnp.dot(u, ws_ref[...], preferred_element_type=F32,
                          precision=lax.Precision.HIGHEST)


def _inproj(x2, g, w_main, w_cmp, w_small, *, tm):
    n, d = x2.shape
    cm, cc, cs = w_main.shape[1], w_cmp.shape[1], w_small.shape[1]
    const = lambda i: (0, 0)
    return pl.pallas_call(
        functools.partial(_inproj_kernel, col_chunk=512),
        out_shape=(jax.ShapeDtypeStruct((n, cm), BF16),
                   jax.ShapeDtypeStruct((n, cc), BF16),
                   jax.ShapeDtypeStruct((n, cs), F32)),
        grid=(n // tm,),
        in_specs=[pl.BlockSpec((tm, d), lambda i: (i, 0)),
                  pl.BlockSpec((1, d), const),
                  pl.BlockSpec((d, cm), const),
                  pl.BlockSpec((d, cc), const),
                  pl.BlockSpec((d, cs), const)],
        out_specs=(pl.BlockSpec((tm, cm), lambda i: (i, 0)),
                   pl.BlockSpec((tm, cc), lambda i: (i, 0)),
                   pl.BlockSpec((tm, cs), lambda i: (i, 0))),
        compiler_params=_cparams("parallel"),
        name="inproj",
    )(x2, g, w_main, w_cmp, w_small)


def _forget_cumsum_kernel(sm_ref, bf_ref, tri_ref, o_ref, carry_ref):
    @pl.when(pl.program_id(1) == 0)
    def _():
        carry_ref[...] = jnp.zeros_like(carry_ref)

    zt = sm_ref[...].T
    z = zt[0:FOX_HEADS, :] + bf_ref[...]
    log_f = jnp.minimum(z, 0.0) - jnp.log1p(jnp.exp(-jnp.abs(z)))
    c = jnp.dot(log_f, tri_ref[...], preferred_element_type=F32,
                precision=lax.Precision.HIGHEST) + carry_ref[...]
    o_ref[...] = c
    carry_ref[...] = c[:, c.shape[1] - 1:]


def _forget_cumsum(small, b_f, *, batch, seq, tc):
    tri = jnp.asarray(np.triu(np.ones((tc, tc), np.float32)))
    nchunk = seq // tc
    return pl.pallas_call(
        _forget_cumsum_kernel,
        out_shape=jax.ShapeDtypeStruct((batch, FOX_HEADS, seq), F32),
        grid=(batch, nchunk),
        in_specs=[pl.BlockSpec((tc, small.shape[1]), lambda b, j: (b * nchunk + j, 0)),
                  pl.BlockSpec((FOX_HEADS, 1), lambda b, j: (0, 0)),
                  pl.BlockSpec((tc, tc), lambda b, j: (0, 0))],
        out_specs=pl.BlockSpec((None, FOX_HEADS, tc), lambda b, j: (b, 0, j)),
        scratch_shapes=[pltpu.VMEM((FOX_HEADS, 1), F32)],
        compiler_params=_cparams("parallel", "arbitrary"),
        name="forget_cumsum",
    )(small, b_f.reshape(FOX_HEADS, 1), tri)


def _lane_is_low():
    return lax.broadcasted_iota(jnp.int32, (1, PAIR), 1) < HEAD_DIM


def _flash_step(s, v_tile, m_ref, l_ref, acc_ref):
    m_old = m_ref[...]
    m_new = jnp.maximum(m_old, jnp.max(s, axis=-1, keepdims=True))
    alpha = jnp.exp(m_old - m_new)
    p = jnp.exp(s - m_new)
    l_ref[...] = alpha * l_ref[...] + jnp.sum(p, axis=-1, keepdims=True)
    acc_ref[...] = alpha * acc_ref[...] + _dot(p.astype(BF16), v_tile)
    m_ref[...] = m_new


def _flash_reset(m_ref, l_ref, acc_ref):
    m_ref[...] = jnp.full_like(m_ref, NEG_INF)
    l_ref[...] = jnp.zeros_like(l_ref)
    acc_ref[...] = jnp.zeros_like(acc_ref)


def _fox_kernel(q_ref, k_ref, v_ref, c_ref, o_ref, m_ref, l_ref, acc_ref, *, tile):
    i = pl.program_id(2)
    low = _lane_is_low()
    q = q_ref[...]
    row = lax.broadcasted_iota(jnp.int32, (tile, tile), 0)
    col = lax.broadcasted_iota(jnp.int32, (tile, tile), 1)
    causal = col <= row
    outs = []
    for h in range(2):
        qh = jnp.where(low if h == 0 else jnp.logical_not(low), q, jnp.zeros_like(q))
        _flash_reset(m_ref, l_ref, acc_ref)

        def scores(j):
            ks = pl.multiple_of(j * tile, tile)
            s = _dot_nt(qh, k_ref[pl.ds(ks, tile), :]) - c_ref[j, h:h + 1, :]
            return s, v_ref[pl.ds(ks, tile), :]

        def body(j, carry):
            s, vt = scores(j)
            _flash_step(s, vt, m_ref, l_ref, acc_ref)
            return carry

        lax.fori_loop(0, i, body, 0)
        s, vt = scores(i)
        _flash_step(jnp.where(causal, s, NEG_INF), vt, m_ref, l_ref, acc_ref)
        outs.append(acc_ref[...] / l_ref[...])
    o_ref[...] = jnp.where(low, outs[0], outs[1]).astype(o_ref.dtype)


def _fox_attention(pj, c_tiles, *, batch, seq, tile, q_blk, k_blk, v_blk):
    nq = seq // tile
    npair = FOX_HEADS // 2
    return pl.pallas_call(
        functools.partial(_fox_kernel, tile=tile),
        out_shape=jax.ShapeDtypeStruct((batch * seq, FOX_HEADS * HEAD_DIM), BF16),
        grid=(batch, npair, nq),
        in_specs=[pl.BlockSpec((tile, PAIR), lambda b, p, i: (b * nq + i, q_blk + p)),
                  pl.BlockSpec((seq, PAIR), lambda b, p, i: (b, k_blk + p)),
                  pl.BlockSpec((seq, PAIR), lambda b, p, i: (b, v_blk + p)),
                  pl.BlockSpec((None, None, nq, 2, tile), lambda b, p, i: (b, p, 0, 0, 0))],
        out_specs=pl.BlockSpec((tile, PAIR), lambda b, p, i: (b * nq + i, p)),
        scratch_shapes=[pltpu.VMEM((tile, 1), F32), pltpu.VMEM((tile, 1), F32),
                        pltpu.VMEM((tile, PAIR), F32)],
        compiler_params=_cparams("parallel", "parallel", "arbitrary"),
        name="fox_attention",
    )(pj, pj, pj, c_tiles)


def _gelu_tanh(x):
    return 0.5 * x * (1.0 + jnp.tanh(0.7978845608028654 * (x + 0.044715 * x * x * x)))


def _compress_kernel(t_ref, pe_ref, w1_ref, w2_ref, o_ref):
    half = w1_ref.shape[0] // 2
    t = t_ref[...]
    a = _dot(t, w1_ref[0:half, :])
    b = _dot(t, w1_ref[half:, :])
    pe_term = _dot(pe_ref[...], w1_ref[...])[0:1, :]
    nch = t.shape[0]
    h = a + pltpu.roll(b, nch - 1, axis=0) + pe_term
    o_ref[...] = _dot(_gelu_tanh(h).astype(BF16), w2_ref[...]).astype(o_ref.dtype)


def _compress(tchunks, pe8, w1, w2dup):
    b, two, g, nch, width = tchunks.shape
    return pl.pallas_call(
        _compress_kernel,
        out_shape=jax.ShapeDtypeStruct((b, two, g, nch, PAIR), BF16),
        grid=(b, two, g),
        in_specs=[pl.BlockSpec((None, None, None, nch, width), lambda bb, kv, gg: (bb, kv, gg, 0, 0)),
                  pl.BlockSpec((None, 8, 2 * width), lambda bb, kv, gg: (kv, 0, 0)),
                  pl.BlockSpec((None, 2 * width, HEAD_DIM), lambda bb, kv, gg: (kv, 0, 0)),
                  pl.BlockSpec((None, HEAD_DIM, PAIR), lambda bb, kv, gg: (kv, 0, 0))],
        out_specs=pl.BlockSpec((None, None, None, nch, PAIR), lambda bb, kv, gg: (bb, kv, gg, 0, 0)),
        compiler_params=_cparams("parallel", "parallel", "parallel"),
        name="nsa_compress",
    )(tchunks, pe8, w1, w2dup)


def _group_slope(g, r):
    return jnp.where(g == 0, ALIBI_SLOPES[r], ALIBI_SLOPES[NSA_GROUP_SIZE + r]).astype(F32)


def _split3(x):
    hi = x.astype(BF16)
    r1 = x - hi.astype(F32)
    mid = r1.astype(BF16)
    lo = (r1 - mid.astype(F32)).astype(BF16)
    return hi, mid, lo


def _cmp_select_kernel(q_ref, kc_ref, vc_ref, pool_ref, oc_ref, sel_ref, *, tq, n_sel):
    g = pl.program_id(1)
    t0 = pl.program_id(2) * tq
    nch = kc_ref.shape[0]
    ns = pool_ref.shape[1]
    low = _lane_is_low()
    t = t0 + lax.broadcasted_iota(jnp.int32, (tq, 1), 0)
    cmp_end = lax.broadcasted_iota(jnp.int32, (1, nch), 1) * CMP_STRIDE + (CMP_BLOCK - 1)
    d = t - cmp_end
    valid = d >= 0
    df = d.astype(F32)
    kc = kc_ref[...]
    vc = vc_ref[...]
    imp = jnp.zeros((tq, nch), F32)
    for pair in range(NSA_GROUP_SIZE // 2):
        q = q_ref[:, pair * PAIR:(pair + 1) * PAIR]
        outs = []
        for h in range(2):
            qh = jnp.where(low if h == 0 else jnp.logical_not(low), q, jnp.zeros_like(q))
            s = _dot_nt(qh, kc) - _group_slope(g, 2 * pair + h) * df
            s = jnp.where(valid, s, NEG_INF)
            m = jnp.max(s, axis=-1, keepdims=True)
            e = jnp.where(valid, jnp.exp(s - m), 0.0)
            p = e / jnp.maximum(jnp.sum(e, axis=-1, keepdims=True), 1e-30)
            outs.append(_dot(p.astype(BF16), vc))
            imp = imp + p
        oc_ref[:, pair * PAIR:(pair + 1) * PAIR] = jnp.where(low, outs[0], outs[1])

    pool = pool_ref[...]
    hi, mid, lo = _split3(imp)
    p_slc = _dot(hi, pool) + _dot(mid, pool) + _dot(lo, pool)

    blk = lax.broadcasted_iota(jnp.int32, (1, ns), 1)
    cur = t // SLC_BLOCK
    forced = (blk == 0) | (blk == cur) | (blk == cur - 1)
    val = jnp.where(forced, BIG, jnp.where(blk > cur, NEG_INF, p_slc))
    sel = jnp.zeros((tq, ns), F32)
    for _ in range(n_sel):
        mx = jnp.max(val, axis=-1, keepdims=True)
        pick = jnp.min(jnp.where(val == mx, blk, ns), axis=-1, keepdims=True)
        hit = blk == pick
        sel = jnp.where(hit, 1.0, sel)
        val = jnp.where(hit, REMOVED, val)
    sel_ref[...] = sel


def _cmp_select(pj, cmp_kv, pool, *, batch, seq, tq, q_blk, n_sel):
    nq = seq // tq
    nch = cmp_kv.shape[3]
    ns = pool.shape[1]
    gw = NSA_GROUP_SIZE * HEAD_DIM
    return pl.pallas_call(
        functools.partial(_cmp_select_kernel, tq=tq, n_sel=n_sel),
        out_shape=(jax.ShapeDtypeStruct((batch * seq, NSA_HEADS * HEAD_DIM), F32),
                   jax.ShapeDtypeStruct((batch, NSA_GROUPS, seq, ns), F32)),
        grid=(batch, NSA_GROUPS, nq),
        in_specs=[pl.BlockSpec((tq, gw), lambda b, g, i: (b * nq + i, q_blk + g)),
                  pl.BlockSpec((None, None, None, nch, PAIR), lambda b, g, i: (b, 0, g, 0, 0)),
                  pl.BlockSpec((None, None, None, nch, PAIR), lambda b, g, i: (b, 1, g, 0, 0)),
                  pl.BlockSpec((nch, ns), lambda b, g, i: (0, 0))],
        out_specs=(pl.BlockSpec((tq, gw), lambda b, g, i: (b * nq + i, g)),
                   pl.BlockSpec((None, None, tq, ns), lambda b, g, i: (b, g, i, 0))),
        compiler_params=_cparams("parallel", "parallel", "parallel"),
        name="nsa_cmp_select",
    )(pj, cmp_kv, cmp_kv, pool)


def _nsa_kernel(q_ref, ks_ref, vs_ref, kwp_ref, vwp_ref, kwd_ref, vwd_ref, sel_ref, oc_ref, gate_ref,
                o_ref, m_ref, l_ref, acc_ref, *, tile):
    g = pl.program_id(1)
    i = pl.program_id(2)
    ns = sel_ref.shape[1]
    bpt = tile // SLC_BLOCK
    low = _lane_is_low()
    row = lax.broadcasted_iota(jnp.int32, (tile, tile), 0)
    col = lax.broadcasted_iota(jnp.int32, (tile, tile), 1)
    causal = col <= row
    relpos = lax.broadcasted_iota(jnp.int32, (1, tile), 1).astype(F32)
    sel = sel_ref[...].astype(BF16)
    gates = jax.nn.sigmoid(gate_ref[...])
    blk_of_key = lax.broadcasted_iota(jnp.int32, (ns, tile), 1) // SLC_BLOCK
    blk_row = lax.broadcasted_iota(jnp.int32, (ns, tile), 0)

    def key_mask(j):
        expand = jnp.where(blk_row == blk_of_key + j * bpt, 1.0, 0.0).astype(BF16)
        return _dot(sel, expand) > 0.5

    for pair in range(NSA_GROUP_SIZE // 2):
        q = q_ref[:, pair * PAIR:(pair + 1) * PAIR]
        outs = []
        for h in range(2):
            r = 2 * pair + h
            slope = _group_slope(g, r)
            qh = jnp.where(low if h == 0 else jnp.logical_not(low), q, jnp.zeros_like(q))

            def alibi(j):
                return slope * (relpos + ((j - i) * tile).astype(F32))

            _flash_reset(m_ref, l_ref, acc_ref)

            def sel_scores(j):
                ks = pl.multiple_of(j * tile, tile)
                s = _dot_nt(qh, ks_ref[pl.ds(ks, tile), :]) + alibi(j)
                return s, vs_ref[pl.ds(ks, tile), :]

            def sel_body(j, carry):
                s, vt = sel_scores(j)
                _flash_step(jnp.where(key_mask(j), s, NEG_INF), vt, m_ref, l_ref, acc_ref)
                return carry

            s, vt = sel_scores(i)
            _flash_step(jnp.where(key_mask(i) & causal, s, NEG_INF), vt, m_ref, l_ref, acc_ref)
            lax.fori_loop(0, i, sel_body, 0)
            o_s = acc_ref[...] / l_ref[...]

            _flash_reset(m_ref, l_ref, acc_ref)
            s = _dot_nt(qh, kwd_ref[...]) + alibi(i)
            _flash_step(jnp.where(causal, s, NEG_INF), vwd_ref[...], m_ref, l_ref, acc_ref)

            @pl.when(i > 0)
            def _():
                sp = _dot_nt(qh, kwp_ref[...]) + alibi(i - 1)
                _flash_step(jnp.where(col > row, sp, NEG_INF), vwp_ref[...], m_ref, l_ref, acc_ref)

            o_w = acc_ref[...] / l_ref[...]

            def gate_col(branch):
                c0 = FOX_HEADS + 3 * r + branch
                c1 = c0 + 3 * NSA_GROUP_SIZE
                return jnp.where(g == 0, gates[:, c0:c0 + 1], gates[:, c1:c1 + 1])

            o_c = oc_ref[:, pair * PAIR:(pair + 1) * PAIR]
            outs.append(gate_col(0) * o_c + gate_col(1) * o_s + gate_col(2) * o_w)
        o_ref[:, pair * PAIR:(pair + 1) * PAIR] = jnp.where(low, outs[0], outs[1]).astype(o_ref.dtype)


def _nsa_attention(pj, sel, o_c, small, *, batch, seq, tile, q_blk, ks_blk, vs_blk, kw_blk, vw_blk):
    nq = seq // tile
    ns = sel.shape[3]
    gw = NSA_GROUP_SIZE * HEAD_DIM
    prev = lambda b, g, i, base: (b * nq + jnp.maximum(i - 1, 0), base + g)
    return pl.pallas_call(
        functools.partial(_nsa_kernel, tile=tile),
        out_shape=jax.ShapeDtypeStruct((batch * seq, NSA_HEADS * HEAD_DIM), BF16),
        grid=(batch, NSA_GROUPS, nq),
        in_specs=[pl.BlockSpec((tile, gw), lambda b, g, i: (b * nq + i, q_blk + g)),
                  pl.BlockSpec((seq, PAIR), lambda b, g, i: (b, ks_blk + g)),
                  pl.BlockSpec((seq, PAIR), lambda b, g, i: (b, vs_blk + g)),
                  pl.BlockSpec((tile, PAIR), functools.partial(prev, base=kw_blk)),
                  pl.BlockSpec((tile, PAIR), functools.partial(prev, base=vw_blk)),
                  pl.BlockSpec((tile, PAIR), lambda b, g, i: (b * nq + i, kw_blk + g)),
                  pl.BlockSpec((tile, PAIR), lambda b, g, i: (b * nq + i, vw_blk + g)),
                  pl.BlockSpec((None, None, tile, ns), lambda b, g, i: (b, g, i, 0)),
                  pl.BlockSpec((tile, gw), lambda b, g, i: (b * nq + i, g)),
                  pl.BlockSpec((tile, small.shape[1]), lambda b, g, i: (b * nq + i, 0))],
        out_specs=pl.BlockSpec((tile, gw), lambda b, g, i: (b * nq + i, g)),
        scratch_shapes=[pltpu.VMEM((tile, 1), F32), pltpu.VMEM((tile, 1), F32),
                        pltpu.VMEM((tile, PAIR), F32)],
        compiler_params=_cparams("parallel", "parallel", "arbitrary"),
        name="nsa_attention",
    )(pj, pj, pj, pj, pj, pj, pj, sel, o_c, small)


def _outproj_kernel(h_ref, a_ref, b_ref, wa_ref, wb_ref, g_ref, o_ref):
    m = _dot(a_ref[...], wa_ref[...]) + _dot(b_ref[...], wb_ref[...])
    o_ref[...] = h_ref[...] + _rms(m, g_ref[...])


def _outproj(h2, a, b, wa, wb, g, *, tm):
    n, d = h2.shape
    const = lambda i: (0, 0)
    return pl.pallas_call(
        _outproj_kernel,
        out_shape=jax.ShapeDtypeStruct((n, d), F32),
        grid=(n // tm,),
        in_specs=[pl.BlockSpec((tm, d), lambda i: (i, 0)),
                  pl.BlockSpec((tm, a.shape[1]), lambda i: (i, 0)),
                  pl.BlockSpec((tm, b.shape[1]), lambda i: (i, 0)),
                  pl.BlockSpec(wa.shape, const),
                  pl.BlockSpec(wb.shape, const),
                  pl.BlockSpec((1, d), const)],
        out_specs=pl.BlockSpec((tm, d), lambda i: (i, 0)),
        compiler_params=_cparams("parallel"),
        name="attn_outproj",
    )(h2, a, b, wa, wb, g)


def _ffn_kernel(h_ref, gin_ref, gout_ref, wg_ref, wu_ref, wd_ref, o_ref, acc_ref, *, chunk):
    h = h_ref[...]
    ub = _rms(h, gin_ref[...]).astype(BF16)
    hidden = wg_ref.shape[1]
    for idx, c in enumerate(range(0, hidden, chunk)):
        gate = _dot(ub, wg_ref[:, c:c + chunk])
        up = _dot(ub, wu_ref[:, c:c + chunk])
        act = (gate * jax.nn.sigmoid(gate) * up).astype(BF16)
        part = _dot(act, wd_ref[c:c + chunk, :])
        if idx == 0:
            acc_ref[...] = part
        else:
            acc_ref[...] += part
    o_ref[...] = h + _rms(acc_ref[...], gout_ref[...])


def _ffn(h2, g_in, g_out, wg, wu, wd, *, tm, chunk):
    n, d = h2.shape
    const = lambda i: (0, 0)
    return pl.pallas_call(
        functools.partial(_ffn_kernel, chunk=chunk),
        out_shape=jax.ShapeDtypeStruct((n, d), F32),
        grid=(n // tm,),
        in_specs=[pl.BlockSpec((tm, d), lambda i: (i, 0)),
                  pl.BlockSpec((1, d), const),
                  pl.BlockSpec((1, d), const),
                  pl.BlockSpec(wg.shape, const),
                  pl.BlockSpec(wu.shape, const),
                  pl.BlockSpec(wd.shape, const)],
        out_specs=pl.BlockSpec((tm, d), lambda i: (i, 0)),
        scratch_shapes=[pltpu.VMEM((tm, d), F32)],
        compiler_params=_cparams("parallel"),
        name="swiglu_ffn",
    )(h2, g_in, g_out, wg, wu, wd)


def _pool_kernel(h_ref, halo_ref, gin_ref, gout_ref, w_ref, sc_ref, o_ref, ext_ref, *, tm, halo, tiles_per_seq):
    i = pl.program_id(0)
    first = (i % tiles_per_seq) == 0
    h = h_ref[...]
    u = _rms(h, gin_ref[...])
    uh = _rms(halo_ref[...], gin_ref[...])
    ext_ref[0:halo, :] = jnp.where(first, 0.0, uh)
    ext_ref[halo:, :] = u
    t = (i % tiles_per_seq) * tm + lax.broadcasted_iota(jnp.int32, (tm, 1), 0)
    group = h.shape[1] // len(POOL_WINDOWS)
    ys = []
    for gi, w in enumerate(POOL_WINDOWS):
        cols = slice(gi * group, (gi + 1) * group)
        total = u[:, cols]
        for j in range(1, w):
            total = total + ext_ref[halo - j:halo - j + tm, cols]
        count = jnp.minimum(t + 1, w).astype(F32)
        pooled = total / count - u[:, cols]
        ys.append(_dot(pooled.astype(BF16), w_ref[gi]))
    y = jnp.concatenate(ys, axis=-1) * sc_ref[...]
    o_ref[...] = h + _rms(y, gout_ref[...])


def _pool_mixer(h2, g_in, g_out, w_groups, scale, *, seq, tm):
    n, d = h2.shape
    halo = max(POOL_WINDOWS)
    tiles_per_seq = seq // tm
    ratio = tm // halo
    const = lambda i: (0, 0)
    return pl.pallas_call(
        functools.partial(_pool_kernel, tm=tm, halo=halo, tiles_per_seq=tiles_per_seq),
        out_shape=jax.ShapeDtypeStruct((n, d), F32),
        grid=(n // tm,),
        in_specs=[pl.BlockSpec((tm, d), lambda i: (i, 0)),
                  pl.BlockSpec((halo, d), lambda i: (jnp.maximum(i * ratio - 1, 0), 0)),
                  pl.BlockSpec((1, d), const),
                  pl.BlockSpec((1, d), const),
                  pl.BlockSpec(w_groups.shape, lambda i: (0, 0, 0)),
                  pl.BlockSpec((1, d), const)],
        out_specs=pl.BlockSpec((tm, d), lambda i: (i, 0)),
        scratch_shapes=[pltpu.VMEM((tm + halo, d), F32)],
        compiler_params=_cparams("parallel"),
        name="pool_mixer",
    )(h2, h2, g_in, g_out, w_groups, scale)


def _pack_in_weights(w_in):
    d = w_in.shape[0]
    hw = FOX_HEADS * HEAD_DIM
    kv = NSA_GROUPS * HEAD_DIM
    sizes = (hw, hw, hw, FOX_HEADS, NSA_HEADS * HEAD_DIM, kv, kv, kv, kv, kv, kv, 3 * NSA_HEADS)
    offs = np.concatenate([[0], np.cumsum(sizes)])
    fq, fk, fv, ff, nq, kc, vc, ks, vs, kw, vw, ng = [w_in[:, offs[j]:offs[j + 1]] for j in range(len(sizes))]
    scale = HEAD_DIM ** -0.5

    def dup(w):
        w = w.reshape(d, NSA_GROUPS, 1, HEAD_DIM)
        return jnp.broadcast_to(w, (d, NSA_GROUPS, 2, HEAD_DIM)).reshape(d, NSA_GROUPS * PAIR)

    w_main = jnp.concatenate([fq * scale, fk, fv, nq * scale, dup(ks), dup(vs), dup(kw), dup(vw)], axis=1)
    w_cmp = jnp.concatenate([kc, vc], axis=1)
    pad = jnp.zeros((d, LANES - FOX_HEADS - 3 * NSA_HEADS), F32)
    w_small = jnp.concatenate([ff, ng, pad], axis=1)
    return w_main.astype(BF16), w_cmp.astype(BF16), w_small


def _mixer_fox_nsa(h2, g_in, g_out, w_in, b_f, cmp_pe, cmp_w1, cmp_w2, w_out, *, batch, seq):
    tile = 512 if seq % 512 == 0 else seq
    w_main, w_cmp, w_small = _pack_in_weights(w_in)
    pj, pc, small = _inproj(h2, g_in, w_main, w_cmp, w_small, tm=tile)
    npair = FOX_HEADS // 2
    q_blk, k_blk, v_blk, nq_blk = 0, npair, 2 * npair, 3 * npair
    ks_blk = nq_blk + NSA_HEADS // 2
    vs_blk, kw_blk, vw_blk = ks_blk + NSA_GROUPS, ks_blk + 2 * NSA_GROUPS, ks_blk + 3 * NSA_GROUPS

    c = _forget_cumsum(small, b_f, batch=batch, seq=seq, tc=tile)
    nq = seq // tile
    c_tiles = c.reshape(batch, npair, 2, nq, tile).transpose(0, 1, 3, 2, 4)
    o_fox = _fox_attention(pj, c_tiles, batch=batch, seq=seq, tile=tile,
                           q_blk=q_blk, k_blk=k_blk, v_blk=v_blk)

    nch = seq // CMP_STRIDE
    ns = seq // SLC_BLOCK
    n_sel = min(N_SELECT, ns)
    kv = NSA_GROUPS * HEAD_DIM
    tch = pc.reshape(batch, nch, CMP_STRIDE, 2, NSA_GROUPS, HEAD_DIM)
    tch = tch.transpose(0, 3, 4, 1, 2, 5).reshape(batch, 2, NSA_GROUPS, nch, CMP_STRIDE * HEAD_DIM)
    pe8 = jnp.broadcast_to(cmp_pe.reshape(2, 1, CMP_BLOCK * HEAD_DIM), (2, 8, CMP_BLOCK * HEAD_DIM)).astype(BF16)
    w2dup = jnp.concatenate([cmp_w2, cmp_w2], axis=-1).astype(BF16)
    cmp_kv = _compress(tch, pe8, cmp_w1.astype(BF16), w2dup)

    ratio = SLC_BLOCK // CMP_STRIDE
    r = CMP_BLOCK // CMP_STRIDE
    n_idx = np.arange(nch)[:, None]
    b_idx = np.arange(ns)[None, :]
    pool_np = ((n_idx >= ratio * b_idx - (r - 1)) & (n_idx <= ratio * b_idx + ratio - 1)
               & (n_idx < nch - r + 1)).astype(np.float32)
    pool = jnp.asarray(pool_np, dtype=BF16)
    tq_c = 256 if seq % 256 == 0 else seq
    o_c, sel = _cmp_select(pj, cmp_kv, pool, batch=batch, seq=seq, tq=tq_c, q_blk=nq_blk // 2, n_sel=n_sel)
    o_nsa = _nsa_attention(pj, sel, o_c, small, batch=batch, seq=seq, tile=tile,
                           q_blk=nq_blk // 2, ks_blk=ks_blk, vs_blk=vs_blk, kw_blk=kw_blk, vw_blk=vw_blk)

    hw = FOX_HEADS * HEAD_DIM
    return _outproj(h2, o_fox, o_nsa, w_out[:hw].astype(BF16), w_out[hw:].astype(BF16), g_out, tm=tile)


def kernel(x, norm_g, attn_w_in, fox_b_f, nsa_cmp_pe, nsa_cmp_w1, nsa_cmp_w2, attn_w_out,
           pool_w, pool_scale, ffn_w_gate, ffn_w_up, ffn_w_down):
    batch, seq, d = x.shape
    depth = norm_g.shape[0]
    tm = 512 if seq % 512 == 0 else seq
    h = x.reshape(batch * seq, d)
    for layer in range(depth):
        g = norm_g[layer].reshape(4, 1, d)
        i = layer // 2
        if layer % 2 == 0:
            h = _mixer_fox_nsa(h, g[0], g[1], attn_w_in[i], fox_b_f[i], nsa_cmp_pe[i], nsa_cmp_w1[i],
                               nsa_cmp_w2[i], attn_w_out[i], batch=batch, seq=seq)
        else:
            h = _pool_mixer(h, g[0], g[1], pool_w[i].astype(BF16), pool_scale[i].reshape(1, d), seq=seq, tm=tm)
        h = _ffn(h, g[2], g[3], ffn_w_gate[layer].astype(BF16), ffn_w_up[layer].astype(BF16),
                 ffn_w_down[layer].astype(BF16), tm=tm, chunk=256)
    return h.reshape(batch, seq, d)
```

```python
import functools

import numpy as np
import jax
import jax.numpy as jnp
from jax import lax
from jax.experimental import pallas as pl
from jax.experimental.pallas import tpu as pltpu

F32 = jnp.float32
BF16 = jnp.bfloat16

HEAD_DIM = 64
FOX_HEADS = 8
NSA_HEADS = 8
NSA_GROUPS = 2
NSA_GROUP_SIZE = NSA_HEADS // NSA_GROUPS
CMP_BLOCK = 32
CMP_STRIDE = 16
SLC_BLOCK = 64
N_SELECT = 16
WINDOW = 512
POOL_WINDOWS = (2, 4, 8, 16)
RMS_EPS = 1e-6
NEG_INF = -1e30
BIG = 1e30
REMOVED = -3e38
LANES = 128
PAIR = 2 * HEAD_DIM
VMEM_LIMIT = 56 * 1024 * 1024

LOG2E = 1.4426950408889634
ALIBI_SLOPES_LOG2 = tuple(float(2.0 ** (-8.0 * (i + 1.0) / NSA_HEADS)) * LOG2E for i in range(NSA_HEADS))


def _cparams(*sem):
    return pltpu.CompilerParams(dimension_semantics=sem, vmem_limit_bytes=VMEM_LIMIT)


def _rms(x, g):
    return x * lax.rsqrt(jnp.mean(x * x, axis=-1, keepdims=True) + RMS_EPS) * g


def _dot(a, b):
    return jnp.dot(a, b, preferred_element_type=F32)


def _dot_nt(a, b):
    return lax.dot_general(a, b, (((1,), (1,)), ((), ())), preferred_element_type=F32)


def _inproj_kernel(x_ref, g_ref, w_ref, wc_ref, ws_ref, o_ref, oc_ref, os_ref, *, col_chunk):
    u = _rms(x_ref[...], g_ref[...])
    ub = u.astype(BF16)
    for c in range(0, o_ref.shape[1], col_chunk):
        o_ref[:, c:c + col_chunk] = _dot(ub, w_ref[:, c:c + col_chunk]).astype(BF16)
    oc_ref[...] = _dot(ub, wc_ref[...]).astype(BF16)
    os_ref[...] = jnp.dot(u, ws_ref[...], preferred_element_type=F32,
                          precision=lax.Precision.HIGHEST)


def _inproj(x2, g, w_main, w_cmp, w_small, *, tm):
    n, d = x2.shape
    cm, cc, cs = w_main.shape[1], w_cmp.shape[1], w_small.shape[1]
    const = lambda i: (0, 0)
    return pl.pallas_call(
        functools.partial(_inproj_kernel, col_chunk=512),
        out_shape=(jax.ShapeDtypeStruct((n, cm), BF16),
                   jax.ShapeDtypeStruct((n, cc), BF16),
                   jax.ShapeDtypeStruct((n, cs), F32)),
        grid=(n // tm,),
        in_specs=[pl.BlockSpec((tm, d), lambda i: (i, 0)),
                  pl.BlockSpec((1, d), const),
                  pl.BlockSpec((d, cm), const),
                  pl.BlockSpec((d, cc), const),
                  pl.BlockSpec((d, cs), const)],
        out_specs=(pl.BlockSpec((tm, cm), lambda i: (i, 0)),
                   pl.BlockSpec((tm, cc), lambda i: (i, 0)),
                   pl.BlockSpec((tm, cs), lambda i: (i, 0))),
        compiler_params=_cparams("parallel"),
        name="inproj",
    )(x2, g, w_main, w_cmp, w_small)


def _forget_cumsum_kernel(sm_ref, bf_ref, tri_ref, o_ref, carry_ref):
    @pl.when(pl.program_id(1) == 0)
    def _():
        carry_ref[...] = jnp.zeros_like(carry_ref)

    zt = sm_ref[...].T
    z = zt[0:FOX_HEADS, :] + bf_ref[...]
    log_f = jnp.minimum(z, 0.0) - jnp.log1p(jnp.exp(-jnp.abs(z)))
    c = jnp.dot(log_f, tri_ref[...], preferred_element_type=F32,
                precision=lax.Precision.HIGHEST) + carry_ref[...]
    o_ref[...] = c * LOG2E
    carry_ref[...] = c[:, c.shape[1] - 1:]


def _forget_cumsum(small, b_f, *, batch, seq, tc):
    tri = jnp.asarray(np.triu(np.ones((tc, tc), np.float32)))
    nchunk = seq // tc
    return pl.pallas_call(
        _forget_cumsum_kernel,
        out_shape=jax.ShapeDtypeStruct((batch, FOX_HEADS, seq), F32),
        grid=(batch, nchunk),
        in_specs=[pl.BlockSpec((tc, small.shape[1]), lambda b, j: (b * nchunk + j, 0)),
                  pl.BlockSpec((FOX_HEADS, 1), lambda b, j: (0, 0)),
                  pl.BlockSpec((tc, tc), lambda b, j: (0, 0))],
        out_specs=pl.BlockSpec((None, FOX_HEADS, tc), lambda b, j: (b, 0, j)),
        scratch_shapes=[pltpu.VMEM((FOX_HEADS, 1), F32)],
        compiler_params=_cparams("parallel", "arbitrary"),
        name="forget_cumsum",
    )(small, b_f.reshape(FOX_HEADS, 1), tri)


def _lane_is_low():
    return lax.broadcasted_iota(jnp.int32, (1, PAIR), 1) < HEAD_DIM


def _flash_step(s, v_tile, m_ref, l_ref, acc_ref):
    m_old = m_ref[...]
    m_new = jnp.maximum(m_old, jnp.max(s, axis=-1, keepdims=True))
    alpha = jnp.exp2(m_old - m_new)
    p = jnp.exp2(s - pltpu.repeat(m_new, s.shape[1] // LANES, axis=1))
    l_ref[...] = alpha * l_ref[...] + jnp.sum(p, axis=-1, keepdims=True)
    acc_ref[...] = alpha * acc_ref[...] + _dot(p.astype(BF16), v_tile)
    m_ref[...] = m_new


def _flash_reset(m_ref, l_ref, acc_ref):
    m_ref[...] = jnp.full_like(m_ref, NEG_INF)
    l_ref[...] = jnp.zeros_like(l_ref)
    acc_ref[...] = jnp.zeros_like(acc_ref)


def _split_pair(q):
    low = _lane_is_low()
    zero = jnp.zeros_like(q)
    return jnp.where(low, q, zero), jnp.where(low, zero, q)


def _fox_kernel(q_ref, k_ref, v_ref, c_ref, o_ref, m_ref, l_ref, acc_ref, *, tile):
    i = pl.program_id(2)
    qh = _split_pair(q_ref[...])
    _flash_reset(m_ref, l_ref, acc_ref)

    def step(j, mask):
        ks = pl.multiple_of(j * tile, tile)
        kt = k_ref[pl.ds(ks, tile), :]
        vt = v_ref[pl.ds(ks, tile), :]
        for h in range(2):
            s = _dot_nt(qh[h], kt) - c_ref[j, h:h + 1, :]
            if mask is not None:
                s = jnp.where(mask, s, NEG_INF)
            _flash_step(s, vt, m_ref.at[h], l_ref.at[h], acc_ref.at[h])

    def body(j, carry):
        step(j, None)
        return carry

    lax.fori_loop(0, i, body, 0)
    row = lax.broadcasted_iota(jnp.int32, (tile, tile), 0)
    col = lax.broadcasted_iota(jnp.int32, (tile, tile), 1)
    step(i, col <= row)
    o_ref[...] = jnp.where(_lane_is_low(), acc_ref[0] / l_ref[0], acc_ref[1] / l_ref[1]).astype(o_ref.dtype)


def _fox_attention(pj, c_tiles, *, batch, seq, tile, q_blk, k_blk, v_blk):
    nq = seq // tile
    npair = FOX_HEADS // 2
    return pl.pallas_call(
        functools.partial(_fox_kernel, tile=tile),
        out_shape=jax.ShapeDtypeStruct((batch * seq, FOX_HEADS * HEAD_DIM), BF16),
        grid=(batch, npair, nq),
        in_specs=[pl.BlockSpec((tile, PAIR), lambda b, p, i: (b * nq + i, q_blk + p)),
                  pl.BlockSpec((seq, PAIR), lambda b, p, i: (b, k_blk + p)),
                  pl.BlockSpec((seq, PAIR), lambda b, p, i: (b, v_blk + p)),
                  pl.BlockSpec((None, None, nq, 2, tile), lambda b, p, i: (b, p, 0, 0, 0))],
        out_specs=pl.BlockSpec((tile, PAIR), lambda b, p, i: (b * nq + i, p)),
        scratch_shapes=[pltpu.VMEM((2, tile, LANES), F32), pltpu.VMEM((2, tile, LANES), F32),
                        pltpu.VMEM((2, tile, PAIR), F32)],
        compiler_params=_cparams("parallel", "parallel", "arbitrary"),
        name="fox_attention",
    )(pj, pj, pj, c_tiles)


def _gelu_tanh(x):
    return 0.5 * x * (1.0 + jnp.tanh(0.7978845608028654 * (x + 0.044715 * x * x * x)))


def _compress_kernel(t_ref, pe_ref, w1_ref, w2_ref, o_ref):
    half = w1_ref.shape[0] // 2
    t = t_ref[...]
    a = _dot(t, w1_ref[0:half, :])
    b = _dot(t, w1_ref[half:, :])
    pe_term = _dot(pe_ref[...], w1_ref[...])[0:1, :]
    nch = t.shape[0]
    h = a + pltpu.roll(b, nch - 1, axis=0) + pe_term
    o_ref[...] = _dot(_gelu_tanh(h).astype(BF16), w2_ref[...]).astype(o_ref.dtype)


def _compress(tchunks, pe8, w1, w2dup):
    b, two, g, nch, width = tchunks.shape
    return pl.pallas_call(
        _compress_kernel,
        out_shape=jax.ShapeDtypeStruct((b, two, g, nch, PAIR), BF16),
        grid=(b, two, g),
        in_specs=[pl.BlockSpec((None, None, None, nch, width), lambda bb, kv, gg: (bb, kv, gg, 0, 0)),
                  pl.BlockSpec((None, 8, 2 * width), lambda bb, kv, gg: (kv, 0, 0)),
                  pl.BlockSpec((None, 2 * width, HEAD_DIM), lambda bb, kv, gg: (kv, 0, 0)),
                  pl.BlockSpec((None, HEAD_DIM, PAIR), lambda bb, kv, gg: (kv, 0, 0))],
        out_specs=pl.BlockSpec((None, None, None, nch, PAIR), lambda bb, kv, gg: (bb, kv, gg, 0, 0)),
        compiler_params=_cparams("parallel", "parallel", "parallel"),
        name="nsa_compress",
    )(tchunks, pe8, w1, w2dup)


def _group_slope(g, r):
    return jnp.where(g == 0, ALIBI_SLOPES_LOG2[r], ALIBI_SLOPES_LOG2[NSA_GROUP_SIZE + r]).astype(F32)


def _split3(x):
    hi = x.astype(BF16)
    r1 = x - hi.astype(F32)
    mid = r1.astype(BF16)
    lo = (r1 - mid.astype(F32)).astype(BF16)
    return hi, mid, lo


def _cmp_select_kernel(q_ref, kc_ref, vc_ref, pool_ref, tmap_ref, oc_ref, sel_ref, flag_ref, *, tq, n_sel):
    g = pl.program_id(1)
    t0 = pl.program_id(2) * tq
    nch = kc_ref.shape[0]
    ns = pool_ref.shape[1]
    low = _lane_is_low()
    t = t0 + lax.broadcasted_iota(jnp.int32, (tq, 1), 0)
    cmp_end = lax.broadcasted_iota(jnp.int32, (1, nch), 1) * CMP_STRIDE + (CMP_BLOCK - 1)
    d = t - cmp_end
    valid = d >= 0
    df = d.astype(F32)
    kc = kc_ref[...]
    vc = vc_ref[...]
    imp = jnp.zeros((tq, nch), F32)
    for pair in range(NSA_GROUP_SIZE // 2):
        qh = _split_pair(q_ref[:, pair * PAIR:(pair + 1) * PAIR])
        outs = []
        for h in range(2):
            s = _dot_nt(qh[h], kc) - _group_slope(g, 2 * pair + h) * df
            s = jnp.where(valid, s, NEG_INF)
            m = jnp.max(s, axis=-1, keepdims=True)
            e = jnp.where(valid, jnp.exp2(s - m), 0.0)
            p = e / jnp.maximum(jnp.sum(e, axis=-1, keepdims=True), 1e-30)
            outs.append(_dot(p.astype(BF16), vc))
            imp = imp + p
        oc_ref[:, pair * PAIR:(pair + 1) * PAIR] = jnp.where(low, outs[0], outs[1])

    pool = pool_ref[...]
    hi, mid, lo = _split3(imp)
    p_slc = _dot(hi, pool) + _dot(mid, pool) + _dot(lo, pool)

    blk = lax.broadcasted_iota(jnp.int32, (1, ns), 1)
    cur = t // SLC_BLOCK
    forced = (blk == 0) | (blk == cur) | (blk == cur - 1)
    val = jnp.where(forced, BIG, jnp.where(blk > cur, NEG_INF, p_slc))
    sel = jnp.zeros((tq, ns), F32)
    for _ in range(n_sel):
        mx = jnp.max(val, axis=-1, keepdims=True)
        pick = jnp.min(jnp.where(val == mx, blk, ns), axis=-1, keepdims=True)
        hit = blk == pick
        sel = jnp.where(hit, 1.0, sel)
        val = jnp.where(hit, REMOVED, val)
    sel_ref[...] = sel
    any_sel = jnp.broadcast_to(jnp.max(sel, axis=0, keepdims=True), (8, ns)).astype(BF16)
    flag_ref[...] = (_dot(any_sel, tmap_ref[...]) > 0.5).astype(jnp.int32)


def _cmp_select(pj, cmp_kv, pool, tmap, *, batch, seq, tq, q_blk, n_sel):
    nq = seq // tq
    nch = cmp_kv.shape[3]
    ns = pool.shape[1]
    gw = NSA_GROUP_SIZE * HEAD_DIM
    return pl.pallas_call(
        functools.partial(_cmp_select_kernel, tq=tq, n_sel=n_sel),
        out_shape=(jax.ShapeDtypeStruct((batch * seq, NSA_HEADS * HEAD_DIM), F32),
                   jax.ShapeDtypeStruct((batch, NSA_GROUPS, seq, ns), F32),
                   jax.ShapeDtypeStruct((batch, NSA_GROUPS, nq, 8, LANES), jnp.int32)),
        grid=(batch, NSA_GROUPS, nq),
        in_specs=[pl.BlockSpec((tq, gw), lambda b, g, i: (b * nq + i, q_blk + g)),
                  pl.BlockSpec((None, None, None, nch, PAIR), lambda b, g, i: (b, 0, g, 0, 0)),
                  pl.BlockSpec((None, None, None, nch, PAIR), lambda b, g, i: (b, 1, g, 0, 0)),
                  pl.BlockSpec((nch, ns), lambda b, g, i: (0, 0)),
                  pl.BlockSpec((ns, LANES), lambda b, g, i: (0, 0))],
        out_specs=(pl.BlockSpec((tq, gw), lambda b, g, i: (b * nq + i, g)),
                   pl.BlockSpec((None, None, tq, ns), lambda b, g, i: (b, g, i, 0)),
                   pl.BlockSpec((None, None, None, 8, LANES), lambda b, g, i: (b, g, i, 0, 0))),
        compiler_params=_cparams("parallel", "parallel", "parallel"),
        name="nsa_cmp_select",
    )(pj, cmp_kv, cmp_kv, pool, tmap)


def _nsa_kernel(flags_ref, q_ref, ks_ref, vs_ref, kwp_ref, vwp_ref, kwd_ref, vwd_ref, sel_ref, oc_ref, gate_ref,
                o_ref, m_ref, l_ref, acc_ref, *, tile, nq):
    b = pl.program_id(0)
    g = pl.program_id(1)
    i = pl.program_id(2)
    nh = NSA_GROUP_SIZE
    ns = sel_ref.shape[1]
    bpt = tile // SLC_BLOCK
    low = _lane_is_low()
    row = lax.broadcasted_iota(jnp.int32, (tile, tile), 0)
    col = lax.broadcasted_iota(jnp.int32, (tile, tile), 1)
    relpos = lax.broadcasted_iota(jnp.int32, (1, tile), 1).astype(F32)
    sel = sel_ref[...].astype(BF16)
    blk_of_key = lax.broadcasted_iota(jnp.int32, (ns, tile), 1) // SLC_BLOCK
    blk_row = lax.broadcasted_iota(jnp.int32, (ns, tile), 0)
    qh = []
    for pair in range(nh // 2):
        qh.extend(_split_pair(q_ref[:, pair * PAIR:(pair + 1) * PAIR]))
    slopes = [_group_slope(g, r) for r in range(nh)]
    _flash_reset(m_ref, l_ref, acc_ref)

    def alibi(r, j):
        return slopes[r] * (relpos + ((j - i) * tile).astype(F32))

    def selected(j):
        expand = jnp.where(blk_row == blk_of_key + j * bpt, 1.0, 0.0).astype(BF16)
        return _dot(sel, expand) > 0.5

    def attend(slot, k_tile, v_tile, j, bias):
        for r in range(nh):
            s = _dot_nt(qh[r], k_tile) + alibi(r, j) + bias
            _flash_step(s, v_tile, m_ref.at[slot + r], l_ref.at[slot + r], acc_ref.at[slot + r])

    def sel_step(j, keep):
        ks = pl.multiple_of(j * tile, tile)
        attend(0, ks_ref[pl.ds(ks, tile), :], vs_ref[pl.ds(ks, tile), :], j, jnp.where(keep, 0.0, NEG_INF))

    sel_step(i, selected(i) & (col <= row))

    def sel_body(j, carry):
        @pl.when(flags_ref[((b * NSA_GROUPS + g) * nq + i) * nq + j] != 0)
        def _():
            sel_step(j, selected(j))
        return carry

    lax.fori_loop(0, i, sel_body, 0)

    attend(nh, kwd_ref[...], vwd_ref[...], i, jnp.where(col <= row, 0.0, NEG_INF))

    @pl.when(i > 0)
    def _():
        attend(nh, kwp_ref[...], vwp_ref[...], i - 1, jnp.where(col > row, 0.0, NEG_INF))

    gates = jax.nn.sigmoid(gate_ref[...])

    def gate_col(r, branch):
        c0 = FOX_HEADS + 3 * r + branch
        c1 = c0 + 3 * nh
        return jnp.where(g == 0, gates[:, c0:c0 + 1], gates[:, c1:c1 + 1])

    for pair in range(nh // 2):
        o_c = oc_ref[:, pair * PAIR:(pair + 1) * PAIR]
        outs = []
        for h in range(2):
            r = 2 * pair + h
            o_s = acc_ref[r] / l_ref[r]
            o_w = acc_ref[nh + r] / l_ref[nh + r]
            outs.append(gate_col(r, 0) * o_c + gate_col(r, 1) * o_s + gate_col(r, 2) * o_w)
        o_ref[:, pair * PAIR:(pair + 1) * PAIR] = jnp.where(low, outs[0], outs[1]).astype(o_ref.dtype)


def _nsa_attention(flags, pj, sel, o_c, small, *, batch, seq, tile, q_blk, ks_blk, vs_blk, kw_blk, vw_blk):
    nq = seq // tile
    ns = sel.shape[3]
    gw = NSA_GROUP_SIZE * HEAD_DIM
    slots = 2 * NSA_GROUP_SIZE
    cur = lambda b, g, i, f, base: (b * nq + i, base + g)
    prev = lambda b, g, i, f, base: (b * nq + jnp.maximum(i - 1, 0), base + g)
    grid_spec = pltpu.PrefetchScalarGridSpec(
        num_scalar_prefetch=1,
        grid=(batch, NSA_GROUPS, nq),
        in_specs=[pl.BlockSpec((tile, gw), functools.partial(cur, base=q_blk)),
                  pl.BlockSpec((seq, PAIR), lambda b, g, i, f: (b, ks_blk + g)),
                  pl.BlockSpec((seq, PAIR), lambda b, g, i, f: (b, vs_blk + g)),
                  pl.BlockSpec((tile, PAIR), functools.partial(prev, base=kw_blk)),
                  pl.BlockSpec((tile, PAIR), functools.partial(prev, base=vw_blk)),
                  pl.BlockSpec((tile, PAIR), functools.partial(cur, base=kw_blk)),
                  pl.BlockSpec((tile, PAIR), functools.partial(cur, base=vw_blk)),
                  pl.BlockSpec((None, None, tile, ns), lambda b, g, i, f: (b, g, i, 0)),
                  pl.BlockSpec((tile, gw), functools.partial(cur, base=0)),
                  pl.BlockSpec((tile, small.shape[1]), lambda b, g, i, f: (b * nq + i, 0))],
        out_specs=pl.BlockSpec((tile, gw), functools.partial(cur, base=0)),
        scratch_shapes=[pltpu.VMEM((slots, tile, LANES), F32), pltpu.VMEM((slots, tile, LANES), F32),
                        pltpu.VMEM((slots, tile, PAIR), F32)])
    return pl.pallas_call(
        functools.partial(_nsa_kernel, tile=tile, nq=nq),
        out_shape=jax.ShapeDtypeStruct((batch * seq, NSA_HEADS * HEAD_DIM), BF16),
        grid_spec=grid_spec,
        compiler_params=_cparams("parallel", "parallel", "arbitrary"),
        name="nsa_attention",
    )(flags, pj, pj, pj, pj, pj, pj, pj, sel, o_c, small)


def _outproj_kernel(h_ref, a_ref, b_ref, wa_ref, wb_ref, g_ref, o_ref):
    m = _dot(a_ref[...], wa_ref[...]) + _dot(b_ref[...], wb_ref[...])
    o_ref[...] = h_ref[...] + _rms(m, g_ref[...])


def _outproj(h2, a, b, wa, wb, g, *, tm):
    n, d = h2.shape
    const = lambda i: (0, 0)
    return pl.pallas_call(
        _outproj_kernel,
        out_shape=jax.ShapeDtypeStruct((n, d), F32),
        grid=(n // tm,),
        in_specs=[pl.BlockSpec((tm, d), lambda i: (i, 0)),
                  pl.BlockSpec((tm, a.shape[1]), lambda i: (i, 0)),
                  pl.BlockSpec((tm, b.shape[1]), lambda i: (i, 0)),
                  pl.BlockSpec(wa.shape, const),
                  pl.BlockSpec(wb.shape, const),
                  pl.BlockSpec((1, d), const)],
        out_specs=pl.BlockSpec((tm, d), lambda i: (i, 0)),
        compiler_params=_cparams("parallel"),
        name="attn_outproj",
    )(h2, a, b, wa, wb, g)


def _ffn_kernel(h_ref, gin_ref, gout_ref, wg_ref, wu_ref, wd_ref, o_ref, acc_ref, *, chunk):
    h = h_ref[...]
    ub = _rms(h, gin_ref[...]).astype(BF16)
    hidden = wg_ref.shape[1]
    for idx, c in enumerate(range(0, hidden, chunk)):
        gate = _dot(ub, wg_ref[:, c:c + chunk])
        up = _dot(ub, wu_ref[:, c:c + chunk])
        act = (gate * jax.nn.sigmoid(gate) * up).astype(BF16)
        part = _dot(act, wd_ref[c:c + chunk, :])
        if idx == 0:
            acc_ref[...] = part
        else:
            acc_ref[...] += part
    o_ref[...] = h + _rms(acc_ref[...], gout_ref[...])


def _ffn(h2, g_in, g_out, wg, wu, wd, *, tm, chunk):
    n, d = h2.shape
    const = lambda i: (0, 0)
    return pl.pallas_call(
        functools.partial(_ffn_kernel, chunk=chunk),
        out_shape=jax.ShapeDtypeStruct((n, d), F32),
        grid=(n // tm,),
        in_specs=[pl.BlockSpec((tm, d), lambda i: (i, 0)),
                  pl.BlockSpec((1, d), const),
                  pl.BlockSpec((1, d), const),
                  pl.BlockSpec(wg.shape, const),
                  pl.BlockSpec(wu.shape, const),
                  pl.BlockSpec(wd.shape, const)],
        out_specs=pl.BlockSpec((tm, d), lambda i: (i, 0)),
        scratch_shapes=[pltpu.VMEM((tm, d), F32)],
        compiler_params=_cparams("parallel"),
        name="swiglu_ffn",
    )(h2, g_in, g_out, wg, wu, wd)


def _pool_kernel(h_ref, halo_ref, gin_ref, gout_ref, w_ref, sc_ref, o_ref, ext_ref, *, tm, halo, tiles_per_seq):
    i = pl.program_id(0)
    first = (i % tiles_per_seq) == 0
    h = h_ref[...]
    u = _rms(h, gin_ref[...])
    uh = _rms(halo_ref[...], gin_ref[...])
    ext_ref[0:halo, :] = jnp.where(first, 0.0, uh)
    ext_ref[halo:, :] = u
    t = (i % tiles_per_seq) * tm + lax.broadcasted_iota(jnp.int32, (tm, 1), 0)
    group = h.shape[1] // len(POOL_WINDOWS)
    ys = []
    for gi, w in enumerate(POOL_WINDOWS):
        cols = slice(gi * group, (gi + 1) * group)
        total = u[:, cols]
        for j in range(1, w):
            total = total + ext_ref[halo - j:halo - j + tm, cols]
        count = jnp.minimum(t + 1, w).astype(F32)
        pooled = total / count - u[:, cols]
        ys.append(_dot(pooled.astype(BF16), w_ref[gi]))
    y = jnp.concatenate(ys, axis=-1) * sc_ref[...]
    o_ref[...] = h + _rms(y, gout_ref[...])


def _pool_mixer(h2, g_in, g_out, w_groups, scale, *, seq, tm):
    n, d = h2.shape
    halo = max(POOL_WINDOWS)
    tiles_per_seq = seq // tm
    ratio = tm // halo
    const = lambda i: (0, 0)
    return pl.pallas_call(
        functools.partial(_pool_kernel, tm=tm, halo=halo, tiles_per_seq=tiles_per_seq),
        out_shape=jax.ShapeDtypeStruct((n, d), F32),
        grid=(n // tm,),
        in_specs=[pl.BlockSpec((tm, d), lambda i: (i, 0)),
                  pl.BlockSpec((halo, d), lambda i: (jnp.maximum(i * ratio - 1, 0), 0)),
                  pl.BlockSpec((1, d), const),
                  pl.BlockSpec((1, d), const),
                  pl.BlockSpec(w_groups.shape, lambda i: (0, 0, 0)),
                  pl.BlockSpec((1, d), const)],
        out_specs=pl.BlockSpec((tm, d), lambda i: (i, 0)),
        scratch_shapes=[pltpu.VMEM((tm + halo, d), F32)],
        compiler_params=_cparams("parallel"),
        name="pool_mixer",
    )(h2, h2, g_in, g_out, w_groups, scale)


def _pack_in_weights(w_in):
    d = w_in.shape[0]
    hw = FOX_HEADS * HEAD_DIM
    kv = NSA_GROUPS * HEAD_DIM
    sizes = (hw, hw, hw, FOX_HEADS, NSA_HEADS * HEAD_DIM, kv, kv, kv, kv, kv, kv, 3 * NSA_HEADS)
    offs = np.concatenate([[0], np.cumsum(sizes)])
    fq, fk, fv, ff, nq, kc, vc, ks, vs, kw, vw, ng = [w_in[:, offs[j]:offs[j + 1]] for j in range(len(sizes))]
    scale = HEAD_DIM ** -0.5 * LOG2E

    def dup(w):
        w = w.reshape(d, NSA_GROUPS, 1, HEAD_DIM)
        return jnp.broadcast_to(w, (d, NSA_GROUPS, 2, HEAD_DIM)).reshape(d, NSA_GROUPS * PAIR)

    w_main = jnp.concatenate([fq * scale, fk, fv, nq * scale, dup(ks), dup(vs), dup(kw), dup(vw)], axis=1)
    w_cmp = jnp.concatenate([kc, vc], axis=1)
    pad = jnp.zeros((d, LANES - FOX_HEADS - 3 * NSA_HEADS), F32)
    w_small = jnp.concatenate([ff, ng, pad], axis=1)
    return w_main.astype(BF16), w_cmp.astype(BF16), w_small


def _mixer_fox_nsa(h2, g_in, g_out, w_in, b_f, cmp_pe, cmp_w1, cmp_w2, w_out, *, batch, seq):
    tile = WINDOW
    assert seq % tile == 0 and seq // tile <= LANES
    w_main, w_cmp, w_small = _pack_in_weights(w_in)
    pj, pc, small = _inproj(h2, g_in, w_main, w_cmp, w_small, tm=tile)
    npair = FOX_HEADS // 2
    q_blk, k_blk, v_blk, nq_blk = 0, npair, 2 * npair, 3 * npair
    ks_blk = nq_blk + NSA_HEADS // 2
    vs_blk, kw_blk, vw_blk = ks_blk + NSA_GROUPS, ks_blk + 2 * NSA_GROUPS, ks_blk + 3 * NSA_GROUPS

    c = _forget_cumsum(small, b_f, batch=batch, seq=seq, tc=tile)
    nq = seq // tile
    c_tiles = c.reshape(batch, npair, 2, nq, tile).transpose(0, 1, 3, 2, 4)
    o_fox = _fox_attention(pj, c_tiles, batch=batch, seq=seq, tile=tile,
                           q_blk=q_blk, k_blk=k_blk, v_blk=v_blk)

    nch = seq // CMP_STRIDE
    ns = seq // SLC_BLOCK
    n_sel = min(N_SELECT, ns)
    kv = NSA_GROUPS * HEAD_DIM
    tch = pc.reshape(batch, nch, CMP_STRIDE, 2, NSA_GROUPS, HEAD_DIM)
    tch = tch.transpose(0, 3, 4, 1, 2, 5).reshape(batch, 2, NSA_GROUPS, nch, CMP_STRIDE * HEAD_DIM)
    pe8 = jnp.broadcast_to(cmp_pe.reshape(2, 1, CMP_BLOCK * HEAD_DIM), (2, 8, CMP_BLOCK * HEAD_DIM)).astype(BF16)
    w2dup = jnp.concatenate([cmp_w2, cmp_w2], axis=-1).astype(BF16)
    cmp_kv = _compress(tch, pe8, cmp_w1.astype(BF16), w2dup)

    ratio = SLC_BLOCK // CMP_STRIDE
    r = CMP_BLOCK // CMP_STRIDE
    n_idx = np.arange(nch)[:, None]
    b_idx = np.arange(ns)[None, :]
    pool_np = ((n_idx >= ratio * b_idx - (r - 1)) & (n_idx <= ratio * b_idx + ratio - 1)
               & (n_idx < nch - r + 1)).astype(np.float32)
    pool = jnp.asarray(pool_np, dtype=BF16)
    tmap_np = (np.arange(ns)[:, None] // (tile // SLC_BLOCK) == np.arange(LANES)[None, :]).astype(np.float32)
    o_c, sel, flag_blocks = _cmp_select(pj, cmp_kv, pool, jnp.asarray(tmap_np, dtype=BF16), batch=batch, seq=seq,
                                        tq=tile, q_blk=nq_blk // 2, n_sel=n_sel)
    flags = flag_blocks[:, :, :, 0, :nq].reshape(-1)
    o_nsa = _nsa_attention(flags, pj, sel, o_c, small, batch=batch, seq=seq, tile=tile,
                           q_blk=nq_blk // 2, ks_blk=ks_blk, vs_blk=vs_blk, kw_blk=kw_blk, vw_blk=vw_blk)

    hw = FOX_HEADS * HEAD_DIM
    return _outproj(h2, o_fox, o_nsa, w_out[:hw].astype(BF16), w_out[hw:].astype(BF16), g_out, tm=tile)


def kernel(x, norm_g, attn_w_in, fox_b_f, nsa_cmp_pe, nsa_cmp_w1, nsa_cmp_w2, attn_w_out,
           pool_w, pool_scale, ffn_w_gate, ffn_w_up, ffn_w_down):
    batch, seq, d = x.shape
    depth = norm_g.shape[0]
    tm = 512 if seq % 512 == 0 else seq
    h = x.reshape(batch * seq, d)
    for layer in range(depth):
        g = norm_g[layer].reshape(4, 1, d)
        i = layer // 2
        if layer % 2 == 0:
            h = _mixer_fox_nsa(h, g[0], g[1], attn_w_in[i], fox_b_f[i], nsa_cmp_pe[i], nsa_cmp_w1[i],
                               nsa_cmp_w2[i], attn_w_out[i], batch=batch, seq=seq)
        else:
            h = _pool_mixer(h, g[0], g[1], pool_w[i].astype(BF16), pool_scale[i].reshape(1, d), seq=seq, tm=tm)
        h = _ffn(h, g[2], g[3], ffn_w_gate[layer].astype(BF16), ffn_w_up[layer].astype(BF16),
                 ffn_w_down[layer].astype(BF16), tm=tm, chunk=256)
    return h.reshape(batch, seq, d)
```

```python
import functools

import numpy as np
import jax
import jax.numpy as jnp
from jax import lax
from jax.experimental import pallas as pl
from jax.experimental.pallas import tpu as pltpu

F32 = jnp.float32
BF16 = jnp.bfloat16

HEAD_DIM = 64
FOX_HEADS = 8
NSA_HEADS = 8
NSA_GROUPS = 2
NSA_GROUP_SIZE = NSA_HEADS // NSA_GROUPS
CMP_BLOCK = 32
CMP_STRIDE = 16
SLC_BLOCK = 64
N_SELECT = 16
WINDOW = 512
POOL_WINDOWS = (2, 4, 8, 16)
RMS_EPS = 1e-6
NEG_INF = -1e30
BIG = 1e30
REMOVED = -3e38
LANES = 128
PAIR = 2 * HEAD_DIM
VMEM_LIMIT = 56 * 1024 * 1024
FOX_TQ, FOX_TK = 1024, 1024

LOG2E = 1.4426950408889634
ALIBI_SLOPES_LOG2 = tuple(float(2.0 ** (-8.0 * (i + 1.0) / NSA_HEADS)) * LOG2E for i in range(NSA_HEADS))


def _cparams(*sem):
    return pltpu.CompilerParams(dimension_semantics=sem, vmem_limit_bytes=VMEM_LIMIT)


def _rms(x, g):
    return x * lax.rsqrt(jnp.mean(x * x, axis=-1, keepdims=True) + RMS_EPS) * g


def _dot(a, b):
    return jnp.dot(a, b, preferred_element_type=F32)


def _dot_nt(a, b):
    return lax.dot_general(a, b, (((1,), (1,)), ((), ())), preferred_element_type=F32)


def _inproj_kernel(x_ref, g_ref, w_ref, wc_ref, ws_ref, o_ref, oc_ref, os_ref, *, col_chunk):
    u = _rms(x_ref[...], g_ref[...])
    ub = u.astype(BF16)
    for c in range(0, o_ref.shape[1], col_chunk):
        o_ref[:, c:c + col_chunk] = _dot(ub, w_ref[:, c:c + col_chunk]).astype(BF16)
    oc_ref[...] = _dot(ub, wc_ref[...]).astype(BF16)
    os_ref[...] = jnp.dot(u, ws_ref[...], preferred_element_type=F32,
                          precision=lax.Precision.HIGHEST)


def _inproj(x2, g, w_main, w_cmp, w_small, *, tm):
    n, d = x2.shape
    cm, cc, cs = w_main.shape[1], w_cmp.shape[1], w_small.shape[1]
    const = lambda i: (0, 0)
    return pl.pallas_call(
        functools.partial(_inproj_kernel, col_chunk=512),
        out_shape=(jax.ShapeDtypeStruct((n, cm), BF16),
                   jax.ShapeDtypeStruct((n, cc), BF16),
                   jax.ShapeDtypeStruct((n, cs), F32)),
        grid=(n // tm,),
        in_specs=[pl.BlockSpec((tm, d), lambda i: (i, 0)),
                  pl.BlockSpec((1, d), const),
                  pl.BlockSpec((d, cm), const),
                  pl.BlockSpec((d, cc), const),
                  pl.BlockSpec((d, cs), const)],
        out_specs=(pl.BlockSpec((tm, cm), lambda i: (i, 0)),
                   pl.BlockSpec((tm, cc), lambda i: (i, 0)),
                   pl.BlockSpec((tm, cs), lambda i: (i, 0))),
        compiler_params=_cparams("parallel"),
        name="inproj",
    )(x2, g, w_main, w_cmp, w_small)


def _forget_cumsum_kernel(sm_ref, bf_ref, tri_ref, o_ref, carry_ref):
    @pl.when(pl.program_id(1) == 0)
    def _():
        carry_ref[...] = jnp.zeros_like(carry_ref)

    zt = sm_ref[...].T
    z = zt[0:FOX_HEADS, :] + bf_ref[...]
    log_f = jnp.minimum(z, 0.0) - jnp.log1p(jnp.exp(-jnp.abs(z)))
    c = jnp.dot(log_f, tri_ref[...], preferred_element_type=F32,
                precision=lax.Precision.HIGHEST) + carry_ref[...]
    o_ref[...] = c * LOG2E
    carry_ref[...] = c[:, c.shape[1] - 1:]


def _forget_cumsum(small, b_f, *, batch, seq, tc):
    tri = jnp.asarray(np.triu(np.ones((tc, tc), np.float32)))
    nchunk = seq // tc
    return pl.pallas_call(
        _forget_cumsum_kernel,
        out_shape=jax.ShapeDtypeStruct((batch, FOX_HEADS, seq), F32),
        grid=(batch, nchunk),
        in_specs=[pl.BlockSpec((tc, small.shape[1]), lambda b, j: (b * nchunk + j, 0)),
                  pl.BlockSpec((FOX_HEADS, 1), lambda b, j: (0, 0)),
                  pl.BlockSpec((tc, tc), lambda b, j: (0, 0))],
        out_specs=pl.BlockSpec((None, FOX_HEADS, tc), lambda b, j: (b, 0, j)),
        scratch_shapes=[pltpu.VMEM((FOX_HEADS, 1), F32)],
        compiler_params=_cparams("parallel", "arbitrary"),
        name="forget_cumsum",
    )(small, b_f.reshape(FOX_HEADS, 1), tri)


def _lane_is_low():
    return lax.broadcasted_iota(jnp.int32, (1, PAIR), 1) < HEAD_DIM


def _with_ones(v_tile):
    return jnp.concatenate([v_tile, jnp.ones_like(v_tile)], axis=1)


def _flash_step(s, v, m_ref, acc_ref, l_ref=None):
    m_old = m_ref[...]
    m_new = jnp.maximum(m_old, jnp.max(s, axis=-1, keepdims=True))
    alpha = jnp.exp2(m_old - m_new)
    p = jnp.exp2(s - jnp.concatenate([m_new] * (s.shape[1] // LANES), axis=1))
    if l_ref is None:
        alpha_acc = jnp.concatenate([alpha, alpha], axis=1)
    else:
        alpha_acc = alpha
        l_ref[...] = alpha * l_ref[...] + jnp.sum(p, axis=-1, keepdims=True)
    acc_ref[...] = alpha_acc * acc_ref[...] + _dot(p.astype(BF16), v)
    m_ref[...] = m_new


def _flash_reset(m_ref, *sum_refs):
    m_ref[...] = jnp.full_like(m_ref, NEG_INF)
    for ref in sum_refs:
        ref[...] = jnp.zeros_like(ref)


def _flash_out(acc):
    return acc[:, :PAIR] / acc[:, PAIR:]


def _split_pair(q):
    low = _lane_is_low()
    zero = jnp.zeros_like(q)
    return jnp.where(low, q, zero), jnp.where(low, zero, q)


def _fox_kernel(q_ref, k_ref, v_ref, c_ref, o_ref, m_ref, l_ref, acc_ref, *, tq, tk):
    i = pl.program_id(2)
    qh = _split_pair(q_ref[...])
    _flash_reset(m_ref, l_ref, acc_ref)

    def step(j, mask):
        ks = pl.multiple_of(j * tk, tk)
        kt = k_ref[pl.ds(ks, tk), :]
        vt = v_ref[pl.ds(ks, tk), :]
        for h in range(2):
            s = _dot_nt(qh[h], kt) - c_ref[j, h:h + 1, :]
            if mask is not None:
                s = jnp.where(mask, s, NEG_INF)
            _flash_step(s, vt, m_ref.at[h], acc_ref.at[h], l_ref.at[h])

    def body(j, carry):
        step(j, None)
        return carry

    n_full = (i * tq) // tk
    lax.fori_loop(0, n_full, body, 0)
    row = lax.broadcasted_iota(jnp.int32, (tq, tk), 0)
    col = lax.broadcasted_iota(jnp.int32, (tq, tk), 1)
    for d in range(max(1, tq // tk)):
        j = n_full + d
        step(j, col + (j * tk - i * tq) <= row)
    o_ref[...] = jnp.where(_lane_is_low(), acc_ref[0] / l_ref[0], acc_ref[1] / l_ref[1]).astype(o_ref.dtype)


def _fox_attention(pj, c_tiles, *, batch, seq, tq, tk, q_blk, k_blk, v_blk):
    nq = seq // tq
    nk = seq // tk
    npair = FOX_HEADS // 2
    return pl.pallas_call(
        functools.partial(_fox_kernel, tq=tq, tk=tk),
        out_shape=jax.ShapeDtypeStruct((batch * seq, FOX_HEADS * HEAD_DIM), BF16),
        grid=(batch, npair, nq),
        in_specs=[pl.BlockSpec((tq, PAIR), lambda b, p, i: (b * nq + i, q_blk + p)),
                  pl.BlockSpec((seq, PAIR), lambda b, p, i: (b, k_blk + p)),
                  pl.BlockSpec((seq, PAIR), lambda b, p, i: (b, v_blk + p)),
                  pl.BlockSpec((None, None, nk, 2, tk), lambda b, p, i: (b, p, 0, 0, 0))],
        out_specs=pl.BlockSpec((tq, PAIR), lambda b, p, i: (b * nq + i, p)),
        scratch_shapes=[pltpu.VMEM((2, tq, LANES), F32), pltpu.VMEM((2, tq, LANES), F32),
                        pltpu.VMEM((2, tq, PAIR), F32)],
        compiler_params=_cparams("parallel", "parallel", "arbitrary"),
        name="fox_attention",
    )(pj, pj, pj, c_tiles)


def _gelu_tanh(x):
    return 0.5 * x * (1.0 + jnp.tanh(0.7978845608028654 * (x + 0.044715 * x * x * x)))


def _compress_kernel(t_ref, pe_ref, w1_ref, w2_ref, o_ref):
    half = w1_ref.shape[0] // 2
    t = t_ref[...]
    a = _dot(t, w1_ref[0:half, :])
    b = _dot(t, w1_ref[half:, :])
    pe_term = _dot(pe_ref[...], w1_ref[...])[0:1, :]
    nch = t.shape[0]
    h = a + pltpu.roll(b, nch - 1, axis=0) + pe_term
    o_ref[...] = _dot(_gelu_tanh(h).astype(BF16), w2_ref[...]).astype(o_ref.dtype)


def _compress(tchunks, pe8, w1, w2dup):
    b, two, g, nch, width = tchunks.shape
    return pl.pallas_call(
        _compress_kernel,
        out_shape=jax.ShapeDtypeStruct((b, two, g, nch, PAIR), BF16),
        grid=(b, two, g),
        in_specs=[pl.BlockSpec((None, None, None, nch, width), lambda bb, kv, gg: (bb, kv, gg, 0, 0)),
                  pl.BlockSpec((None, 8, 2 * width), lambda bb, kv, gg: (kv, 0, 0)),
                  pl.BlockSpec((None, 2 * width, HEAD_DIM), lambda bb, kv, gg: (kv, 0, 0)),
                  pl.BlockSpec((None, HEAD_DIM, PAIR), lambda bb, kv, gg: (kv, 0, 0))],
        out_specs=pl.BlockSpec((None, None, None, nch, PAIR), lambda bb, kv, gg: (bb, kv, gg, 0, 0)),
        compiler_params=_cparams("parallel", "parallel", "parallel"),
        name="nsa_compress",
    )(tchunks, pe8, w1, w2dup)


def _group_slope(g, r):
    return jnp.where(g == 0, ALIBI_SLOPES_LOG2[r], ALIBI_SLOPES_LOG2[NSA_GROUP_SIZE + r]).astype(F32)


def _split3(x):
    hi = x.astype(BF16)
    r1 = x - hi.astype(F32)
    mid = r1.astype(BF16)
    lo = (r1 - mid.astype(F32)).astype(BF16)
    return hi, mid, lo


def _cmp_select_kernel(q_ref, kc_ref, vc_ref, pool_ref, tmap_ref, oc_ref, sel_ref, flag_ref, *, tq, n_sel):
    g = pl.program_id(1)
    t0 = pl.program_id(2) * tq
    nch = kc_ref.shape[0]
    ns = pool_ref.shape[1]
    low = _lane_is_low()
    t = t0 + lax.broadcasted_iota(jnp.int32, (tq, 1), 0)
    cmp_end = lax.broadcasted_iota(jnp.int32, (1, nch), 1) * CMP_STRIDE + (CMP_BLOCK - 1)
    d = t - cmp_end
    valid = d >= 0
    df = d.astype(F32)
    kc = kc_ref[...]
    vc = vc_ref[...]
    imp = jnp.zeros((tq, nch), F32)
    for pair in range(NSA_GROUP_SIZE // 2):
        qh = _split_pair(q_ref[:, pair * PAIR:(pair + 1) * PAIR])
        outs = []
        for h in range(2):
            s = _dot_nt(qh[h], kc) - _group_slope(g, 2 * pair + h) * df
            s = jnp.where(valid, s, NEG_INF)
            m = jnp.max(s, axis=-1, keepdims=True)
            e = jnp.where(valid, jnp.exp2(s - m), 0.0)
            p = e / jnp.maximum(jnp.sum(e, axis=-1, keepdims=True), 1e-30)
            outs.append(_dot(p.astype(BF16), vc))
            imp = imp + p
        oc_ref[:, pair * PAIR:(pair + 1) * PAIR] = jnp.where(low, outs[0], outs[1])

    pool = pool_ref[...]
    hi, mid, lo = _split3(imp)
    p_slc = _dot(hi, pool) + _dot(mid, pool) + _dot(lo, pool)

    blk = lax.broadcasted_iota(jnp.int32, (1, ns), 1)
    cur = t // SLC_BLOCK
    forced = (blk == 0) | (blk == cur) | (blk == cur - 1)
    val = jnp.where(forced, REMOVED, jnp.where(blk > cur, NEG_INF, p_slc))
    for _ in range(n_sel - 3):
        mx = jnp.max(val, axis=-1, keepdims=True)
        pick = jnp.min(jnp.where(val == mx, blk, ns), axis=-1, keepdims=True)
        val = jnp.where(blk == pick, REMOVED, val)
    sel = jnp.where(val == REMOVED, 1.0, 0.0)
    sel_ref[...] = sel
    any_sel = jnp.broadcast_to(jnp.max(sel, axis=0, keepdims=True), (8, ns)).astype(BF16)
    flag_ref[...] = (_dot(any_sel, tmap_ref[...]) > 0.5).astype(jnp.int32)


def _cmp_select(pj, cmp_kv, pool, tmap, *, batch, seq, tq, q_blk, n_sel):
    nq = seq // tq
    nch = cmp_kv.shape[3]
    ns = pool.shape[1]
    gw = NSA_GROUP_SIZE * HEAD_DIM
    return pl.pallas_call(
        functools.partial(_cmp_select_kernel, tq=tq, n_sel=n_sel),
        out_shape=(jax.ShapeDtypeStruct((batch * seq, NSA_HEADS * HEAD_DIM), F32),
                   jax.ShapeDtypeStruct((batch, NSA_GROUPS, seq, ns), F32),
                   jax.ShapeDtypeStruct((batch, NSA_GROUPS, nq, 8, LANES), jnp.int32)),
        grid=(batch, NSA_GROUPS, nq),
        in_specs=[pl.BlockSpec((tq, gw), lambda b, g, i: (b * nq + i, q_blk + g)),
                  pl.BlockSpec((None, None, None, nch, PAIR), lambda b, g, i: (b, 0, g, 0, 0)),
                  pl.BlockSpec((None, None, None, nch, PAIR), lambda b, g, i: (b, 1, g, 0, 0)),
                  pl.BlockSpec((nch, ns), lambda b, g, i: (0, 0)),
                  pl.BlockSpec((ns, LANES), lambda b, g, i: (0, 0))],
        out_specs=(pl.BlockSpec((tq, gw), lambda b, g, i: (b * nq + i, g)),
                   pl.BlockSpec((None, None, tq, ns), lambda b, g, i: (b, g, i, 0)),
                   pl.BlockSpec((None, None, None, 8, LANES), lambda b, g, i: (b, g, i, 0, 0))),
        compiler_params=_cparams("parallel", "parallel", "parallel"),
        name="nsa_cmp_select",
    )(pj, cmp_kv, cmp_kv, pool, tmap)


def _nsa_kernel(flags_ref, q_ref, ks_ref, vs_ref, kwp_ref, vwp_ref, kwd_ref, vwd_ref, sel_ref, oc_ref, gate_ref,
                o_ref, m_ref, acc_ref, *, tile, nq):
    b = pl.program_id(0)
    g = pl.program_id(1)
    i = pl.program_id(2)
    nh = NSA_GROUP_SIZE
    ns = sel_ref.shape[1]
    bpt = tile // SLC_BLOCK
    low = _lane_is_low()
    row = lax.broadcasted_iota(jnp.int32, (tile, tile), 0)
    col = lax.broadcasted_iota(jnp.int32, (tile, tile), 1)
    relpos = lax.broadcasted_iota(jnp.int32, (1, tile), 1).astype(F32)
    sel = sel_ref[...].astype(BF16)
    blk_of_key = lax.broadcasted_iota(jnp.int32, (ns, tile), 1) // SLC_BLOCK
    blk_row = lax.broadcasted_iota(jnp.int32, (ns, tile), 0)
    qh = []
    for pair in range(nh // 2):
        qh.extend(_split_pair(q_ref[:, pair * PAIR:(pair + 1) * PAIR]))
    slopes = [_group_slope(g, r) for r in range(nh)]
    _flash_reset(m_ref, acc_ref)

    def alibi(r, j):
        return slopes[r] * (relpos + ((j - i) * tile).astype(F32))

    def selected(j):
        expand = jnp.where(blk_row == blk_of_key + j * bpt, 1.0, 0.0).astype(BF16)
        return _dot(sel, expand) > 0.5

    def attend(slot, k_tile, v_tile, j, bias):
        v_aug = _with_ones(v_tile)
        for r in range(nh):
            s = _dot_nt(qh[r], k_tile) + alibi(r, j) + bias
            _flash_step(s, v_aug, m_ref.at[slot + r], acc_ref.at[slot + r])

    def sel_step(j, keep):
        ks = pl.multiple_of(j * tile, tile)
        attend(0, ks_ref[pl.ds(ks, tile), :], vs_ref[pl.ds(ks, tile), :], j, jnp.where(keep, 0.0, NEG_INF))

    sel_step(i, selected(i) & (col <= row))

    def sel_body(j, carry):
        @pl.when(flags_ref[((b * NSA_GROUPS + g) * nq + i) * nq + j] != 0)
        def _():
            sel_step(j, selected(j))
        return carry

    lax.fori_loop(0, i, sel_body, 0)

    attend(nh, kwd_ref[...], vwd_ref[...], i, jnp.where(col <= row, 0.0, NEG_INF))

    @pl.when(i > 0)
    def _():
        attend(nh, kwp_ref[...], vwp_ref[...], i - 1, jnp.where(col > row, 0.0, NEG_INF))

    gates = jax.nn.sigmoid(gate_ref[...])

    def gate_col(r, branch):
        c0 = FOX_HEADS + 3 * r + branch
        c1 = c0 + 3 * nh
        return jnp.where(g == 0, gates[:, c0:c0 + 1], gates[:, c1:c1 + 1])

    for pair in range(nh // 2):
        o_c = oc_ref[:, pair * PAIR:(pair + 1) * PAIR]
        outs = []
        for h in range(2):
            r = 2 * pair + h
            o_s = _flash_out(acc_ref[r])
            o_w = _flash_out(acc_ref[nh + r])
            outs.append(gate_col(r, 0) * o_c + gate_col(r, 1) * o_s + gate_col(r, 2) * o_w)
        o_ref[:, pair * PAIR:(pair + 1) * PAIR] = jnp.where(low, outs[0], outs[1]).astype(o_ref.dtype)


def _nsa_attention(flags, pj, sel, o_c, small, *, batch, seq, tile, q_blk, ks_blk, vs_blk, kw_blk, vw_blk):
    nq = seq // tile
    ns = sel.shape[3]
    gw = NSA_GROUP_SIZE * HEAD_DIM
    slots = 2 * NSA_GROUP_SIZE
    cur = lambda b, g, i, f, base: (b * nq + i, base + g)
    prev = lambda b, g, i, f, base: (b * nq + jnp.maximum(i - 1, 0), base + g)
    grid_spec = pltpu.PrefetchScalarGridSpec(
        num_scalar_prefetch=1,
        grid=(batch, NSA_GROUPS, nq),
        in_specs=[pl.BlockSpec((tile, gw), functools.partial(cur, base=q_blk)),
                  pl.BlockSpec((seq, PAIR), lambda b, g, i, f: (b, ks_blk + g)),
                  pl.BlockSpec((seq, PAIR), lambda b, g, i, f: (b, vs_blk + g)),
                  pl.BlockSpec((tile, PAIR), functools.partial(prev, base=kw_blk)),
                  pl.BlockSpec((tile, PAIR), functools.partial(prev, base=vw_blk)),
                  pl.BlockSpec((tile, PAIR), functools.partial(cur, base=kw_blk)),
                  pl.BlockSpec((tile, PAIR), functools.partial(cur, base=vw_blk)),
                  pl.BlockSpec((None, None, tile, ns), lambda b, g, i, f: (b, g, i, 0)),
                  pl.BlockSpec((tile, gw), functools.partial(cur, base=0)),
                  pl.BlockSpec((tile, small.shape[1]), lambda b, g, i, f: (b * nq + i, 0))],
        out_specs=pl.BlockSpec((tile, gw), functools.partial(cur, base=0)),
        scratch_shapes=[pltpu.VMEM((slots, tile, LANES), F32), pltpu.VMEM((slots, tile, 2 * PAIR), F32)])
    return pl.pallas_call(
        functools.partial(_nsa_kernel, tile=tile, nq=nq),
        out_shape=jax.ShapeDtypeStruct((batch * seq, NSA_HEADS * HEAD_DIM), BF16),
        grid_spec=grid_spec,
        compiler_params=_cparams("parallel", "parallel", "arbitrary"),
        name="nsa_attention",
    )(flags, pj, pj, pj, pj, pj, pj, pj, sel, o_c, small)


def _outproj_kernel(h_ref, a_ref, b_ref, wa_ref, wb_ref, g_ref, o_ref):
    m = _dot(a_ref[...], wa_ref[...]) + _dot(b_ref[...], wb_ref[...])
    o_ref[...] = h_ref[...] + _rms(m, g_ref[...])


def _outproj(h2, a, b, wa, wb, g, *, tm):
    n, d = h2.shape
    const = lambda i: (0, 0)
    return pl.pallas_call(
        _outproj_kernel,
        out_shape=jax.ShapeDtypeStruct((n, d), F32),
        grid=(n // tm,),
        in_specs=[pl.BlockSpec((tm, d), lambda i: (i, 0)),
                  pl.BlockSpec((tm, a.shape[1]), lambda i: (i, 0)),
                  pl.BlockSpec((tm, b.shape[1]), lambda i: (i, 0)),
                  pl.BlockSpec(wa.shape, const),
                  pl.BlockSpec(wb.shape, const),
                  pl.BlockSpec((1, d), const)],
        out_specs=pl.BlockSpec((tm, d), lambda i: (i, 0)),
        compiler_params=_cparams("parallel"),
        name="attn_outproj",
    )(h2, a, b, wa, wb, g)


def _ffn_kernel(h_ref, gin_ref, gout_ref, wg_ref, wu_ref, wd_ref, o_ref, acc_ref, *, chunk):
    h = h_ref[...]
    ub = _rms(h, gin_ref[...]).astype(BF16)
    hidden = wg_ref.shape[1]
    for idx, c in enumerate(range(0, hidden, chunk)):
        gate = _dot(ub, wg_ref[:, c:c + chunk])
        up = _dot(ub, wu_ref[:, c:c + chunk])
        act = (gate * jax.nn.sigmoid(gate) * up).astype(BF16)
        part = _dot(act, wd_ref[c:c + chunk, :])
        if idx == 0:
            acc_ref[...] = part
        else:
            acc_ref[...] += part
    o_ref[...] = h + _rms(acc_ref[...], gout_ref[...])


def _ffn(h2, g_in, g_out, wg, wu, wd, *, tm, chunk):
    n, d = h2.shape
    const = lambda i: (0, 0)
    return pl.pallas_call(
        functools.partial(_ffn_kernel, chunk=chunk),
        out_shape=jax.ShapeDtypeStruct((n, d), F32),
        grid=(n // tm,),
        in_specs=[pl.BlockSpec((tm, d), lambda i: (i, 0)),
                  pl.BlockSpec((1, d), const),
                  pl.BlockSpec((1, d), const),
                  pl.BlockSpec(wg.shape, const),
                  pl.BlockSpec(wu.shape, const),
                  pl.BlockSpec(wd.shape, const)],
        out_specs=pl.BlockSpec((tm, d), lambda i: (i, 0)),
        scratch_shapes=[pltpu.VMEM((tm, d), F32)],
        compiler_params=_cparams("parallel"),
        name="swiglu_ffn",
    )(h2, g_in, g_out, wg, wu, wd)


def _pool_kernel(h_ref, halo_ref, gin_ref, gout_ref, w_ref, sc_ref, o_ref, ext_ref, *, tm, halo, tiles_per_seq):
    i = pl.program_id(0)
    first = (i % tiles_per_seq) == 0
    h = h_ref[...]
    u = _rms(h, gin_ref[...])
    uh = _rms(halo_ref[...], gin_ref[...])
    ext_ref[0:halo, :] = jnp.where(first, 0.0, uh)
    ext_ref[halo:, :] = u
    t = (i % tiles_per_seq) * tm + lax.broadcasted_iota(jnp.int32, (tm, 1), 0)
    group = h.shape[1] // len(POOL_WINDOWS)
    ys = []
    for gi, w in enumerate(POOL_WINDOWS):
        cols = slice(gi * group, (gi + 1) * group)
        total = u[:, cols]
        for j in range(1, w):
            total = total + ext_ref[halo - j:halo - j + tm, cols]
        count = jnp.minimum(t + 1, w).astype(F32)
        pooled = total / count - u[:, cols]
        ys.append(_dot(pooled.astype(BF16), w_ref[gi]))
    y = jnp.concatenate(ys, axis=-1) * sc_ref[...]
    o_ref[...] = h + _rms(y, gout_ref[...])


def _pool_mixer(h2, g_in, g_out, w_groups, scale, *, seq, tm):
    n, d = h2.shape
    halo = max(POOL_WINDOWS)
    tiles_per_seq = seq // tm
    ratio = tm // halo
    const = lambda i: (0, 0)
    return pl.pallas_call(
        functools.partial(_pool_kernel, tm=tm, halo=halo, tiles_per_seq=tiles_per_seq),
        out_shape=jax.ShapeDtypeStruct((n, d), F32),
        grid=(n // tm,),
        in_specs=[pl.BlockSpec((tm, d), lambda i: (i, 0)),
                  pl.BlockSpec((halo, d), lambda i: (jnp.maximum(i * ratio - 1, 0), 0)),
                  pl.BlockSpec((1, d), const),
                  pl.BlockSpec((1, d), const),
                  pl.BlockSpec(w_groups.shape, lambda i: (0, 0, 0)),
                  pl.BlockSpec((1, d), const)],
        out_specs=pl.BlockSpec((tm, d), lambda i: (i, 0)),
        scratch_shapes=[pltpu.VMEM((tm + halo, d), F32)],
        compiler_params=_cparams("parallel"),
        name="pool_mixer",
    )(h2, h2, g_in, g_out, w_groups, scale)


def _pack_in_weights(w_in):
    d = w_in.shape[0]
    hw = FOX_HEADS * HEAD_DIM
    kv = NSA_GROUPS * HEAD_DIM
    sizes = (hw, hw, hw, FOX_HEADS, NSA_HEADS * HEAD_DIM, kv, kv, kv, kv, kv, kv, 3 * NSA_HEADS)
    offs = np.concatenate([[0], np.cumsum(sizes)])
    fq, fk, fv, ff, nq, kc, vc, ks, vs, kw, vw, ng = [w_in[:, offs[j]:offs[j + 1]] for j in range(len(sizes))]
    scale = HEAD_DIM ** -0.5 * LOG2E

    def dup(w):
        w = w.reshape(d, NSA_GROUPS, 1, HEAD_DIM)
        return jnp.broadcast_to(w, (d, NSA_GROUPS, 2, HEAD_DIM)).reshape(d, NSA_GROUPS * PAIR)

    w_main = jnp.concatenate([fq * scale, fk, fv, nq * scale, dup(ks), dup(vs), dup(kw), dup(vw)], axis=1)
    w_cmp = jnp.concatenate([kc, vc], axis=1)
    pad = jnp.zeros((d, LANES - FOX_HEADS - 3 * NSA_HEADS), F32)
    w_small = jnp.concatenate([ff, ng, pad], axis=1)
    return w_main.astype(BF16), w_cmp.astype(BF16), w_small


def _mixer_fox_nsa(h2, g_in, g_out, w_in, b_f, cmp_pe, cmp_w1, cmp_w2, w_out, *, batch, seq):
    tile = WINDOW
    assert seq % tile == 0 and seq // tile <= LANES
    w_main, w_cmp, w_small = _pack_in_weights(w_in)
    pj, pc, small = _inproj(h2, g_in, w_main, w_cmp, w_small, tm=tile)
    npair = FOX_HEADS // 2
    q_blk, k_blk, v_blk, nq_blk = 0, npair, 2 * npair, 3 * npair
    ks_blk = nq_blk + NSA_HEADS // 2
    vs_blk, kw_blk, vw_blk = ks_blk + NSA_GROUPS, ks_blk + 2 * NSA_GROUPS, ks_blk + 3 * NSA_GROUPS

    c = _forget_cumsum(small, b_f, batch=batch, seq=seq, tc=tile)
    nq = seq // tile
    fox_tq, fox_tk = min(FOX_TQ, seq), min(FOX_TK, seq)
    c_tiles = c.reshape(batch, npair, 2, seq // fox_tk, fox_tk).transpose(0, 1, 3, 2, 4)
    o_fox = _fox_attention(pj, c_tiles, batch=batch, seq=seq, tq=fox_tq, tk=fox_tk,
                           q_blk=q_blk, k_blk=k_blk, v_blk=v_blk)

    nch = seq // CMP_STRIDE
    ns = seq // SLC_BLOCK
    n_sel = min(N_SELECT, ns)
    kv = NSA_GROUPS * HEAD_DIM
    tch = pc.reshape(batch, nch, CMP_STRIDE, 2, NSA_GROUPS, HEAD_DIM)
    tch = tch.transpose(0, 3, 4, 1, 2, 5).reshape(batch, 2, NSA_GROUPS, nch, CMP_STRIDE * HEAD_DIM)
    pe8 = jnp.broadcast_to(cmp_pe.reshape(2, 1, CMP_BLOCK * HEAD_DIM), (2, 8, CMP_BLOCK * HEAD_DIM)).astype(BF16)
    w2dup = jnp.concatenate([cmp_w2, cmp_w2], axis=-1).astype(BF16)
    cmp_kv = _compress(tch, pe8, cmp_w1.astype(BF16), w2dup)

    ratio = SLC_BLOCK // CMP_STRIDE
    r = CMP_BLOCK // CMP_STRIDE
    n_idx = np.arange(nch)[:, None]
    b_idx = np.arange(ns)[None, :]
    pool_np = ((n_idx >= ratio * b_idx - (r - 1)) & (n_idx <= ratio * b_idx + ratio - 1)
               & (n_idx < nch - r + 1)).astype(np.float32)
    pool = jnp.asarray(pool_np, dtype=BF16)
    tmap_np = (np.arange(ns)[:, None] // (tile // SLC_BLOCK) == np.arange(LANES)[None, :]).astype(np.float32)
    o_c, sel, flag_blocks = _cmp_select(pj, cmp_kv, pool, jnp.asarray(tmap_np, dtype=BF16), batch=batch, seq=seq,
                                        tq=tile, q_blk=nq_blk // 2, n_sel=n_sel)
    flags = flag_blocks[:, :, :, 0, :nq].reshape(-1)
    o_nsa = _nsa_attention(flags, pj, sel, o_c, small, batch=batch, seq=seq, tile=tile,
                           q_blk=nq_blk // 2, ks_blk=ks_blk, vs_blk=vs_blk, kw_blk=kw_blk, vw_blk=vw_blk)

    hw = FOX_HEADS * HEAD_DIM
    return _outproj(h2, o_fox, o_nsa, w_out[:hw].astype(BF16), w_out[hw:].astype(BF16), g_out, tm=tile)


def kernel(x, norm_g, attn_w_in, fox_b_f, nsa_cmp_pe, nsa_cmp_w1, nsa_cmp_w2, attn_w_out,
           pool_w, pool_scale, ffn_w_gate, ffn_w_up, ffn_w_down):
    batch, seq, d = x.shape
    depth = norm_g.shape[0]
    tm = 512 if seq % 512 == 0 else seq
    h = x.reshape(batch * seq, d)
    for layer in range(depth):
        g = norm_g[layer].reshape(4, 1, d)
        i = layer // 2
        if layer % 2 == 0:
            h = _mixer_fox_nsa(h, g[0], g[1], attn_w_in[i], fox_b_f[i], nsa_cmp_pe[i], nsa_cmp_w1[i],
                               nsa_cmp_w2[i], attn_w_out[i], batch=batch, seq=seq)
        else:
            h = _pool_mixer(h, g[0], g[1], pool_w[i].astype(BF16), pool_scale[i].reshape(1, d), seq=seq, tm=tm)
        h = _ffn(h, g[2], g[3], ffn_w_gate[layer].astype(BF16), ffn_w_up[layer].astype(BF16),
                 ffn_w_down[layer].astype(BF16), tm=tm, chunk=256)
    return h.reshape(batch, seq, d)
```

```python
import functools

import numpy as np
import jax
import jax.numpy as jnp
from jax import lax
from jax.experimental import pallas as pl
from jax.experimental.pallas import tpu as pltpu

F32 = jnp.float32
BF16 = jnp.bfloat16

HEAD_DIM = 64
FOX_HEADS = 8
NSA_HEADS = 8
NSA_GROUPS = 2
NSA_GROUP_SIZE = NSA_HEADS // NSA_GROUPS
CMP_BLOCK = 32
CMP_STRIDE = 16
SLC_BLOCK = 64
N_SELECT = 16
WINDOW = 512
POOL_WINDOWS = (2, 4, 8, 16)
RMS_EPS = 1e-6
NEG_INF = -1e30
BIG = 1e30
REMOVED = -3e38
LANES = 128
PAIR = 2 * HEAD_DIM
VMEM_LIMIT = 56 * 1024 * 1024
CMP_SEGMENTS = 4
UNDERFLOW_BITS = 160.0
NORM_SLACK = 1.02
FOX_TQ, FOX_TK = 1024, 1024

LOG2E = 1.4426950408889634
ALIBI_SLOPES_LOG2 = tuple(float(2.0 ** (-8.0 * (i + 1.0) / NSA_HEADS)) * LOG2E for i in range(NSA_HEADS))


def _cparams(*sem):
    return pltpu.CompilerParams(dimension_semantics=sem, vmem_limit_bytes=VMEM_LIMIT)


def _rms(x, g):
    return x * lax.rsqrt(jnp.mean(x * x, axis=-1, keepdims=True) + RMS_EPS) * g


def _dot(a, b):
    return jnp.dot(a, b, preferred_element_type=F32)


def _dot_nt(a, b):
    return lax.dot_general(a, b, (((1,), (1,)), ((), ())), preferred_element_type=F32)


def _inproj_kernel(x_ref, g_ref, w_ref, wc_ref, ws_ref, hmap_ref, o_ref, oc_ref, os_ref, nrm_ref, *, fox_width):
    u = _rms(x_ref[...], g_ref[...])
    ub = u.astype(BF16)
    norms = []
    for c in range(0, o_ref.shape[1], fox_width):
        r = _dot(ub, w_ref[:, c:c + fox_width]).astype(BF16)
        o_ref[:, c:c + fox_width] = r
        if c < 2 * fox_width:
            rf = r.astype(F32)
            sq = _dot((rf * rf).astype(BF16), hmap_ref[...])
            norms.append(jnp.max(sq, axis=0, keepdims=True))
    oc_ref[...] = _dot(ub, wc_ref[...]).astype(BF16)
    os_ref[...] = jnp.dot(u, ws_ref[...], preferred_element_type=F32,
                          precision=lax.Precision.HIGHEST)
    row = lax.broadcasted_iota(jnp.int32, nrm_ref.shape, 0)
    nrm_ref[...] = jnp.where(row == 0, norms[0], jnp.where(row == 1, norms[1], 0.0))


def _inproj(x2, g, w_main, w_cmp, w_small, *, tm):
    n, d = x2.shape
    cm, cc, cs = w_main.shape[1], w_cmp.shape[1], w_small.shape[1]
    fox_width = FOX_HEADS * HEAD_DIM
    hmap = jnp.asarray((np.arange(fox_width)[:, None] // HEAD_DIM == np.arange(LANES)[None, :]).astype(np.float32),
                       dtype=BF16)
    const = lambda i: (0, 0)
    return pl.pallas_call(
        functools.partial(_inproj_kernel, fox_width=fox_width),
        out_shape=(jax.ShapeDtypeStruct((n, cm), BF16),
                   jax.ShapeDtypeStruct((n, cc), BF16),
                   jax.ShapeDtypeStruct((n, cs), F32),
                   jax.ShapeDtypeStruct((n // tm, 8, LANES), F32)),
        grid=(n // tm,),
        in_specs=[pl.BlockSpec((tm, d), lambda i: (i, 0)),
                  pl.BlockSpec((1, d), const),
                  pl.BlockSpec((d, cm), const),
                  pl.BlockSpec((d, cc), const),
                  pl.BlockSpec((d, cs), const),
                  pl.BlockSpec((fox_width, LANES), const)],
        out_specs=(pl.BlockSpec((tm, cm), lambda i: (i, 0)),
                   pl.BlockSpec((tm, cc), lambda i: (i, 0)),
                   pl.BlockSpec((tm, cs), lambda i: (i, 0)),
                   pl.BlockSpec((None, 8, LANES), lambda i: (i, 0, 0))),
        compiler_params=_cparams("parallel"),
        name="inproj",
    )(x2, g, w_main, w_cmp, w_small, hmap)


def _forget_cumsum_kernel(sm_ref, bf_ref, tri_ref, o_ref, carry_ref):
    @pl.when(pl.program_id(1) == 0)
    def _():
        carry_ref[...] = jnp.zeros_like(carry_ref)

    zt = sm_ref[...].T
    z = zt[0:FOX_HEADS, :] + bf_ref[...]
    log_f = jnp.minimum(z, 0.0) - jnp.log1p(jnp.exp(-jnp.abs(z)))
    c = jnp.dot(log_f, tri_ref[...], preferred_element_type=F32,
                precision=lax.Precision.HIGHEST) + carry_ref[...]
    o_ref[...] = c * LOG2E
    carry_ref[...] = c[:, c.shape[1] - 1:]


def _forget_cumsum(small, b_f, *, batch, seq, tc):
    tri = jnp.asarray(np.triu(np.ones((tc, tc), np.float32)))
    nchunk = seq // tc
    return pl.pallas_call(
        _forget_cumsum_kernel,
        out_shape=jax.ShapeDtypeStruct((batch, FOX_HEADS, seq), F32),
        grid=(batch, nchunk),
        in_specs=[pl.BlockSpec((tc, small.shape[1]), lambda b, j: (b * nchunk + j, 0)),
                  pl.BlockSpec((FOX_HEADS, 1), lambda b, j: (0, 0)),
                  pl.BlockSpec((tc, tc), lambda b, j: (0, 0))],
        out_specs=pl.BlockSpec((None, FOX_HEADS, tc), lambda b, j: (b, 0, j)),
        scratch_shapes=[pltpu.VMEM((FOX_HEADS, 1), F32)],
        compiler_params=_cparams("parallel", "arbitrary"),
        name="forget_cumsum",
    )(small, b_f.reshape(FOX_HEADS, 1), tri)


def _lane_is_low():
    return lax.broadcasted_iota(jnp.int32, (1, PAIR), 1) < HEAD_DIM


def _with_ones(v_tile):
    return jnp.concatenate([v_tile, jnp.ones_like(v_tile)], axis=1)


def _flash_step(s, v, m_ref, acc_ref, l_ref=None):
    m_old = m_ref[...]
    m_new = jnp.maximum(m_old, jnp.max(s, axis=-1, keepdims=True))
    alpha = jnp.exp2(m_old - m_new)
    p = jnp.exp2(s - jnp.concatenate([m_new] * (s.shape[1] // LANES), axis=1))
    if l_ref is None:
        alpha_acc = jnp.concatenate([alpha, alpha], axis=1)
    else:
        alpha_acc = alpha
        l_ref[...] = alpha * l_ref[...] + jnp.sum(p, axis=-1, keepdims=True)
    acc_ref[...] = alpha_acc * acc_ref[...] + _dot(p.astype(BF16), v)
    m_ref[...] = m_new


def _flash_reset(m_ref, *sum_refs):
    m_ref[...] = jnp.full_like(m_ref, NEG_INF)
    for ref in sum_refs:
        ref[...] = jnp.zeros_like(ref)


def _flash_out(acc):
    return acc[:, :PAIR] / acc[:, PAIR:]


def _split_pair(q):
    low = _lane_is_low()
    zero = jnp.zeros_like(q)
    return jnp.where(low, q, zero), jnp.where(low, zero, q)


def _fox_kernel(live_ref, q_ref, k_ref, v_ref, c_ref, o_ref, m_ref, l_ref, acc_ref, *, tq, tk):
    b = pl.program_id(0)
    pair = pl.program_id(1)
    i = pl.program_id(2)
    nq = pl.num_programs(2)
    nk = c_ref.shape[0]
    qh = _split_pair(q_ref[...])
    _flash_reset(m_ref, l_ref, acc_ref)

    def head_step(h, j, mask):
        ks = pl.multiple_of(j * tk, tk)
        s = _dot_nt(qh[h], k_ref[pl.ds(ks, tk), :]) - c_ref[j, h:h + 1, :]
        if mask is not None:
            s = jnp.where(mask, s, NEG_INF)
        _flash_step(s, v_ref[pl.ds(ks, tk), :], m_ref.at[h], acc_ref.at[h], l_ref.at[h])

    def body(j, carry):
        for h in range(2):
            @pl.when(live_ref[(((b * FOX_HEADS) + 2 * pair + h) * nq + i) * nk + j] != 0)
            def _():
                head_step(h, j, None)
        return carry

    n_full = (i * tq) // tk
    lax.fori_loop(0, n_full, body, 0)
    row = lax.broadcasted_iota(jnp.int32, (tq, tk), 0)
    col = lax.broadcasted_iota(jnp.int32, (tq, tk), 1)
    for d in range(max(1, tq // tk)):
        j = n_full + d
        for h in range(2):
            head_step(h, j, col + (j * tk - i * tq) <= row)
    o_ref[...] = jnp.where(_lane_is_low(), acc_ref[0] / l_ref[0], acc_ref[1] / l_ref[1]).astype(o_ref.dtype)


def _fox_live_tiles(c2, nrm, *, batch, seq, tq, tk, tm):
    nq, nk = seq // tq, seq // tk
    qn = nrm[:, 0, :FOX_HEADS].reshape(batch, nq, tq // tm, FOX_HEADS).max(axis=2)
    kn = nrm[:, 1, :FOX_HEADS].reshape(batch, seq // tm, FOX_HEADS).max(axis=1)
    bound = 2.0 * NORM_SLACK * jnp.sqrt(qn * kn[:, None, :])
    c_start = c2[:, :, ::tq]
    c_end = c2[:, :, tk - 1::tk]
    decay = c_end[:, :, None, :] - c_start[:, :, :, None]
    live = decay < UNDERFLOW_BITS + jnp.transpose(bound, (0, 2, 1))[..., None]
    return live.astype(jnp.int32).reshape(-1)


def _fox_attention(live, pj, c_tiles, *, batch, seq, tq, tk, q_blk, k_blk, v_blk):
    nq = seq // tq
    nk = seq // tk
    npair = FOX_HEADS // 2
    grid_spec = pltpu.PrefetchScalarGridSpec(
        num_scalar_prefetch=1,
        grid=(batch, npair, nq),
        in_specs=[pl.BlockSpec((tq, PAIR), lambda b, p, i, f: (b * nq + i, q_blk + p)),
                  pl.BlockSpec((seq, PAIR), lambda b, p, i, f: (b, k_blk + p)),
                  pl.BlockSpec((seq, PAIR), lambda b, p, i, f: (b, v_blk + p)),
                  pl.BlockSpec((None, None, nk, 2, tk), lambda b, p, i, f: (b, p, 0, 0, 0))],
        out_specs=pl.BlockSpec((tq, PAIR), lambda b, p, i, f: (b * nq + i, p)),
        scratch_shapes=[pltpu.VMEM((2, tq, LANES), F32), pltpu.VMEM((2, tq, LANES), F32),
                        pltpu.VMEM((2, tq, PAIR), F32)])
    return pl.pallas_call(
        functools.partial(_fox_kernel, tq=tq, tk=tk),
        out_shape=jax.ShapeDtypeStruct((batch * seq, FOX_HEADS * HEAD_DIM), BF16),
        grid_spec=grid_spec,
        compiler_params=_cparams("parallel", "parallel", "arbitrary"),
        name="fox_attention",
    )(live, pj, pj, pj, c_tiles)


def _gelu_tanh(x):
    return 0.5 * x * (1.0 + jnp.tanh(0.7978845608028654 * (x + 0.044715 * x * x * x)))


def _compress_kernel(t_ref, pe_ref, w1_ref, w2_ref, o_ref):
    half = w1_ref.shape[0] // 2
    t = t_ref[...]
    a = _dot(t, w1_ref[0:half, :])
    b = _dot(t, w1_ref[half:, :])
    pe_term = _dot(pe_ref[...], w1_ref[...])[0:1, :]
    nch = t.shape[0]
    h = a + pltpu.roll(b, nch - 1, axis=0) + pe_term
    o_ref[...] = _dot(_gelu_tanh(h).astype(BF16), w2_ref[...]).astype(o_ref.dtype)


def _compress(tchunks, pe8, w1, w2dup):
    b, two, g, nch, width = tchunks.shape
    return pl.pallas_call(
        _compress_kernel,
        out_shape=jax.ShapeDtypeStruct((b, two, g, nch, PAIR), BF16),
        grid=(b, two, g),
        in_specs=[pl.BlockSpec((None, None, None, nch, width), lambda bb, kv, gg: (bb, kv, gg, 0, 0)),
                  pl.BlockSpec((None, 8, 2 * width), lambda bb, kv, gg: (kv, 0, 0)),
                  pl.BlockSpec((None, 2 * width, HEAD_DIM), lambda bb, kv, gg: (kv, 0, 0)),
                  pl.BlockSpec((None, HEAD_DIM, PAIR), lambda bb, kv, gg: (kv, 0, 0))],
        out_specs=pl.BlockSpec((None, None, None, nch, PAIR), lambda bb, kv, gg: (bb, kv, gg, 0, 0)),
        compiler_params=_cparams("parallel", "parallel", "parallel"),
        name="nsa_compress",
    )(tchunks, pe8, w1, w2dup)


def _group_slope(g, r):
    return jnp.where(g == 0, ALIBI_SLOPES_LOG2[r], ALIBI_SLOPES_LOG2[NSA_GROUP_SIZE + r]).astype(F32)


def _split3(x):
    hi = x.astype(BF16)
    r1 = x - hi.astype(F32)
    mid = r1.astype(BF16)
    lo = (r1 - mid.astype(F32)).astype(BF16)
    return hi, mid, lo


def _cmp_select_kernel(q_ref, kc_ref, vc_ref, pool_ref, tmap_ref, *rest, tq, n_sel, first_tile):
    oc_ref, sel_ref, flag_ref = rest[-3:]
    g = pl.program_id(1)
    t0 = (first_tile + pl.program_id(2)) * tq
    nch = kc_ref.shape[0]
    ns = pool_ref.shape[1]
    low = _lane_is_low()
    t = t0 + lax.broadcasted_iota(jnp.int32, (tq, 1), 0)
    cmp_end = lax.broadcasted_iota(jnp.int32, (1, nch), 1) * CMP_STRIDE + (CMP_BLOCK - 1)
    d = t - cmp_end
    valid = d >= 0
    df = d.astype(F32)
    kc = kc_ref[...]
    vc = vc_ref[...]
    imp = jnp.zeros((tq, nch), F32)
    for pair in range(NSA_GROUP_SIZE // 2):
        qh = _split_pair(q_ref[:, pair * PAIR:(pair + 1) * PAIR])
        outs = []
        for h in range(2):
            s = _dot_nt(qh[h], kc) - _group_slope(g, 2 * pair + h) * df
            s = jnp.where(valid, s, NEG_INF)
            m = jnp.max(s, axis=-1, keepdims=True)
            e = jnp.where(valid, jnp.exp2(s - m), 0.0)
            p = e / jnp.maximum(jnp.sum(e, axis=-1, keepdims=True), 1e-30)
            outs.append(_dot(p.astype(BF16), vc))
            imp = imp + p
        oc_ref[:, pair * PAIR:(pair + 1) * PAIR] = jnp.where(low, outs[0], outs[1])

    pool = pool_ref[...]
    hi, mid, lo = _split3(imp)
    p_slc = _dot(hi, pool) + _dot(mid, pool) + _dot(lo, pool)

    blk = lax.broadcasted_iota(jnp.int32, (1, ns), 1)
    cur = t // SLC_BLOCK
    forced = (blk == 0) | (blk == cur) | (blk == cur - 1)
    val = jnp.where(forced, REMOVED, jnp.where(blk > cur, NEG_INF, p_slc))
    for _ in range(n_sel - 3):
        mx = jnp.max(val, axis=-1, keepdims=True)
        pick = jnp.min(jnp.where(val == mx, blk, ns), axis=-1, keepdims=True)
        val = jnp.where(blk == pick, REMOVED, val)
    sel = jnp.where(val == REMOVED, 1.0, 0.0)
    sel_ref[...] = sel
    any_sel = jnp.broadcast_to(jnp.max(sel, axis=0, keepdims=True), (8, ns)).astype(BF16)
    flag_ref[...] = (_dot(any_sel, tmap_ref[...]) > 0.5).astype(jnp.int32)


def _cmp_select(pj, cmp_kv, pool, tmap, *, batch, seq, tq, q_blk, n_sel):
    nq = seq // tq
    nch = cmp_kv.shape[3]
    ns = pool.shape[1]
    gw = NSA_GROUP_SIZE * HEAD_DIM
    nseg = CMP_SEGMENTS if nq % CMP_SEGMENTS == 0 else 1
    nt = nq // nseg
    out_shape = (jax.ShapeDtypeStruct((batch * seq, NSA_HEADS * HEAD_DIM), F32),
                 jax.ShapeDtypeStruct((batch, NSA_GROUPS, seq, ns), F32),
                 jax.ShapeDtypeStruct((batch, NSA_GROUPS, nq, 8, LANES), jnp.int32))
    outs = ()
    for seg in range(nseg):
        ext = (seg + 1) * nch // nseg
        first = seg * nt
        outs = pl.pallas_call(
            functools.partial(_cmp_select_kernel, tq=tq, n_sel=n_sel, first_tile=first),
            out_shape=out_shape,
            grid=(batch, NSA_GROUPS, nt),
            in_specs=[pl.BlockSpec((tq, gw), lambda b, g, i, first=first: (b * nq + first + i, q_blk + g)),
                      pl.BlockSpec((None, None, None, ext, PAIR), lambda b, g, i: (b, 0, g, 0, 0)),
                      pl.BlockSpec((None, None, None, ext, PAIR), lambda b, g, i: (b, 1, g, 0, 0)),
                      pl.BlockSpec((ext, ns), lambda b, g, i: (0, 0)),
                      pl.BlockSpec((ns, LANES), lambda b, g, i: (0, 0))]
                     + [pl.BlockSpec(memory_space=pl.ANY)] * len(outs),
            out_specs=(pl.BlockSpec((tq, gw), lambda b, g, i, first=first: (b * nq + first + i, g)),
                       pl.BlockSpec((None, None, tq, ns), lambda b, g, i, first=first: (b, g, first + i, 0)),
                       pl.BlockSpec((None, None, None, 8, LANES),
                                    lambda b, g, i, first=first: (b, g, first + i, 0, 0))),
            input_output_aliases={5 + k: k for k in range(len(outs))},
            compiler_params=_cparams("parallel", "parallel", "parallel"),
            name=f"nsa_cmp_select_{seg}",
        )(pj, cmp_kv, cmp_kv, pool, tmap, *outs)
    return outs


def _nsa_kernel(flags_ref, q_ref, ks_ref, vs_ref, kwp_ref, vwp_ref, kwd_ref, vwd_ref, sel_ref, oc_ref, gate_ref,
                o_ref, m_ref, acc_ref, *, tile, nq):
    b = pl.program_id(0)
    g = pl.program_id(1)
    i = pl.program_id(2)
    nh = NSA_GROUP_SIZE
    ns = sel_ref.shape[1]
    bpt = tile // SLC_BLOCK
    low = _lane_is_low()
    row = lax.broadcasted_iota(jnp.int32, (tile, tile), 0)
    col = lax.broadcasted_iota(jnp.int32, (tile, tile), 1)
    relpos = lax.broadcasted_iota(jnp.int32, (1, tile), 1).astype(F32)
    sel = sel_ref[...].astype(BF16)
    blk_of_key = lax.broadcasted_iota(jnp.int32, (ns, tile), 1) // SLC_BLOCK
    blk_row = lax.broadcasted_iota(jnp.int32, (ns, tile), 0)
    qh = []
    for pair in range(nh // 2):
        qh.extend(_split_pair(q_ref[:, pair * PAIR:(pair + 1) * PAIR]))
    slopes = [_group_slope(g, r) for r in range(nh)]
    _flash_reset(m_ref, acc_ref)

    def alibi(r, j):
        return slopes[r] * (relpos + ((j - i) * tile).astype(F32))

    def selected(j):
        expand = jnp.where(blk_row == blk_of_key + j * bpt, 1.0, 0.0).astype(BF16)
        return _dot(sel, expand) > 0.5

    def attend(slot, k_tile, v_tile, j, bias):
        v_aug = _with_ones(v_tile)
        for r in range(nh):
            s = _dot_nt(qh[r], k_tile) + alibi(r, j) + bias
            _flash_step(s, v_aug, m_ref.at[slot + r], acc_ref.at[slot + r])

    def sel_step(j, keep):
        ks = pl.multiple_of(j * tile, tile)
        attend(0, ks_ref[pl.ds(ks, tile), :], vs_ref[pl.ds(ks, tile), :], j, jnp.where(keep, 0.0, NEG_INF))

    sel_step(i, selected(i) & (col <= row))

    def sel_body(j, carry):
        @pl.when(flags_ref[((b * NSA_GROUPS + g) * nq + i) * nq + j] != 0)
        def _():
            sel_step(j, selected(j))
        return carry

    lax.fori_loop(0, i, sel_body, 0)

    attend(nh, kwd_ref[...], vwd_ref[...], i, jnp.where(col <= row, 0.0, NEG_INF))

    @pl.when(i > 0)
    def _():
        attend(nh, kwp_ref[...], vwp_ref[...], i - 1, jnp.where(col > row, 0.0, NEG_INF))

    gates = jax.nn.sigmoid(gate_ref[...])

    def gate_col(r, branch):
        c0 = FOX_HEADS + 3 * r + branch
        c1 = c0 + 3 * nh
        return jnp.where(g == 0, gates[:, c0:c0 + 1], gates[:, c1:c1 + 1])

    for pair in range(nh // 2):
        o_c = oc_ref[:, pair * PAIR:(pair + 1) * PAIR]
        outs = []
        for h in range(2):
            r = 2 * pair + h
            o_s = _flash_out(acc_ref[r])
            o_w = _flash_out(acc_ref[nh + r])
            outs.append(gate_col(r, 0) * o_c + gate_col(r, 1) * o_s + gate_col(r, 2) * o_w)
        o_ref[:, pair * PAIR:(pair + 1) * PAIR] = jnp.where(low, outs[0], outs[1]).astype(o_ref.dtype)


def _nsa_attention(flags, pj, sel, o_c, small, *, batch, seq, tile, q_blk, ks_blk, vs_blk, kw_blk, vw_blk):
    nq = seq // tile
    ns = sel.shape[3]
    gw = NSA_GROUP_SIZE * HEAD_DIM
    slots = 2 * NSA_GROUP_SIZE
    cur = lambda b, g, i, f, base: (b * nq + i, base + g)
    prev = lambda b, g, i, f, base: (b * nq + jnp.maximum(i - 1, 0), base + g)
    grid_spec = pltpu.PrefetchScalarGridSpec(
        num_scalar_prefetch=1,
        grid=(batch, NSA_GROUPS, nq),
        in_specs=[pl.BlockSpec((tile, gw), functools.partial(cur, base=q_blk)),
                  pl.BlockSpec((seq, PAIR), lambda b, g, i, f: (b, ks_blk + g)),
                  pl.BlockSpec((seq, PAIR), lambda b, g, i, f: (b, vs_blk + g)),
                  pl.BlockSpec((tile, PAIR), functools.partial(prev, base=kw_blk)),
                  pl.BlockSpec((tile, PAIR), functools.partial(prev, base=vw_blk)),
                  pl.BlockSpec((tile, PAIR), functools.partial(cur, base=kw_blk)),
                  pl.BlockSpec((tile, PAIR), functools.partial(cur, base=vw_blk)),
                  pl.BlockSpec((None, None, tile, ns), lambda b, g, i, f: (b, g, i, 0)),
                  pl.BlockSpec((tile, gw), functools.partial(cur, base=0)),
                  pl.BlockSpec((tile, small.shape[1]), lambda b, g, i, f: (b * nq + i, 0))],
        out_specs=pl.BlockSpec((tile, gw), functools.partial(cur, base=0)),
        scratch_shapes=[pltpu.VMEM((slots, tile, LANES), F32), pltpu.VMEM((slots, tile, 2 * PAIR), F32)])
    return pl.pallas_call(
        functools.partial(_nsa_kernel, tile=tile, nq=nq),
        out_shape=jax.ShapeDtypeStruct((batch * seq, NSA_HEADS * HEAD_DIM), BF16),
        grid_spec=grid_spec,
        compiler_params=_cparams("parallel", "parallel", "arbitrary"),
        name="nsa_attention",
    )(flags, pj, pj, pj, pj, pj, pj, pj, sel, o_c, small)


def _outproj_kernel(h_ref, a_ref, b_ref, wa_ref, wb_ref, g_ref, o_ref):
    m = _dot(a_ref[...], wa_ref[...]) + _dot(b_ref[...], wb_ref[...])
    o_ref[...] = h_ref[...] + _rms(m, g_ref[...])


def _outproj(h2, a, b, wa, wb, g, *, tm):
    n, d = h2.shape
    const = lambda i: (0, 0)
    return pl.pallas_call(
        _outproj_kernel,
        out_shape=jax.ShapeDtypeStruct((n, d), F32),
        grid=(n // tm,),
        in_specs=[pl.BlockSpec((tm, d), lambda i: (i, 0)),
                  pl.BlockSpec((tm, a.shape[1]), lambda i: (i, 0)),
                  pl.BlockSpec((tm, b.shape[1]), lambda i: (i, 0)),
                  pl.BlockSpec(wa.shape, const),
                  pl.BlockSpec(wb.shape, const),
                  pl.BlockSpec((1, d), const)],
        out_specs=pl.BlockSpec((tm, d), lambda i: (i, 0)),
        compiler_params=_cparams("parallel"),
        name="attn_outproj",
    )(h2, a, b, wa, wb, g)


def _ffn_kernel(h_ref, gin_ref, gout_ref, wg_ref, wu_ref, wd_ref, o_ref, acc_ref, *, chunk):
    h = h_ref[...]
    ub = _rms(h, gin_ref[...]).astype(BF16)
    hidden = wg_ref.shape[1]
    for idx, c in enumerate(range(0, hidden, chunk)):
        gate = _dot(ub, wg_ref[:, c:c + chunk])
        up = _dot(ub, wu_ref[:, c:c + chunk])
        act = (gate * jax.nn.sigmoid(gate) * up).astype(BF16)
        part = _dot(act, wd_ref[c:c + chunk, :])
        if idx == 0:
            acc_ref[...] = part
        else:
            acc_ref[...] += part
    o_ref[...] = h + _rms(acc_ref[...], gout_ref[...])


def _ffn(h2, g_in, g_out, wg, wu, wd, *, tm, chunk):
    n, d = h2.shape
    const = lambda i: (0, 0)
    return pl.pallas_call(
        functools.partial(_ffn_kernel, chunk=chunk),
        out_shape=jax.ShapeDtypeStruct((n, d), F32),
        grid=(n // tm,),
        in_specs=[pl.BlockSpec((tm, d), lambda i: (i, 0)),
                  pl.BlockSpec((1, d), const),
                  pl.BlockSpec((1, d), const),
                  pl.BlockSpec(wg.shape, const),
                  pl.BlockSpec(wu.shape, const),
                  pl.BlockSpec(wd.shape, const)],
        out_specs=pl.BlockSpec((tm, d), lambda i: (i, 0)),
        scratch_shapes=[pltpu.VMEM((tm, d), F32)],
        compiler_params=_cparams("parallel"),
        name="swiglu_ffn",
    )(h2, g_in, g_out, wg, wu, wd)


def _pool_kernel(h_ref, halo_ref, gin_ref, gout_ref, w_ref, sc_ref, o_ref, ext_ref, *, tm, halo, tiles_per_seq):
    i = pl.program_id(0)
    first = (i % tiles_per_seq) == 0
    h = h_ref[...]
    u = _rms(h, gin_ref[...])
    uh = _rms(halo_ref[...], gin_ref[...])
    ext_ref[0:halo, :] = jnp.where(first, 0.0, uh)
    ext_ref[halo:, :] = u
    t = (i % tiles_per_seq) * tm + lax.broadcasted_iota(jnp.int32, (tm, 1), 0)
    group = h.shape[1] // len(POOL_WINDOWS)
    ys = []
    for gi, w in enumerate(POOL_WINDOWS):
        cols = slice(gi * group, (gi + 1) * group)
        total = u[:, cols]
        for j in range(1, w):
            total = total + ext_ref[halo - j:halo - j + tm, cols]
        count = jnp.minimum(t + 1, w).astype(F32)
        pooled = total / count - u[:, cols]
        ys.append(_dot(pooled.astype(BF16), w_ref[gi]))
    y = jnp.concatenate(ys, axis=-1) * sc_ref[...]
    o_ref[...] = h + _rms(y, gout_ref[...])


def _pool_mixer(h2, g_in, g_out, w_groups, scale, *, seq, tm):
    n, d = h2.shape
    halo = max(POOL_WINDOWS)
    tiles_per_seq = seq // tm
    ratio = tm // halo
    const = lambda i: (0, 0)
    return pl.pallas_call(
        functools.partial(_pool_kernel, tm=tm, halo=halo, tiles_per_seq=tiles_per_seq),
        out_shape=jax.ShapeDtypeStruct((n, d), F32),
        grid=(n // tm,),
        in_specs=[pl.BlockSpec((tm, d), lambda i: (i, 0)),
                  pl.BlockSpec((halo, d), lambda i: (jnp.maximum(i * ratio - 1, 0), 0)),
                  pl.BlockSpec((1, d), const),
                  pl.BlockSpec((1, d), const),
                  pl.BlockSpec(w_groups.shape, lambda i: (0, 0, 0)),
                  pl.BlockSpec((1, d), const)],
        out_specs=pl.BlockSpec((tm, d), lambda i: (i, 0)),
        scratch_shapes=[pltpu.VMEM((tm + halo, d), F32)],
        compiler_params=_cparams("parallel"),
        name="pool_mixer",
    )(h2, h2, g_in, g_out, w_groups, scale)


def _pack_in_weights(w_in):
    d = w_in.shape[0]
    hw = FOX_HEADS * HEAD_DIM
    kv = NSA_GROUPS * HEAD_DIM
    sizes = (hw, hw, hw, FOX_HEADS, NSA_HEADS * HEAD_DIM, kv, kv, kv, kv, kv, kv, 3 * NSA_HEADS)
    offs = np.concatenate([[0], np.cumsum(sizes)])
    fq, fk, fv, ff, nq, kc, vc, ks, vs, kw, vw, ng = [w_in[:, offs[j]:offs[j + 1]] for j in range(len(sizes))]
    scale = HEAD_DIM ** -0.5 * LOG2E

    def dup(w):
        w = w.reshape(d, NSA_GROUPS, 1, HEAD_DIM)
        return jnp.broadcast_to(w, (d, NSA_GROUPS, 2, HEAD_DIM)).reshape(d, NSA_GROUPS * PAIR)

    w_main = jnp.concatenate([fq * scale, fk, fv, nq * scale, dup(ks), dup(vs), dup(kw), dup(vw)], axis=1)
    w_cmp = jnp.concatenate([kc, vc], axis=1)
    pad = jnp.zeros((d, LANES - FOX_HEADS - 3 * NSA_HEADS), F32)
    w_small = jnp.concatenate([ff, ng, pad], axis=1)
    return w_main.astype(BF16), w_cmp.astype(BF16), w_small


def _mixer_fox_nsa(h2, g_in, g_out, w_in, b_f, cmp_pe, cmp_w1, cmp_w2, w_out, *, batch, seq):
    tile = WINDOW
    assert seq % tile == 0 and seq // tile <= LANES
    w_main, w_cmp, w_small = _pack_in_weights(w_in)
    pj, pc, small, nrm = _inproj(h2, g_in, w_main, w_cmp, w_small, tm=tile)
    npair = FOX_HEADS // 2
    q_blk, k_blk, v_blk, nq_blk = 0, npair, 2 * npair, 3 * npair
    ks_blk = nq_blk + NSA_HEADS // 2
    vs_blk, kw_blk, vw_blk = ks_blk + NSA_GROUPS, ks_blk + 2 * NSA_GROUPS, ks_blk + 3 * NSA_GROUPS

    c = _forget_cumsum(small, b_f, batch=batch, seq=seq, tc=tile)
    nq = seq // tile
    fox_tq, fox_tk = min(FOX_TQ, seq), min(FOX_TK, seq)
    c_tiles = c.reshape(batch, npair, 2, seq // fox_tk, fox_tk).transpose(0, 1, 3, 2, 4)
    live = _fox_live_tiles(c, nrm, batch=batch, seq=seq, tq=fox_tq, tk=fox_tk, tm=tile)
    o_fox = _fox_attention(live, pj, c_tiles, batch=batch, seq=seq, tq=fox_tq, tk=fox_tk,
                           q_blk=q_blk, k_blk=k_blk, v_blk=v_blk)

    nch = seq // CMP_STRIDE
    ns = seq // SLC_BLOCK
    n_sel = min(N_SELECT, ns)
    kv = NSA_GROUPS * HEAD_DIM
    tch = pc.reshape(batch, nch, CMP_STRIDE, 2, NSA_GROUPS, HEAD_DIM)
    tch = tch.transpose(0, 3, 4, 1, 2, 5).reshape(batch, 2, NSA_GROUPS, nch, CMP_STRIDE * HEAD_DIM)
    pe8 = jnp.broadcast_to(cmp_pe.reshape(2, 1, CMP_BLOCK * HEAD_DIM), (2, 8, CMP_BLOCK * HEAD_DIM)).astype(BF16)
    w2dup = jnp.concatenate([cmp_w2, cmp_w2], axis=-1).astype(BF16)
    cmp_kv = _compress(tch, pe8, cmp_w1.astype(BF16), w2dup)

    ratio = SLC_BLOCK // CMP_STRIDE
    r = CMP_BLOCK // CMP_STRIDE
    n_idx = np.arange(nch)[:, None]
    b_idx = np.arange(ns)[None, :]
    pool_np = ((n_idx >= ratio * b_idx - (r - 1)) & (n_idx <= ratio * b_idx + ratio - 1)
               & (n_idx < nch - r + 1)).astype(np.float32)
    pool = jnp.asarray(pool_np, dtype=BF16)
    tmap_np = (np.arange(ns)[:, None] // (tile // SLC_BLOCK) == np.arange(LANES)[None, :]).astype(np.float32)
    o_c, sel, flag_blocks = _cmp_select(pj, cmp_kv, pool, jnp.asarray(tmap_np, dtype=BF16), batch=batch, seq=seq,
                                        tq=tile, q_blk=nq_blk // 2, n_sel=n_sel)
    flags = flag_blocks[:, :, :, 0, :nq].reshape(-1)
    o_nsa = _nsa_attention(flags, pj, sel, o_c, small, batch=batch, seq=seq, tile=tile,
                           q_blk=nq_blk // 2, ks_blk=ks_blk, vs_blk=vs_blk, kw_blk=kw_blk, vw_blk=vw_blk)

    hw = FOX_HEADS * HEAD_DIM
    return _outproj(h2, o_fox, o_nsa, w_out[:hw].astype(BF16), w_out[hw:].astype(BF16), g_out, tm=tile)


def kernel(x, norm_g, attn_w_in, fox_b_f, nsa_cmp_pe, nsa_cmp_w1, nsa_cmp_w2, attn_w_out,
           pool_w, pool_scale, ffn_w_gate, ffn_w_up, ffn_w_down):
    batch, seq, d = x.shape
    depth = norm_g.shape[0]
    tm = 512 if seq % 512 == 0 else seq
    h = x.reshape(batch * seq, d)
    for layer in range(depth):
        g = norm_g[layer].reshape(4, 1, d)
        i = layer // 2
        if layer % 2 == 0:
            h = _mixer_fox_nsa(h, g[0], g[1], attn_w_in[i], fox_b_f[i], nsa_cmp_pe[i], nsa_cmp_w1[i],
                               nsa_cmp_w2[i], attn_w_out[i], batch=batch, seq=seq)
        else:
            h = _pool_mixer(h, g[0], g[1], pool_w[i].astype(BF16), pool_scale[i].reshape(1, d), seq=seq, tm=tm)
        h = _ffn(h, g[2], g[3], ffn_w_gate[layer].astype(BF16), ffn_w_up[layer].astype(BF16),
                 ffn_w_down[layer].astype(BF16), tm=tm, chunk=256)
    return h.reshape(batch, seq, d)
```

```python
import functools

import numpy as np
import jax
import jax.numpy as jnp
from jax import lax
from jax.experimental import pallas as pl
from jax.experimental.pallas import tpu as pltpu

F32 = jnp.float32
BF16 = jnp.bfloat16

HEAD_DIM = 64
FOX_HEADS = 8
NSA_HEADS = 8
NSA_GROUPS = 2
NSA_GROUP_SIZE = NSA_HEADS // NSA_GROUPS
CMP_BLOCK = 32
CMP_STRIDE = 16
SLC_BLOCK = 64
N_SELECT = 16
WINDOW = 512
POOL_WINDOWS = (2, 4, 8, 16)
RMS_EPS = 1e-6
NEG_INF = -1e30
BIG = 1e30
REMOVED = -3e38
LANES = 128
PAIR = 2 * HEAD_DIM
VMEM_LIMIT = 56 * 1024 * 1024
CMP_SEGMENTS = 4
UNDERFLOW_BITS = 160.0
NORM_SLACK = 1.02
FOX_TQ, FOX_TK = 1024, 1024

LOG2E = 1.4426950408889634
ALIBI_SLOPES_LOG2 = tuple(float(2.0 ** (-8.0 * (i + 1.0) / NSA_HEADS)) * LOG2E for i in range(NSA_HEADS))


def _cparams(*sem):
    return pltpu.CompilerParams(dimension_semantics=sem, vmem_limit_bytes=VMEM_LIMIT)


def _rms(x, g):
    return x * lax.rsqrt(jnp.mean(x * x, axis=-1, keepdims=True) + RMS_EPS) * g


def _dot(a, b):
    return jnp.dot(a, b, preferred_element_type=F32)


def _dot_nt(a, b):
    return lax.dot_general(a, b, (((1,), (1,)), ((), ())), preferred_element_type=F32)


def _inproj_kernel(x_ref, g_ref, w_ref, wc_ref, ws_ref, hmap_ref, o_ref, oc_ref, os_ref, nrm_ref, *, fox_width):
    u = _rms(x_ref[...], g_ref[...])
    ub = u.astype(BF16)
    norms = []
    for c in range(0, o_ref.shape[1], fox_width):
        r = _dot(ub, w_ref[:, c:c + fox_width]).astype(BF16)
        o_ref[:, c:c + fox_width] = r
        if c < 2 * fox_width:
            rf = r.astype(F32)
            sq = _dot((rf * rf).astype(BF16), hmap_ref[...])
            norms.append(jnp.max(sq, axis=0, keepdims=True))
    oc_ref[...] = _dot(ub, wc_ref[...]).astype(BF16)
    os_ref[...] = jnp.dot(u, ws_ref[...], preferred_element_type=F32,
                          precision=lax.Precision.HIGHEST)
    row = lax.broadcasted_iota(jnp.int32, nrm_ref.shape, 0)
    nrm_ref[...] = jnp.where(row == 0, norms[0], jnp.where(row == 1, norms[1], 0.0))


def _inproj(x2, g, w_main, w_cmp, w_small, *, tm):
    n, d = x2.shape
    cm, cc, cs = w_main.shape[1], w_cmp.shape[1], w_small.shape[1]
    fox_width = FOX_HEADS * HEAD_DIM
    hmap = jnp.asarray((np.arange(fox_width)[:, None] // HEAD_DIM == np.arange(LANES)[None, :]).astype(np.float32),
                       dtype=BF16)
    const = lambda i: (0, 0)
    return pl.pallas_call(
        functools.partial(_inproj_kernel, fox_width=fox_width),
        out_shape=(jax.ShapeDtypeStruct((n, cm), BF16),
                   jax.ShapeDtypeStruct((n, cc), BF16),
                   jax.ShapeDtypeStruct((n, cs), F32),
                   jax.ShapeDtypeStruct((n // tm, 8, LANES), F32)),
        grid=(n // tm,),
        in_specs=[pl.BlockSpec((tm, d), lambda i: (i, 0)),
                  pl.BlockSpec((1, d), const),
                  pl.BlockSpec((d, cm), const),
                  pl.BlockSpec((d, cc), const),
                  pl.BlockSpec((d, cs), const),
                  pl.BlockSpec((fox_width, LANES), const)],
        out_specs=(pl.BlockSpec((tm, cm), lambda i: (i, 0)),
                   pl.BlockSpec((tm, cc), lambda i: (i, 0)),
                   pl.BlockSpec((tm, cs), lambda i: (i, 0)),
                   pl.BlockSpec((None, 8, LANES), lambda i: (i, 0, 0))),
        compiler_params=_cparams("parallel"),
        name="inproj",
    )(x2, g, w_main, w_cmp, w_small, hmap)


def _forget_cumsum_kernel(sm_ref, bf_ref, tri_ref, o_ref, carry_ref):
    @pl.when(pl.program_id(1) == 0)
    def _():
        carry_ref[...] = jnp.zeros_like(carry_ref)

    zt = sm_ref[...].T
    z = zt[0:FOX_HEADS, :] + bf_ref[...]
    log_f = jnp.minimum(z, 0.0) - jnp.log1p(jnp.exp(-jnp.abs(z)))
    c = jnp.dot(log_f, tri_ref[...], preferred_element_type=F32,
                precision=lax.Precision.HIGHEST) + carry_ref[...]
    o_ref[...] = c * LOG2E
    carry_ref[...] = c[:, c.shape[1] - 1:]


def _forget_cumsum(small, b_f, *, batch, seq, tc):
    tri = jnp.asarray(np.triu(np.ones((tc, tc), np.float32)))
    nchunk = seq // tc
    return pl.pallas_call(
        _forget_cumsum_kernel,
        out_shape=jax.ShapeDtypeStruct((batch, FOX_HEADS, seq), F32),
        grid=(batch, nchunk),
        in_specs=[pl.BlockSpec((tc, small.shape[1]), lambda b, j: (b * nchunk + j, 0)),
                  pl.BlockSpec((FOX_HEADS, 1), lambda b, j: (0, 0)),
                  pl.BlockSpec((tc, tc), lambda b, j: (0, 0))],
        out_specs=pl.BlockSpec((None, FOX_HEADS, tc), lambda b, j: (b, 0, j)),
        scratch_shapes=[pltpu.VMEM((FOX_HEADS, 1), F32)],
        compiler_params=_cparams("parallel", "arbitrary"),
        name="forget_cumsum",
    )(small, b_f.reshape(FOX_HEADS, 1), tri)


def _lane_is_low():
    return lax.broadcasted_iota(jnp.int32, (1, PAIR), 1) < HEAD_DIM


def _with_ones(v_tile):
    return jnp.concatenate([v_tile, jnp.ones_like(v_tile)], axis=1)


def _flash_step(s, v, m_ref, acc_ref, l_ref=None):
    m_old = m_ref[...]
    m_new = jnp.maximum(m_old, jnp.max(s, axis=-1, keepdims=True))
    alpha = jnp.exp2(m_old - m_new)
    p = jnp.exp2(s - jnp.concatenate([m_new] * (s.shape[1] // LANES), axis=1))
    if l_ref is None:
        alpha_acc = jnp.concatenate([alpha, alpha], axis=1)
    else:
        alpha_acc = alpha
        l_ref[...] = alpha * l_ref[...] + jnp.sum(p, axis=-1, keepdims=True)
    acc_ref[...] = alpha_acc * acc_ref[...] + _dot(p.astype(BF16), v)
    m_ref[...] = m_new


def _flash_reset(m_ref, *sum_refs):
    m_ref[...] = jnp.full_like(m_ref, NEG_INF)
    for ref in sum_refs:
        ref[...] = jnp.zeros_like(ref)


def _flash_out(acc):
    return acc[:, :PAIR] / acc[:, PAIR:]


def _split_pair(q):
    low = _lane_is_low()
    zero = jnp.zeros_like(q)
    return jnp.where(low, q, zero), jnp.where(low, zero, q)


def _fox_kernel(live_ref, q_ref, k_ref, v_ref, c_ref, o_ref, m_ref, l_ref, acc_ref, *, tq, tk):
    b = pl.program_id(0)
    pair = pl.program_id(1)
    i = pl.program_id(2)
    nq = pl.num_programs(2)
    nk = c_ref.shape[0]
    qh = _split_pair(q_ref[...])
    _flash_reset(m_ref, l_ref, acc_ref)

    def head_step(h, j, mask):
        ks = pl.multiple_of(j * tk, tk)
        s = _dot_nt(qh[h], k_ref[pl.ds(ks, tk), :]) - c_ref[j, h:h + 1, :]
        if mask is not None:
            s = jnp.where(mask, s, NEG_INF)
        _flash_step(s, v_ref[pl.ds(ks, tk), :], m_ref.at[h], acc_ref.at[h], l_ref.at[h])

    def body(j, carry):
        live = [live_ref[(((b * FOX_HEADS) + 2 * pair + h) * nq + i) * nk + j] != 0 for h in range(2)]

        @pl.when(live[0] & live[1])
        def _():
            head_step(0, j, None)
            head_step(1, j, None)

        for h in range(2):
            @pl.when(live[h] & jnp.logical_not(live[1 - h]))
            def _(h=h):
                head_step(h, j, None)
        return carry

    n_full = (i * tq) // tk
    lax.fori_loop(0, n_full, body, 0)
    row = lax.broadcasted_iota(jnp.int32, (tq, tk), 0)
    col = lax.broadcasted_iota(jnp.int32, (tq, tk), 1)
    for d in range(max(1, tq // tk)):
        j = n_full + d
        for h in range(2):
            head_step(h, j, col + (j * tk - i * tq) <= row)
    o_ref[...] = jnp.where(_lane_is_low(), acc_ref[0] / l_ref[0], acc_ref[1] / l_ref[1]).astype(o_ref.dtype)


def _fox_live_tiles(c2, nrm, *, batch, seq, tq, tk, tm):
    nq, nk = seq // tq, seq // tk
    qn = nrm[:, 0, :FOX_HEADS].reshape(batch, nq, tq // tm, FOX_HEADS).max(axis=2)
    kn = nrm[:, 1, :FOX_HEADS].reshape(batch, seq // tm, FOX_HEADS).max(axis=1)
    bound = 2.0 * NORM_SLACK * jnp.sqrt(qn * kn[:, None, :])
    c_start = c2[:, :, ::tq]
    c_end = c2[:, :, tk - 1::tk]
    decay = c_end[:, :, None, :] - c_start[:, :, :, None]
    live = decay < UNDERFLOW_BITS + jnp.transpose(bound, (0, 2, 1))[..., None]
    return live.astype(jnp.int32).reshape(-1)


def _fox_attention(live, pj, c_tiles, *, batch, seq, tq, tk, q_blk, k_blk, v_blk):
    nq = seq // tq
    nk = seq // tk
    npair = FOX_HEADS // 2
    grid_spec = pltpu.PrefetchScalarGridSpec(
        num_scalar_prefetch=1,
        grid=(batch, npair, nq),
        in_specs=[pl.BlockSpec((tq, PAIR), lambda b, p, i, f: (b * nq + i, q_blk + p)),
                  pl.BlockSpec((seq, PAIR), lambda b, p, i, f: (b, k_blk + p)),
                  pl.BlockSpec((seq, PAIR), lambda b, p, i, f: (b, v_blk + p)),
                  pl.BlockSpec((None, None, nk, 2, tk), lambda b, p, i, f: (b, p, 0, 0, 0))],
        out_specs=pl.BlockSpec((tq, PAIR), lambda b, p, i, f: (b * nq + i, p)),
        scratch_shapes=[pltpu.VMEM((2, tq, LANES), F32), pltpu.VMEM((2, tq, LANES), F32),
                        pltpu.VMEM((2, tq, PAIR), F32)])
    return pl.pallas_call(
        functools.partial(_fox_kernel, tq=tq, tk=tk),
        out_shape=jax.ShapeDtypeStruct((batch * seq, FOX_HEADS * HEAD_DIM), BF16),
        grid_spec=grid_spec,
        compiler_params=_cparams("parallel", "parallel", "arbitrary"),
        name="fox_attention",
    )(live, pj, pj, pj, c_tiles)


def _gelu_tanh(x):
    return 0.5 * x * (1.0 + jnp.tanh(0.7978845608028654 * (x + 0.044715 * x * x * x)))


def _compress_kernel(t_ref, pe_ref, w1_ref, w2_ref, o_ref):
    half = w1_ref.shape[0] // 2
    t = t_ref[...]
    a = _dot(t, w1_ref[0:half, :])
    b = _dot(t, w1_ref[half:, :])
    pe_term = _dot(pe_ref[...], w1_ref[...])[0:1, :]
    nch = t.shape[0]
    h = a + pltpu.roll(b, nch - 1, axis=0) + pe_term
    o_ref[...] = _dot(_gelu_tanh(h).astype(BF16), w2_ref[...]).astype(o_ref.dtype)


def _compress(tchunks, pe8, w1, w2dup):
    b, two, g, nch, width = tchunks.shape
    return pl.pallas_call(
        _compress_kernel,
        out_shape=jax.ShapeDtypeStruct((b, two, g, nch, PAIR), BF16),
        grid=(b, two, g),
        in_specs=[pl.BlockSpec((None, None, None, nch, width), lambda bb, kv, gg: (bb, kv, gg, 0, 0)),
                  pl.BlockSpec((None, 8, 2 * width), lambda bb, kv, gg: (kv, 0, 0)),
                  pl.BlockSpec((None, 2 * width, HEAD_DIM), lambda bb, kv, gg: (kv, 0, 0)),
                  pl.BlockSpec((None, HEAD_DIM, PAIR), lambda bb, kv, gg: (kv, 0, 0))],
        out_specs=pl.BlockSpec((None, None, None, nch, PAIR), lambda bb, kv, gg: (bb, kv, gg, 0, 0)),
        compiler_params=_cparams("parallel", "parallel", "parallel"),
        name="nsa_compress",
    )(tchunks, pe8, w1, w2dup)


def _group_slope(g, r):
    return jnp.where(g == 0, ALIBI_SLOPES_LOG2[r], ALIBI_SLOPES_LOG2[NSA_GROUP_SIZE + r]).astype(F32)


def _split3(x):
    hi = x.astype(BF16)
    r1 = x - hi.astype(F32)
    mid = r1.astype(BF16)
    lo = (r1 - mid.astype(F32)).astype(BF16)
    return hi, mid, lo


def _cmp_select_kernel(q_ref, kc_ref, vc_ref, pool_ref, tmap_ref, oc_ref, sel_ref, flag_ref, *, tq, n_sel, nseg):
    i = pl.program_id(2)
    nq = pl.num_programs(2)
    total = kc_ref.shape[0]
    for seg in range(nseg):
        @pl.when((i * nseg) // nq == seg)
        def _(seg=seg):
            _cmp_select_body(q_ref, kc_ref, vc_ref, pool_ref, tmap_ref, oc_ref, sel_ref, flag_ref,
                             tq=tq, n_sel=n_sel, nch=(seg + 1) * total // nseg)


def _cmp_select_body(q_ref, kc_ref, vc_ref, pool_ref, tmap_ref, oc_ref, sel_ref, flag_ref, *, tq, n_sel, nch):
    g = pl.program_id(1)
    t0 = pl.program_id(2) * tq
    ns = pool_ref.shape[1]
    low = _lane_is_low()
    t = t0 + lax.broadcasted_iota(jnp.int32, (tq, 1), 0)
    cmp_end = lax.broadcasted_iota(jnp.int32, (1, nch), 1) * CMP_STRIDE + (CMP_BLOCK - 1)
    d = t - cmp_end
    valid = d >= 0
    df = d.astype(F32)
    kc = kc_ref[0:nch, :]
    vc = vc_ref[0:nch, :]
    imp = jnp.zeros((tq, nch), F32)
    for pair in range(NSA_GROUP_SIZE // 2):
        qh = _split_pair(q_ref[:, pair * PAIR:(pair + 1) * PAIR])
        outs = []
        for h in range(2):
            s = _dot_nt(qh[h], kc) - _group_slope(g, 2 * pair + h) * df
            s = jnp.where(valid, s, NEG_INF)
            m = jnp.max(s, axis=-1, keepdims=True)
            e = jnp.where(valid, jnp.exp2(s - m), 0.0)
            p = e / jnp.maximum(jnp.sum(e, axis=-1, keepdims=True), 1e-30)
            outs.append(_dot(p.astype(BF16), vc))
            imp = imp + p
        oc_ref[:, pair * PAIR:(pair + 1) * PAIR] = jnp.where(low, outs[0], outs[1])

    pool = pool_ref[0:nch, :]
    hi, mid, lo = _split3(imp)
    p_slc = _dot(hi, pool) + _dot(mid, pool) + _dot(lo, pool)

    blk = lax.broadcasted_iota(jnp.int32, (1, ns), 1)
    cur = t // SLC_BLOCK
    forced = (blk == 0) | (blk == cur) | (blk == cur - 1)
    val = jnp.where(forced, REMOVED, jnp.where(blk > cur, NEG_INF, p_slc))
    for _ in range(n_sel - 3):
        mx = jnp.max(val, axis=-1, keepdims=True)
        pick = jnp.min(jnp.where(val == mx, blk, ns), axis=-1, keepdims=True)
        val = jnp.where(blk == pick, REMOVED, val)
    sel = jnp.where(val == REMOVED, 1.0, 0.0)
    sel_ref[...] = sel
    any_sel = jnp.broadcast_to(jnp.max(sel, axis=0, keepdims=True), (8, ns)).astype(BF16)
    flag_ref[...] = (_dot(any_sel, tmap_ref[...]) > 0.5).astype(jnp.int32)


def _cmp_select(pj, cmp_kv, pool, tmap, *, batch, seq, tq, q_blk, n_sel):
    nq = seq // tq
    nch = cmp_kv.shape[3]
    ns = pool.shape[1]
    gw = NSA_GROUP_SIZE * HEAD_DIM
    nseg = CMP_SEGMENTS if nq % CMP_SEGMENTS == 0 else 1
    return pl.pallas_call(
        functools.partial(_cmp_select_kernel, tq=tq, n_sel=n_sel, nseg=nseg),
        out_shape=(jax.ShapeDtypeStruct((batch * seq, NSA_HEADS * HEAD_DIM), F32),
                   jax.ShapeDtypeStruct((batch, NSA_GROUPS, seq, ns), F32),
                   jax.ShapeDtypeStruct((batch, NSA_GROUPS, nq, 8, LANES), jnp.int32)),
        grid=(batch, NSA_GROUPS, nq),
        in_specs=[pl.BlockSpec((tq, gw), lambda b, g, i: (b * nq + i, q_blk + g)),
                  pl.BlockSpec((None, None, None, nch, PAIR), lambda b, g, i: (b, 0, g, 0, 0)),
                  pl.BlockSpec((None, None, None, nch, PAIR), lambda b, g, i: (b, 1, g, 0, 0)),
                  pl.BlockSpec((nch, ns), lambda b, g, i: (0, 0)),
                  pl.BlockSpec((ns, LANES), lambda b, g, i: (0, 0))],
        out_specs=(pl.BlockSpec((tq, gw), lambda b, g, i: (b * nq + i, g)),
                   pl.BlockSpec((None, None, tq, ns), lambda b, g, i: (b, g, i, 0)),
                   pl.BlockSpec((None, None, None, 8, LANES), lambda b, g, i: (b, g, i, 0, 0))),
        compiler_params=_cparams("parallel", "parallel", "parallel"),
        name="nsa_cmp_select",
    )(pj, cmp_kv, cmp_kv, pool, tmap)


def _nsa_kernel(flags_ref, q_ref, ks_ref, vs_ref, kwp_ref, vwp_ref, kwd_ref, vwd_ref, sel_ref, oc_ref, gate_ref,
                o_ref, m_ref, acc_ref, *, tile, nq):
    b = pl.program_id(0)
    g = pl.program_id(1)
    i = pl.program_id(2)
    nh = NSA_GROUP_SIZE
    ns = sel_ref.shape[1]
    bpt = tile // SLC_BLOCK
    low = _lane_is_low()
    row = lax.broadcasted_iota(jnp.int32, (tile, tile), 0)
    col = lax.broadcasted_iota(jnp.int32, (tile, tile), 1)
    relpos = lax.broadcasted_iota(jnp.int32, (1, tile), 1).astype(F32)
    sel = sel_ref[...].astype(BF16)
    blk_of_key = lax.broadcasted_iota(jnp.int32, (ns, tile), 1) // SLC_BLOCK
    blk_row = lax.broadcasted_iota(jnp.int32, (ns, tile), 0)
    qh = []
    for pair in range(nh // 2):
        qh.extend(_split_pair(q_ref[:, pair * PAIR:(pair + 1) * PAIR]))
    slopes = [_group_slope(g, r) for r in range(nh)]
    _flash_reset(m_ref, acc_ref)

    def alibi(r, j):
        return slopes[r] * (relpos + ((j - i) * tile).astype(F32))

    def selected(j):
        expand = jnp.where(blk_row == blk_of_key + j * bpt, 1.0, 0.0).astype(BF16)
        return _dot(sel, expand) > 0.5

    def attend(slot, k_tile, v_tile, j, bias):
        v_aug = _with_ones(v_tile)
        for r in range(nh):
            s = _dot_nt(qh[r], k_tile) + alibi(r, j) + bias
            _flash_step(s, v_aug, m_ref.at[slot + r], acc_ref.at[slot + r])

    def sel_step(j, keep):
        ks = pl.multiple_of(j * tile, tile)
        attend(0, ks_ref[pl.ds(ks, tile), :], vs_ref[pl.ds(ks, tile), :], j, jnp.where(keep, 0.0, NEG_INF))

    sel_step(i, selected(i) & (col <= row))

    def sel_body(j, carry):
        @pl.when(flags_ref[((b * NSA_GROUPS + g) * nq + i) * nq + j] != 0)
        def _():
            sel_step(j, selected(j))
        return carry

    lax.fori_loop(0, i, sel_body, 0)

    attend(nh, kwd_ref[...], vwd_ref[...], i, jnp.where(col <= row, 0.0, NEG_INF))

    @pl.when(i > 0)
    def _():
        attend(nh, kwp_ref[...], vwp_ref[...], i - 1, jnp.where(col > row, 0.0, NEG_INF))

    gates = jax.nn.sigmoid(gate_ref[...])

    def gate_col(r, branch):
        c0 = FOX_HEADS + 3 * r + branch
        c1 = c0 + 3 * nh
        return jnp.where(g == 0, gates[:, c0:c0 + 1], gates[:, c1:c1 + 1])

    for pair in range(nh // 2):
        o_c = oc_ref[:, pair * PAIR:(pair + 1) * PAIR]
        outs = []
        for h in range(2):
            r = 2 * pair + h
            o_s = _flash_out(acc_ref[r])
            o_w = _flash_out(acc_ref[nh + r])
            outs.append(gate_col(r, 0) * o_c + gate_col(r, 1) * o_s + gate_col(r, 2) * o_w)
        o_ref[:, pair * PAIR:(pair + 1) * PAIR] = jnp.where(low, outs[0], outs[1]).astype(o_ref.dtype)


def _nsa_attention(flags, pj, sel, o_c, small, *, batch, seq, tile, q_blk, ks_blk, vs_blk, kw_blk, vw_blk):
    nq = seq // tile
    ns = sel.shape[3]
    gw = NSA_GROUP_SIZE * HEAD_DIM
    slots = 2 * NSA_GROUP_SIZE
    cur = lambda b, g, i, f, base: (b * nq + i, base + g)
    prev = lambda b, g, i, f, base: (b * nq + jnp.maximum(i - 1, 0), base + g)
    grid_spec = pltpu.PrefetchScalarGridSpec(
        num_scalar_prefetch=1,
        grid=(batch, NSA_GROUPS, nq),
        in_specs=[pl.BlockSpec((tile, gw), functools.partial(cur, base=q_blk)),
                  pl.BlockSpec((seq, PAIR), lambda b, g, i, f: (b, ks_blk + g)),
                  pl.BlockSpec((seq, PAIR), lambda b, g, i, f: (b, vs_blk + g)),
                  pl.BlockSpec((tile, PAIR), functools.partial(prev, base=kw_blk)),
                  pl.BlockSpec((tile, PAIR), functools.partial(prev, base=vw_blk)),
                  pl.BlockSpec((tile, PAIR), functools.partial(cur, base=kw_blk)),
                  pl.BlockSpec((tile, PAIR), functools.partial(cur, base=vw_blk)),
                  pl.BlockSpec((None, None, tile, ns), lambda b, g, i, f: (b, g, i, 0)),
                  pl.BlockSpec((tile, gw), functools.partial(cur, base=0)),
                  pl.BlockSpec((tile, small.shape[1]), lambda b, g, i, f: (b * nq + i, 0))],
        out_specs=pl.BlockSpec((tile, gw), functools.partial(cur, base=0)),
        scratch_shapes=[pltpu.VMEM((slots, tile, LANES), F32), pltpu.VMEM((slots, tile, 2 * PAIR), F32)])
    return pl.pallas_call(
        functools.partial(_nsa_kernel, tile=tile, nq=nq),
        out_shape=jax.ShapeDtypeStruct((batch * seq, NSA_HEADS * HEAD_DIM), BF16),
        grid_spec=grid_spec,
        compiler_params=_cparams("parallel", "parallel", "arbitrary"),
        name="nsa_attention",
    )(flags, pj, pj, pj, pj, pj, pj, pj, sel, o_c, small)


def _ffn_kernel(h_ref, gin_ref, gout_ref, wg_ref, wu_ref, wd_ref, *rest, chunk):
    o_ref, acc_ref = rest[-2:]
    h = h_ref[...]
    if len(rest) > 2:
        a_ref, b_ref, wa_ref, wb_ref, gmix_ref = rest[:5]
        h = h + _rms(_dot(a_ref[...], wa_ref[...]) + _dot(b_ref[...], wb_ref[...]), gmix_ref[...])
    ub = _rms(h, gin_ref[...]).astype(BF16)
    hidden = wg_ref.shape[1]
    for idx, c in enumerate(range(0, hidden, chunk)):
        gate = _dot(ub, wg_ref[:, c:c + chunk])
        up = _dot(ub, wu_ref[:, c:c + chunk])
        act = (gate * jax.nn.sigmoid(gate) * up).astype(BF16)
        part = _dot(act, wd_ref[c:c + chunk, :])
        if idx == 0:
            acc_ref[...] = part
        else:
            acc_ref[...] += part
    o_ref[...] = h + _rms(acc_ref[...], gout_ref[...])


def _ffn(h2, g_in, g_out, wg, wu, wd, *, tm, chunk, mix=()):
    n, d = h2.shape
    const = lambda i: (0, 0)
    rows = lambda i: (i, 0)
    mix_specs = []
    if mix:
        a, b, wa, wb, g_mix = mix
        mix_specs = [pl.BlockSpec((tm, a.shape[1]), rows), pl.BlockSpec((tm, b.shape[1]), rows),
                     pl.BlockSpec(wa.shape, const), pl.BlockSpec(wb.shape, const), pl.BlockSpec((1, d), const)]
    return pl.pallas_call(
        functools.partial(_ffn_kernel, chunk=chunk),
        out_shape=jax.ShapeDtypeStruct((n, d), F32),
        grid=(n // tm,),
        in_specs=[pl.BlockSpec((tm, d), rows),
                  pl.BlockSpec((1, d), const),
                  pl.BlockSpec((1, d), const),
                  pl.BlockSpec(wg.shape, const),
                  pl.BlockSpec(wu.shape, const),
                  pl.BlockSpec(wd.shape, const)] + mix_specs,
        out_specs=pl.BlockSpec((tm, d), rows),
        scratch_shapes=[pltpu.VMEM((tm, d), F32)],
        compiler_params=_cparams("parallel"),
        name="swiglu_ffn",
    )(h2, g_in, g_out, wg, wu, wd, *mix)


def _pool_kernel(h_ref, halo_ref, gin_ref, gout_ref, w_ref, sc_ref, o_ref, ext_ref, *, tm, halo, tiles_per_seq):
    i = pl.program_id(0)
    first = (i % tiles_per_seq) == 0
    h = h_ref[...]
    u = _rms(h, gin_ref[...])
    uh = _rms(halo_ref[...], gin_ref[...])
    ext_ref[0:halo, :] = jnp.where(first, 0.0, uh)
    ext_ref[halo:, :] = u
    t = (i % tiles_per_seq) * tm + lax.broadcasted_iota(jnp.int32, (tm, 1), 0)
    group = h.shape[1] // len(POOL_WINDOWS)
    ys = []
    for gi, w in enumerate(POOL_WINDOWS):
        cols = slice(gi * group, (gi + 1) * group)
        total = u[:, cols]
        for j in range(1, w):
            total = total + ext_ref[halo - j:halo - j + tm, cols]
        count = jnp.minimum(t + 1, w).astype(F32)
        pooled = total / count - u[:, cols]
        ys.append(_dot(pooled.astype(BF16), w_ref[gi]))
    y = jnp.concatenate(ys, axis=-1) * sc_ref[...]
    o_ref[...] = h + _rms(y, gout_ref[...])


def _pool_mixer(h2, g_in, g_out, w_groups, scale, *, seq, tm):
    n, d = h2.shape
    halo = max(POOL_WINDOWS)
    tiles_per_seq = seq // tm
    ratio = tm // halo
    const = lambda i: (0, 0)
    return pl.pallas_call(
        functools.partial(_pool_kernel, tm=tm, halo=halo, tiles_per_seq=tiles_per_seq),
        out_shape=jax.ShapeDtypeStruct((n, d), F32),
        grid=(n // tm,),
        in_specs=[pl.BlockSpec((tm, d), lambda i: (i, 0)),
                  pl.BlockSpec((halo, d), lambda i: (jnp.maximum(i * ratio - 1, 0), 0)),
                  pl.BlockSpec((1, d), const),
                  pl.BlockSpec((1, d), const),
                  pl.BlockSpec(w_groups.shape, lambda i: (0, 0, 0)),
                  pl.BlockSpec((1, d), const)],
        out_specs=pl.BlockSpec((tm, d), lambda i: (i, 0)),
        scratch_shapes=[pltpu.VMEM((tm + halo, d), F32)],
        compiler_params=_cparams("parallel"),
        name="pool_mixer",
    )(h2, h2, g_in, g_out, w_groups, scale)


def _pack_in_weights(w_in):
    d = w_in.shape[0]
    hw = FOX_HEADS * HEAD_DIM
    kv = NSA_GROUPS * HEAD_DIM
    sizes = (hw, hw, hw, FOX_HEADS, NSA_HEADS * HEAD_DIM, kv, kv, kv, kv, kv, kv, 3 * NSA_HEADS)
    offs = np.concatenate([[0], np.cumsum(sizes)])
    fq, fk, fv, ff, nq, kc, vc, ks, vs, kw, vw, ng = [w_in[:, offs[j]:offs[j + 1]] for j in range(len(sizes))]
    scale = HEAD_DIM ** -0.5 * LOG2E

    def dup(w):
        w = w.reshape(d, NSA_GROUPS, 1, HEAD_DIM)
        return jnp.broadcast_to(w, (d, NSA_GROUPS, 2, HEAD_DIM)).reshape(d, NSA_GROUPS * PAIR)

    w_main = jnp.concatenate([fq * scale, fk, fv, nq * scale, dup(ks), dup(vs), dup(kw), dup(vw)], axis=1)
    w_cmp = jnp.concatenate([kc, vc], axis=1)
    pad = jnp.zeros((d, LANES - FOX_HEADS - 3 * NSA_HEADS), F32)
    w_small = jnp.concatenate([ff, ng, pad], axis=1)
    return w_main.astype(BF16), w_cmp.astype(BF16), w_small


def _mixer_fox_nsa(h2, g_in, g_out, w_in, b_f, cmp_pe, cmp_w1, cmp_w2, w_out, *, batch, seq):
    tile = WINDOW
    assert seq % tile == 0 and seq // tile <= LANES
    w_main, w_cmp, w_small = _pack_in_weights(w_in)
    pj, pc, small, nrm = _inproj(h2, g_in, w_main, w_cmp, w_small, tm=tile)
    npair = FOX_HEADS // 2
    q_blk, k_blk, v_blk, nq_blk = 0, npair, 2 * npair, 3 * npair
    ks_blk = nq_blk + NSA_HEADS // 2
    vs_blk, kw_blk, vw_blk = ks_blk + NSA_GROUPS, ks_blk + 2 * NSA_GROUPS, ks_blk + 3 * NSA_GROUPS

    c = _forget_cumsum(small, b_f, batch=batch, seq=seq, tc=tile)
    nq = seq // tile
    fox_tq, fox_tk = min(FOX_TQ, seq), min(FOX_TK, seq)
    c_tiles = c.reshape(batch, npair, 2, seq // fox_tk, fox_tk).transpose(0, 1, 3, 2, 4)
    live = _fox_live_tiles(c, nrm, batch=batch, seq=seq, tq=fox_tq, tk=fox_tk, tm=tile)
    o_fox = _fox_attention(live, pj, c_tiles, batch=batch, seq=seq, tq=fox_tq, tk=fox_tk,
                           q_blk=q_blk, k_blk=k_blk, v_blk=v_blk)

    nch = seq // CMP_STRIDE
    ns = seq // SLC_BLOCK
    n_sel = min(N_SELECT, ns)
    kv = NSA_GROUPS * HEAD_DIM
    tch = pc.reshape(batch, nch, CMP_STRIDE, 2, NSA_GROUPS, HEAD_DIM)
    tch = tch.transpose(0, 3, 4, 1, 2, 5).reshape(batch, 2, NSA_GROUPS, nch, CMP_STRIDE * HEAD_DIM)
    pe8 = jnp.broadcast_to(cmp_pe.reshape(2, 1, CMP_BLOCK * HEAD_DIM), (2, 8, CMP_BLOCK * HEAD_DIM)).astype(BF16)
    w2dup = jnp.concatenate([cmp_w2, cmp_w2], axis=-1).astype(BF16)
    cmp_kv = _compress(tch, pe8, cmp_w1.astype(BF16), w2dup)

    ratio = SLC_BLOCK // CMP_STRIDE
    r = CMP_BLOCK // CMP_STRIDE
    n_idx = np.arange(nch)[:, None]
    b_idx = np.arange(ns)[None, :]
    pool_np = ((n_idx >= ratio * b_idx - (r - 1)) & (n_idx <= ratio * b_idx + ratio - 1)
               & (n_idx < nch - r + 1)).astype(np.float32)
    pool = jnp.asarray(pool_np, dtype=BF16)
    tmap_np = (np.arange(ns)[:, None] // (tile // SLC_BLOCK) == np.arange(LANES)[None, :]).astype(np.float32)
    o_c, sel, flag_blocks = _cmp_select(pj, cmp_kv, pool, jnp.asarray(tmap_np, dtype=BF16), batch=batch, seq=seq,
                                        tq=tile, q_blk=nq_blk // 2, n_sel=n_sel)
    flags = flag_blocks[:, :, :, 0, :nq].reshape(-1)
    o_nsa = _nsa_attention(flags, pj, sel, o_c, small, batch=batch, seq=seq, tile=tile,
                           q_blk=nq_blk // 2, ks_blk=ks_blk, vs_blk=vs_blk, kw_blk=kw_blk, vw_blk=vw_blk)

    hw = FOX_HEADS * HEAD_DIM
    return o_fox, o_nsa, w_out[:hw].astype(BF16), w_out[hw:].astype(BF16), g_out


def kernel(x, norm_g, attn_w_in, fox_b_f, nsa_cmp_pe, nsa_cmp_w1, nsa_cmp_w2, attn_w_out,
           pool_w, pool_scale, ffn_w_gate, ffn_w_up, ffn_w_down):
    batch, seq, d = x.shape
    depth = norm_g.shape[0]
    tm = 512 if seq % 512 == 0 else seq
    h = x.reshape(batch * seq, d)
    for layer in range(depth):
        g = norm_g[layer].reshape(4, 1, d)
        i = layer // 2
        mix = ()
        if layer % 2 == 0:
            mix = _mixer_fox_nsa(h, g[0], g[1], attn_w_in[i], fox_b_f[i], nsa_cmp_pe[i], nsa_cmp_w1[i],
                                 nsa_cmp_w2[i], attn_w_out[i], batch=batch, seq=seq)
        else:
            h = _pool_mixer(h, g[0], g[1], pool_w[i].astype(BF16), pool_scale[i].reshape(1, d), seq=seq, tm=tm)
        h = _ffn(h, g[2], g[3], ffn_w_gate[layer].astype(BF16), ffn_w_up[layer].astype(BF16),
                 ffn_w_down[layer].astype(BF16), tm=tm, chunk=256, mix=mix)
    return h.reshape(batch, seq, d)
```

```python
import functools

import numpy as np
import jax
import jax.numpy as jnp
from jax import lax
from jax.experimental import pallas as pl
from jax.experimental.pallas import tpu as pltpu

F32 = jnp.float32
BF16 = jnp.bfloat16

HEAD_DIM = 64
FOX_HEADS = 8
NSA_HEADS = 8
NSA_GROUPS = 2
NSA_GROUP_SIZE = NSA_HEADS // NSA_GROUPS
CMP_BLOCK = 32
CMP_STRIDE = 16
SLC_BLOCK = 64
N_SELECT = 16
WINDOW = 512
POOL_WINDOWS = (2, 4, 8, 16)
RMS_EPS = 1e-6
NEG_INF = -1e30
BIG = 1e30
REMOVED = -3e38
LANES = 128
PAIR = 2 * HEAD_DIM
VMEM_LIMIT = 56 * 1024 * 1024
CMP_SEGMENTS = 4
UNDERFLOW_BITS = 160.0
NORM_SLACK = 1.02
NORM_CHUNKS = (0, 1, 3, 4)
NRM_FOX_Q, NRM_FOX_K, NRM_NSA_Q, NRM_NSA_KS = range(4)
FOX_TQ, FOX_TK = 1024, 1024

LOG2E = 1.4426950408889634
ALIBI_SLOPES_LOG2 = tuple(float(2.0 ** (-8.0 * (i + 1.0) / NSA_HEADS)) * LOG2E for i in range(NSA_HEADS))


def _cparams(*sem):
    return pltpu.CompilerParams(dimension_semantics=sem, vmem_limit_bytes=VMEM_LIMIT)


def _rms(x, g):
    return x * lax.rsqrt(jnp.mean(x * x, axis=-1, keepdims=True) + RMS_EPS) * g


def _dot(a, b):
    return jnp.dot(a, b, preferred_element_type=F32)


def _dot_nt(a, b):
    return lax.dot_general(a, b, (((1,), (1,)), ((), ())), preferred_element_type=F32)


def _inproj_kernel(x_ref, g_ref, w_ref, wc_ref, ws_ref, hmap_ref, o_ref, oc_ref, os_ref, nrm_ref, *, fox_width):
    u = _rms(x_ref[...], g_ref[...])
    ub = u.astype(BF16)
    norms = []
    for c in range(0, o_ref.shape[1], fox_width):
        r = _dot(ub, w_ref[:, c:c + fox_width]).astype(BF16)
        o_ref[:, c:c + fox_width] = r
        if c // fox_width in NORM_CHUNKS:
            rf = r.astype(F32)
            sq = _dot((rf * rf).astype(BF16), hmap_ref[...])
            norms.append(jnp.max(sq, axis=0, keepdims=True))
    oc_ref[...] = _dot(ub, wc_ref[...]).astype(BF16)
    width = os_ref.shape[1]
    u_lo = (u - ub.astype(F32)).astype(BF16)
    both = _dot(ub, ws_ref[...])
    os_ref[...] = both[:, :width] + both[:, width:] + _dot(u_lo, ws_ref[:, :width])
    row = lax.broadcasted_iota(jnp.int32, nrm_ref.shape, 0)
    out = jnp.zeros(nrm_ref.shape, F32)
    for k, nk in enumerate(norms):
        out = jnp.where(row == k, nk, out)
    nrm_ref[...] = out


def _inproj(x2, g, w_main, w_cmp, w_small, *, tm):
    n, d = x2.shape
    cm, cc, cs = w_main.shape[1], w_cmp.shape[1], w_small.shape[1] // 2
    fox_width = FOX_HEADS * HEAD_DIM
    hmap = jnp.asarray((np.arange(fox_width)[:, None] // HEAD_DIM == np.arange(LANES)[None, :]).astype(np.float32),
                       dtype=BF16)
    const = lambda i: (0, 0)
    return pl.pallas_call(
        functools.partial(_inproj_kernel, fox_width=fox_width),
        out_shape=(jax.ShapeDtypeStruct((n, cm), BF16),
                   jax.ShapeDtypeStruct((n, cc), BF16),
                   jax.ShapeDtypeStruct((n, cs), F32),
                   jax.ShapeDtypeStruct((n // tm, 8, LANES), F32)),
        grid=(n // tm,),
        in_specs=[pl.BlockSpec((tm, d), lambda i: (i, 0)),
                  pl.BlockSpec((1, d), const),
                  pl.BlockSpec((d, cm), const),
                  pl.BlockSpec((d, cc), const),
                  pl.BlockSpec((d, 2 * cs), const),
                  pl.BlockSpec((fox_width, LANES), const)],
        out_specs=(pl.BlockSpec((tm, cm), lambda i: (i, 0)),
                   pl.BlockSpec((tm, cc), lambda i: (i, 0)),
                   pl.BlockSpec((tm, cs), lambda i: (i, 0)),
                   pl.BlockSpec((None, 8, LANES), lambda i: (i, 0, 0))),
        compiler_params=_cparams("parallel"),
        name="inproj",
    )(x2, g, w_main, w_cmp, w_small, hmap)


def _forget_cumsum_kernel(sm_ref, bf_ref, tri_ref, o_ref, carry_ref):
    @pl.when(pl.program_id(1) == 0)
    def _():
        carry_ref[...] = jnp.zeros_like(carry_ref)

    zt = sm_ref[...].T
    z = zt[0:FOX_HEADS, :] + bf_ref[...]
    log_f = jnp.minimum(z, 0.0) - jnp.log1p(jnp.exp(-jnp.abs(z)))
    c = jnp.dot(log_f, tri_ref[...], preferred_element_type=F32,
                precision=lax.Precision.HIGHEST) + carry_ref[...]
    o_ref[...] = c * LOG2E
    carry_ref[...] = c[:, c.shape[1] - 1:]


def _forget_cumsum(small, b_f, *, batch, seq, tc):
    tri = jnp.asarray(np.triu(np.ones((tc, tc), np.float32)))
    nchunk = seq // tc
    return pl.pallas_call(
        _forget_cumsum_kernel,
        out_shape=jax.ShapeDtypeStruct((batch, FOX_HEADS, seq), F32),
        grid=(batch, nchunk),
        in_specs=[pl.BlockSpec((tc, small.shape[1]), lambda b, j: (b * nchunk + j, 0)),
                  pl.BlockSpec((FOX_HEADS, 1), lambda b, j: (0, 0)),
                  pl.BlockSpec((tc, tc), lambda b, j: (0, 0))],
        out_specs=pl.BlockSpec((None, FOX_HEADS, tc), lambda b, j: (b, 0, j)),
        scratch_shapes=[pltpu.VMEM((FOX_HEADS, 1), F32)],
        compiler_params=_cparams("parallel", "arbitrary"),
        name="forget_cumsum",
    )(small, b_f.reshape(FOX_HEADS, 1), tri)


def _lane_is_low():
    return lax.broadcasted_iota(jnp.int32, (1, PAIR), 1) < HEAD_DIM


def _with_ones(v_tile):
    return jnp.concatenate([v_tile, jnp.ones_like(v_tile)], axis=1)


def _flash_step(s, v, m_ref, acc_ref, l_ref=None):
    m_old = m_ref[...]
    m_new = jnp.maximum(m_old, jnp.max(s, axis=-1, keepdims=True))
    alpha = jnp.exp2(m_old - m_new)
    p = jnp.exp2(s - jnp.concatenate([m_new] * (s.shape[1] // LANES), axis=1))
    if l_ref is None:
        alpha_acc = jnp.concatenate([alpha, alpha], axis=1)
    else:
        alpha_acc = alpha
        l_ref[...] = alpha * l_ref[...] + jnp.sum(p, axis=-1, keepdims=True)
    acc_ref[...] = alpha_acc * acc_ref[...] + _dot(p.astype(BF16), v)
    m_ref[...] = m_new


def _flash_reset(m_ref, *sum_refs):
    m_ref[...] = jnp.full_like(m_ref, NEG_INF)
    for ref in sum_refs:
        ref[...] = jnp.zeros_like(ref)


def _flash_out(acc):
    return acc[:, :PAIR] / acc[:, PAIR:]


def _split_pair(q):
    low = _lane_is_low()
    zero = jnp.zeros_like(q)
    return jnp.where(low, q, zero), jnp.where(low, zero, q)


def _fox_kernel(live_ref, q_ref, k_ref, v_ref, c_ref, o_ref, m_ref, l_ref, acc_ref, *, tq, tk, nq):
    b = pl.program_id(0)
    pair = pl.program_id(1)
    i = pl.program_id(2)
    nk = c_ref.shape[0]
    qh = _split_pair(q_ref[...])
    _flash_reset(m_ref, l_ref, acc_ref)

    def head_step(h, j, mask):
        ks = pl.multiple_of(j * tk, tk)
        s = _dot_nt(qh[h], k_ref[pl.ds(ks, tk), :]) - c_ref[j, h:h + 1, :]
        if mask is not None:
            s = jnp.where(mask, s, NEG_INF)
        _flash_step(s, v_ref[pl.ds(ks, tk), :], m_ref.at[h], acc_ref.at[h], l_ref.at[h])

    def body(j, carry):
        live = [live_ref[(((b * FOX_HEADS) + 2 * pair + h) * nq + i) * nk + j] != 0 for h in range(2)]

        @pl.when(live[0] & live[1])
        def _():
            head_step(0, j, None)
            head_step(1, j, None)

        for h in range(2):
            @pl.when(live[h] & jnp.logical_not(live[1 - h]))
            def _(h=h):
                head_step(h, j, None)
        return carry

    lax.fori_loop(0, i, body, 0)

    def diag_part(h, r0, nr, nkeys):
        ks = pl.multiple_of(i * tq, tq)
        rows = pl.ds(r0, nr)
        row = lax.broadcasted_iota(jnp.int32, (nr, nkeys), 0) + r0
        col = lax.broadcasted_iota(jnp.int32, (nr, nkeys), 1)
        s = _dot_nt(qh[h][r0:r0 + nr], k_ref[pl.ds(ks, nkeys), :]) - c_ref[i, h:h + 1, 0:nkeys]
        _flash_step(jnp.where(col <= row, s, NEG_INF), v_ref[pl.ds(ks, nkeys), :],
                    m_ref.at[h, rows], acc_ref.at[h, rows], l_ref.at[h, rows])

    half = tq // 2
    for h in range(2):
        diag_part(h, 0, half, half)
        diag_part(h, half, half, tq)
    o_ref[...] = jnp.where(_lane_is_low(), acc_ref[0] / l_ref[0], acc_ref[1] / l_ref[1]).astype(o_ref.dtype)


def _fox_live_tiles(c2, nrm, *, batch, seq, tq, tk, tm):
    nq, nk = seq // tq, seq // tk
    qn = nrm[:, 0, :FOX_HEADS].reshape(batch, nq, tq // tm, FOX_HEADS).max(axis=2)
    kn = nrm[:, 1, :FOX_HEADS].reshape(batch, seq // tm, FOX_HEADS).max(axis=1)
    bound = 2.0 * NORM_SLACK * jnp.sqrt(qn * kn[:, None, :])
    c_start = c2[:, :, ::tq]
    c_end = c2[:, :, tk - 1::tk]
    decay = c_end[:, :, None, :] - c_start[:, :, :, None]
    live = decay < UNDERFLOW_BITS + jnp.transpose(bound, (0, 2, 1))[..., None]
    return live.astype(jnp.int32).reshape(-1)


def _fox_attention(live, pj, c_tiles, *, batch, seq, tq, tk, q_blk, k_blk, v_blk):
    assert tq == tk and tq % (2 * LANES) == 0
    nq = seq // tq
    nk = seq // tk
    npair = FOX_HEADS // 2
    grid_spec = pltpu.PrefetchScalarGridSpec(
        num_scalar_prefetch=1,
        grid=(batch, npair, nq),
        in_specs=[pl.BlockSpec((tq, PAIR), lambda b, p, i, f: (b * nq + i, q_blk + p)),
                  pl.BlockSpec((seq, PAIR), lambda b, p, i, f: (b, k_blk + p)),
                  pl.BlockSpec((seq, PAIR), lambda b, p, i, f: (b, v_blk + p)),
                  pl.BlockSpec((None, None, nk, 2, tk), lambda b, p, i, f: (b, p, 0, 0, 0))],
        out_specs=pl.BlockSpec((tq, PAIR), lambda b, p, i, f: (b * nq + i, p)),
        scratch_shapes=[pltpu.VMEM((2, tq, LANES), F32), pltpu.VMEM((2, tq, LANES), F32),
                        pltpu.VMEM((2, tq, PAIR), F32)])
    return pl.pallas_call(
        functools.partial(_fox_kernel, tq=tq, tk=tk, nq=nq),
        out_shape=jax.ShapeDtypeStruct((batch * seq, FOX_HEADS * HEAD_DIM), BF16),
        grid_spec=grid_spec,
        compiler_params=_cparams("parallel", "parallel", "arbitrary"),
        name="fox_attention",
    )(live, pj, pj, pj, c_tiles)


def _gelu_tanh(x):
    return 0.5 * x * (1.0 + jnp.tanh(0.7978845608028654 * (x + 0.044715 * x * x * x)))


def _compress_kernel(t_ref, pe_ref, w1_ref, w2_ref, o_ref, nrm_ref):
    half = w1_ref.shape[0] // 2
    t = t_ref[...]
    a = _dot(t, w1_ref[0:half, :])
    b = _dot(t, w1_ref[half:, :])
    pe_term = _dot(pe_ref[...], w1_ref[...])[0:1, :]
    nch = t.shape[0]
    h = a + pltpu.roll(b, nch - 1, axis=0) + pe_term
    o = _dot(_gelu_tanh(h).astype(BF16), w2_ref[...]).astype(o_ref.dtype)
    o_ref[...] = o
    of = o.astype(F32)
    sq = 0.5 * jnp.sum(of * of, axis=-1, keepdims=True)
    nrm_ref[...] = jnp.broadcast_to(jnp.max(sq, axis=0, keepdims=True), nrm_ref.shape)


def _compress(tchunks, pe8, w1, w2dup):
    b, two, g, nch, width = tchunks.shape
    tile5 = lambda bb, kv, gg: (bb, kv, gg, 0, 0)
    return pl.pallas_call(
        _compress_kernel,
        out_shape=(jax.ShapeDtypeStruct((b, two, g, nch, PAIR), BF16),
                   jax.ShapeDtypeStruct((b, two, g, 8, LANES), F32)),
        grid=(b, two, g),
        in_specs=[pl.BlockSpec((None, None, None, nch, width), tile5),
                  pl.BlockSpec((None, 8, 2 * width), lambda bb, kv, gg: (kv, 0, 0)),
                  pl.BlockSpec((None, 2 * width, HEAD_DIM), lambda bb, kv, gg: (kv, 0, 0)),
                  pl.BlockSpec((None, HEAD_DIM, PAIR), lambda bb, kv, gg: (kv, 0, 0))],
        out_specs=(pl.BlockSpec((None, None, None, nch, PAIR), tile5),
                   pl.BlockSpec((None, None, None, 8, LANES), tile5)),
        compiler_params=_cparams("parallel", "parallel", "parallel"),
        name="nsa_compress",
    )(tchunks, pe8, w1, w2dup)


def _group_slope(g, r):
    return jnp.where(g == 0, ALIBI_SLOPES_LOG2[r], ALIBI_SLOPES_LOG2[NSA_GROUP_SIZE + r]).astype(F32)


def _split3(x):
    hi = x.astype(BF16)
    r1 = x - hi.astype(F32)
    mid = r1.astype(BF16)
    lo = (r1 - mid.astype(F32)).astype(BF16)
    return hi, mid, lo


def _cmp_select_kernel(first_ref, q_ref, kc_ref, vc_ref, pool_ref, tmap_ref, oc_ref, sel_ref, flag_ref, *,
                       tq, n_sel, nseg, nq):
    b = pl.program_id(0)
    g = pl.program_id(1)
    i = pl.program_id(2)
    chunk = kc_ref.shape[0] // nseg
    first = first_ref[(b * NSA_GROUPS + g) * nq + i]
    n_live = (i * nseg) // nq + 1 - first
    for k in range(1, nseg + 1):
        @pl.when(n_live == k)
        def _(k=k):
            _cmp_select_body(q_ref, kc_ref, vc_ref, pool_ref, tmap_ref, oc_ref, sel_ref, flag_ref,
                             tq=tq, n_sel=n_sel, start=pl.multiple_of(first * chunk, chunk), nch=k * chunk)


def _cmp_select_body(q_ref, kc_ref, vc_ref, pool_ref, tmap_ref, oc_ref, sel_ref, flag_ref, *,
                     tq, n_sel, start, nch):
    g = pl.program_id(1)
    t0 = pl.program_id(2) * tq
    ns = pool_ref.shape[1]
    low = _lane_is_low()
    t = t0 + lax.broadcasted_iota(jnp.int32, (tq, 1), 0)
    cmp_end = (start + lax.broadcasted_iota(jnp.int32, (1, nch), 1)) * CMP_STRIDE + (CMP_BLOCK - 1)
    rel_end = (cmp_end - t0).astype(F32)
    mask_bias = jnp.where(t >= cmp_end, 0.0, NEG_INF)
    row_valid = jnp.where(t >= CMP_BLOCK - 1, 1.0, 0.0)
    kc = kc_ref[pl.ds(start, nch), :]
    vc = vc_ref[pl.ds(start, nch), :]
    imp = jnp.zeros((tq, nch), F32)
    for pair in range(NSA_GROUP_SIZE // 2):
        qh = _split_pair(q_ref[:, pair * PAIR:(pair + 1) * PAIR])
        outs = []
        for h in range(2):
            s = _dot_nt(qh[h], kc) + _group_slope(g, 2 * pair + h) * rel_end + mask_bias
            e = jnp.exp2(s - jnp.max(s, axis=-1, keepdims=True))
            p = e * (row_valid / jnp.maximum(jnp.sum(e, axis=-1, keepdims=True), 1e-30))
            outs.append(_dot(p.astype(BF16), vc))
            imp = imp + p
        oc_ref[:, pair * PAIR:(pair + 1) * PAIR] = jnp.where(low, outs[0], outs[1])

    pool = pool_ref[pl.ds(start, nch), :]
    hi, mid, lo = _split3(imp)
    p_slc = _dot(hi, pool) + _dot(mid, pool) + _dot(lo, pool)

    blk = lax.broadcasted_iota(jnp.int32, (1, ns), 1)
    cur = t // SLC_BLOCK
    forced = (blk == 0) | (blk == cur) | (blk == cur - 1)
    val = jnp.where(forced, REMOVED, jnp.where(blk > cur, NEG_INF, p_slc))
    sel = jnp.where(forced, 1.0, 0.0)
    for _ in range(n_sel - 3):
        mx = jnp.max(val, axis=-1, keepdims=True)
        pick = jnp.min(jnp.where(val == mx, blk, ns), axis=-1, keepdims=True)
        hit = blk == pick
        sel = jnp.where(hit, 1.0, sel)
        val = jnp.where(hit, REMOVED, val)
    sel_ref[...] = sel
    any_sel = jnp.broadcast_to(jnp.max(sel, axis=0, keepdims=True), (8, ns)).astype(BF16)
    flag_ref[...] = (_dot(any_sel, tmap_ref[...]) > 0.5).astype(jnp.int32)


def _cmp_select(first_chunk, pj, cmp_kv, pool, tmap, *, batch, seq, tq, q_blk, n_sel, nseg):
    nq = seq // tq
    nch = cmp_kv.shape[3]
    ns = pool.shape[1]
    gw = NSA_GROUP_SIZE * HEAD_DIM
    grid_spec = pltpu.PrefetchScalarGridSpec(
        num_scalar_prefetch=1,
        grid=(batch, NSA_GROUPS, nq),
        in_specs=[pl.BlockSpec((tq, gw), lambda b, g, i, f: (b * nq + i, q_blk + g)),
                  pl.BlockSpec((None, None, None, nch, PAIR), lambda b, g, i, f: (b, 0, g, 0, 0)),
                  pl.BlockSpec((None, None, None, nch, PAIR), lambda b, g, i, f: (b, 1, g, 0, 0)),
                  pl.BlockSpec((nch, ns), lambda b, g, i, f: (0, 0)),
                  pl.BlockSpec((ns, LANES), lambda b, g, i, f: (0, 0))],
        out_specs=(pl.BlockSpec((tq, gw), lambda b, g, i, f: (b * nq + i, g)),
                   pl.BlockSpec((None, None, tq, ns), lambda b, g, i, f: (b, g, i, 0)),
                   pl.BlockSpec((None, None, None, 8, LANES), lambda b, g, i, f: (b, g, i, 0, 0))))
    return pl.pallas_call(
        functools.partial(_cmp_select_kernel, tq=tq, n_sel=n_sel, nseg=nseg, nq=nq),
        out_shape=(jax.ShapeDtypeStruct((batch * seq, NSA_HEADS * HEAD_DIM), F32),
                   jax.ShapeDtypeStruct((batch, NSA_GROUPS, seq, ns), F32),
                   jax.ShapeDtypeStruct((batch, NSA_GROUPS, nq, 8, LANES), jnp.int32)),
        grid_spec=grid_spec,
        compiler_params=_cparams("parallel", "parallel", "parallel"),
        name="nsa_cmp_select",
    )(first_chunk, pj, cmp_kv, cmp_kv, pool, tmap)


def _alibi_dead(q_sq, k_sq, dist, group_slope):
    bound = 2.0 * NORM_SLACK * jnp.sqrt(q_sq * k_sq)
    return group_slope * dist >= UNDERFLOW_BITS + bound


def _nsa_kernel(flags_ref, q_ref, ks_ref, vs_ref, kwp_ref, vwp_ref, kwd_ref, vwd_ref, sel_ref, oc_ref, gate_ref,
                o_ref, m_ref, acc_ref, *, tile, nq):
    b = pl.program_id(0)
    g = pl.program_id(1)
    i = pl.program_id(2)
    nh = NSA_GROUP_SIZE
    ns = sel_ref.shape[1]
    bpt = tile // SLC_BLOCK
    low = _lane_is_low()
    row = lax.broadcasted_iota(jnp.int32, (tile, tile), 0)
    col = lax.broadcasted_iota(jnp.int32, (tile, tile), 1)
    relpos = lax.broadcasted_iota(jnp.int32, (1, tile), 1).astype(F32)
    sel = sel_ref[...].astype(BF16)
    blk_of_key = lax.broadcasted_iota(jnp.int32, (ns, tile), 1) // SLC_BLOCK
    blk_row = lax.broadcasted_iota(jnp.int32, (ns, tile), 0)
    qh = []
    for pair in range(nh // 2):
        qh.extend(_split_pair(q_ref[:, pair * PAIR:(pair + 1) * PAIR]))
    slopes = [_group_slope(g, r) for r in range(nh)]
    _flash_reset(m_ref, acc_ref)

    def selected(j):
        expand = jnp.where(blk_row == blk_of_key + j * bpt, 1.0, 0.0).astype(BF16)
        return _dot(sel, expand) > 0.5

    def attend(slot, k_keys, v_keys, rel, keep):
        v_aug = _with_ones(v_keys)
        bias = jnp.where(keep, 0.0, NEG_INF)
        for r in range(nh):
            s = _dot_nt(qh[r], k_keys) + slopes[r] * rel + bias
            _flash_step(s, v_aug, m_ref.at[slot + r], acc_ref.at[slot + r])

    def sel_step(j, keep):
        ks = pl.multiple_of(j * tile, tile)
        attend(0, ks_ref[pl.ds(ks, tile), :], vs_ref[pl.ds(ks, tile), :],
               relpos + ((j - i) * tile).astype(F32), keep)

    sel_step(i, selected(i) & (col <= row))

    def sel_body(j, carry):
        @pl.when(flags_ref[((b * NSA_GROUPS + g) * nq + i) * nq + j] != 0)
        def _():
            sel_step(j, selected(j))
        return carry

    lax.fori_loop(0, i, sel_body, 0)

    row2 = lax.broadcasted_iota(jnp.int32, (tile, 2 * tile), 0)
    col2 = lax.broadcasted_iota(jnp.int32, (tile, 2 * tile), 1)
    in_window = ((col2 >= tile) & (col2 - tile <= row2)) | ((col2 < tile) & (col2 > row2) & (i > 0))
    rel2 = (lax.broadcasted_iota(jnp.int32, (1, 2 * tile), 1) - tile).astype(F32)
    attend(nh, jnp.concatenate([kwp_ref[...], kwd_ref[...]], axis=0),
           jnp.concatenate([vwp_ref[...], vwd_ref[...]], axis=0), rel2, in_window)

    gates = jax.nn.sigmoid(gate_ref[...])

    def gate_col(r, branch):
        c0 = FOX_HEADS + 3 * r + branch
        c1 = c0 + 3 * nh
        return jnp.where(g == 0, gates[:, c0:c0 + 1], gates[:, c1:c1 + 1])

    for pair in range(nh // 2):
        o_c = oc_ref[:, pair * PAIR:(pair + 1) * PAIR]
        outs = []
        for h in range(2):
            r = 2 * pair + h
            o_s = _flash_out(acc_ref[r])
            o_w = _flash_out(acc_ref[nh + r])
            outs.append(gate_col(r, 0) * o_c + gate_col(r, 1) * o_s + gate_col(r, 2) * o_w)
        o_ref[:, pair * PAIR:(pair + 1) * PAIR] = jnp.where(low, outs[0], outs[1]).astype(o_ref.dtype)


def _nsa_attention(flags, pj, sel, o_c, small, *, batch, seq, tile, q_blk, ks_blk, vs_blk, kw_blk, vw_blk):
    nq = seq // tile
    ns = sel.shape[3]
    gw = NSA_GROUP_SIZE * HEAD_DIM
    slots = 2 * NSA_GROUP_SIZE
    cur = lambda b, g, i, f, base: (b * nq + i, base + g)
    prev = lambda b, g, i, f, base: (b * nq + jnp.maximum(i - 1, 0), base + g)
    grid_spec = pltpu.PrefetchScalarGridSpec(
        num_scalar_prefetch=1,
        grid=(batch, NSA_GROUPS, nq),
        in_specs=[pl.BlockSpec((tile, gw), functools.partial(cur, base=q_blk)),
                  pl.BlockSpec((seq, PAIR), lambda b, g, i, f: (b, ks_blk + g)),
                  pl.BlockSpec((seq, PAIR), lambda b, g, i, f: (b, vs_blk + g)),
                  pl.BlockSpec((tile, PAIR), functools.partial(prev, base=kw_blk)),
                  pl.BlockSpec((tile, PAIR), functools.partial(prev, base=vw_blk)),
                  pl.BlockSpec((tile, PAIR), functools.partial(cur, base=kw_blk)),
                  pl.BlockSpec((tile, PAIR), functools.partial(cur, base=vw_blk)),
                  pl.BlockSpec((None, None, tile, ns), lambda b, g, i, f: (b, g, i, 0)),
                  pl.BlockSpec((tile, gw), functools.partial(cur, base=0)),
                  pl.BlockSpec((tile, small.shape[1]), lambda b, g, i, f: (b * nq + i, 0))],
        out_specs=pl.BlockSpec((tile, gw), functools.partial(cur, base=0)),
        scratch_shapes=[pltpu.VMEM((slots, tile, LANES), F32), pltpu.VMEM((slots, tile, 2 * PAIR), F32)])
    return pl.pallas_call(
        functools.partial(_nsa_kernel, tile=tile, nq=nq),
        out_shape=jax.ShapeDtypeStruct((batch * seq, NSA_HEADS * HEAD_DIM), BF16),
        grid_spec=grid_spec,
        compiler_params=_cparams("parallel", "parallel", "arbitrary"),
        name="nsa_attention",
    )(flags, pj, pj, pj, pj, pj, pj, pj, sel, o_c, small)


def _ffn_kernel(h_ref, gin_ref, gout_ref, wg_ref, wu_ref, wd_ref, *rest, chunk):
    o_ref, acc_ref = rest[-2:]
    h = h_ref[...]
    if len(rest) > 2:
        a_ref, b_ref, wa_ref, wb_ref, gmix_ref = rest[:5]
        h = h + _rms(_dot(a_ref[...], wa_ref[...]) + _dot(b_ref[...], wb_ref[...]), gmix_ref[...])
    ub = _rms(h, gin_ref[...]).astype(BF16)
    hidden = wg_ref.shape[1]
    for idx, c in enumerate(range(0, hidden, chunk)):
        gate = _dot(ub, wg_ref[:, c:c + chunk])
        up = _dot(ub, wu_ref[:, c:c + chunk])
        act = (gate * jax.nn.sigmoid(gate) * up).astype(BF16)
        part = _dot(act, wd_ref[c:c + chunk, :])
        if idx == 0:
            acc_ref[...] = part
        else:
            acc_ref[...] += part
    o_ref[...] = h + _rms(acc_ref[...], gout_ref[...])


def _ffn(h2, g_in, g_out, wg, wu, wd, *, tm, chunk, mix=()):
    n, d = h2.shape
    const = lambda i: (0, 0)
    rows = lambda i: (i, 0)
    mix_specs = []
    if mix:
        a, b, wa, wb, g_mix = mix
        mix_specs = [pl.BlockSpec((tm, a.shape[1]), rows), pl.BlockSpec((tm, b.shape[1]), rows),
                     pl.BlockSpec(wa.shape, const), pl.BlockSpec(wb.shape, const), pl.BlockSpec((1, d), const)]
    return pl.pallas_call(
        functools.partial(_ffn_kernel, chunk=chunk),
        out_shape=jax.ShapeDtypeStruct((n, d), F32),
        grid=(n // tm,),
        in_specs=[pl.BlockSpec((tm, d), rows),
                  pl.BlockSpec((1, d), const),
                  pl.BlockSpec((1, d), const),
                  pl.BlockSpec(wg.shape, const),
                  pl.BlockSpec(wu.shape, const),
                  pl.BlockSpec(wd.shape, const)] + mix_specs,
        out_specs=pl.BlockSpec((tm, d), rows),
        scratch_shapes=[pltpu.VMEM((tm, d), F32)],
        compiler_params=_cparams("parallel"),
        name="swiglu_ffn",
    )(h2, g_in, g_out, wg, wu, wd, *mix)


def _pool_kernel(h_ref, halo_ref, gin_ref, gout_ref, w_ref, sc_ref, o_ref, ext_ref, *, tm, halo, tiles_per_seq):
    i = pl.program_id(0)
    first = (i % tiles_per_seq) == 0
    h = h_ref[...]
    u = _rms(h, gin_ref[...])
    uh = _rms(halo_ref[...], gin_ref[...])
    ext_ref[0:halo, :] = jnp.where(first, 0.0, uh)
    ext_ref[halo:, :] = u
    t = (i % tiles_per_seq) * tm + lax.broadcasted_iota(jnp.int32, (tm, 1), 0)
    group = h.shape[1] // len(POOL_WINDOWS)
    ys = []
    for gi, w in enumerate(POOL_WINDOWS):
        cols = slice(gi * group, (gi + 1) * group)
        total = u[:, cols]
        for j in range(1, w):
            total = total + ext_ref[halo - j:halo - j + tm, cols]
        count = jnp.minimum(t + 1, w).astype(F32)
        pooled = total / count - u[:, cols]
        ys.append(_dot(pooled.astype(BF16), w_ref[gi]))
    y = jnp.concatenate(ys, axis=-1) * sc_ref[...]
    o_ref[...] = h + _rms(y, gout_ref[...])


def _pool_mixer(h2, g_in, g_out, w_groups, scale, *, seq, tm):
    n, d = h2.shape
    halo = max(POOL_WINDOWS)
    tiles_per_seq = seq // tm
    ratio = tm // halo
    const = lambda i: (0, 0)
    return pl.pallas_call(
        functools.partial(_pool_kernel, tm=tm, halo=halo, tiles_per_seq=tiles_per_seq),
        out_shape=jax.ShapeDtypeStruct((n, d), F32),
        grid=(n // tm,),
        in_specs=[pl.BlockSpec((tm, d), lambda i: (i, 0)),
                  pl.BlockSpec((halo, d), lambda i: (jnp.maximum(i * ratio - 1, 0), 0)),
                  pl.BlockSpec((1, d), const),
                  pl.BlockSpec((1, d), const),
                  pl.BlockSpec(w_groups.shape, lambda i: (0, 0, 0)),
                  pl.BlockSpec((1, d), const)],
        out_specs=pl.BlockSpec((tm, d), lambda i: (i, 0)),
        scratch_shapes=[pltpu.VMEM((tm + halo, d), F32)],
        compiler_params=_cparams("parallel"),
        name="pool_mixer",
    )(h2, h2, g_in, g_out, w_groups, scale)


def _pack_in_weights(w_in):
    d = w_in.shape[0]
    hw = FOX_HEADS * HEAD_DIM
    kv = NSA_GROUPS * HEAD_DIM
    sizes = (hw, hw, hw, FOX_HEADS, NSA_HEADS * HEAD_DIM, kv, kv, kv, kv, kv, kv, 3 * NSA_HEADS)
    offs = np.concatenate([[0], np.cumsum(sizes)])
    fq, fk, fv, ff, nq, kc, vc, ks, vs, kw, vw, ng = [w_in[:, offs[j]:offs[j + 1]] for j in range(len(sizes))]
    scale = HEAD_DIM ** -0.5 * LOG2E

    def dup(w):
        w = w.reshape(d, NSA_GROUPS, 1, HEAD_DIM)
        return jnp.broadcast_to(w, (d, NSA_GROUPS, 2, HEAD_DIM)).reshape(d, NSA_GROUPS * PAIR)

    w_main = jnp.concatenate([fq * scale, fk, fv, nq * scale, dup(ks), dup(vs), dup(kw), dup(vw)], axis=1)
    w_cmp = jnp.concatenate([kc, vc], axis=1)
    pad = jnp.zeros((d, LANES - FOX_HEADS - 3 * NSA_HEADS), F32)
    w_small = jnp.concatenate([ff, ng, pad], axis=1)
    w_small_hi = w_small.astype(BF16)
    w_small_lo = (w_small - w_small_hi.astype(F32)).astype(BF16)
    return w_main.astype(BF16), w_cmp.astype(BF16), jnp.concatenate([w_small_hi, w_small_lo], axis=1)


def _mixer_fox_nsa(h2, g_in, g_out, w_in, b_f, cmp_pe, cmp_w1, cmp_w2, w_out, *, batch, seq):
    tile = WINDOW
    assert seq % tile == 0 and seq // tile <= LANES
    w_main, w_cmp, w_small = _pack_in_weights(w_in)
    pj, pc, small, nrm = _inproj(h2, g_in, w_main, w_cmp, w_small, tm=tile)
    npair = FOX_HEADS // 2
    q_blk, k_blk, v_blk, nq_blk = 0, npair, 2 * npair, 3 * npair
    ks_blk = nq_blk + NSA_HEADS // 2
    vs_blk, kw_blk, vw_blk = ks_blk + NSA_GROUPS, ks_blk + 2 * NSA_GROUPS, ks_blk + 3 * NSA_GROUPS

    c = _forget_cumsum(small, b_f, batch=batch, seq=seq, tc=tile)
    nq = seq // tile
    fox_tq, fox_tk = min(FOX_TQ, seq), min(FOX_TK, seq)
    c_tiles = c.reshape(batch, npair, 2, seq // fox_tk, fox_tk).transpose(0, 1, 3, 2, 4)
    live = _fox_live_tiles(c, nrm, batch=batch, seq=seq, tq=fox_tq, tk=fox_tk, tm=tile)
    o_fox = _fox_attention(live, pj, c_tiles, batch=batch, seq=seq, tq=fox_tq, tk=fox_tk,
                           q_blk=q_blk, k_blk=k_blk, v_blk=v_blk)

    nch = seq // CMP_STRIDE
    ns = seq // SLC_BLOCK
    n_sel = min(N_SELECT, ns)
    kv = NSA_GROUPS * HEAD_DIM
    tch = pc.reshape(batch, nch, CMP_STRIDE, 2, NSA_GROUPS, HEAD_DIM)
    tch = tch.transpose(0, 3, 4, 1, 2, 5).reshape(batch, 2, NSA_GROUPS, nch, CMP_STRIDE * HEAD_DIM)
    pe8 = jnp.broadcast_to(cmp_pe.reshape(2, 1, CMP_BLOCK * HEAD_DIM), (2, 8, CMP_BLOCK * HEAD_DIM)).astype(BF16)
    w2dup = jnp.concatenate([cmp_w2, cmp_w2], axis=-1).astype(BF16)
    cmp_kv, cmp_nrm = _compress(tch, pe8, cmp_w1.astype(BF16), w2dup)

    group_slope = jnp.asarray([min(ALIBI_SLOPES_LOG2[g * NSA_GROUP_SIZE:(g + 1) * NSA_GROUP_SIZE])
                               for g in range(NSA_GROUPS)], F32)
    qn = nrm[:, NRM_NSA_Q, :NSA_HEADS].reshape(batch, nq, NSA_GROUPS, NSA_GROUP_SIZE).max(axis=3)
    ksn = nrm[:, NRM_NSA_KS, 0:2 * NSA_GROUPS:2].reshape(batch, nq, NSA_GROUPS).max(axis=1)
    kcn = cmp_nrm[:, 0, :, 0, 0]
    tiles = jnp.arange(nq)
    nseg = CMP_SEGMENTS if nq % CMP_SEGMENTS == 0 else 1
    chunk = nch // nseg
    last_end = (jnp.arange(nseg) + 1) * chunk * CMP_STRIDE + (CMP_BLOCK - 1 - CMP_STRIDE)
    dist_c = (tiles[:, None] * tile - last_end[None, :] - (CMP_STRIDE - 1)).astype(F32)
    dead_c = _alibi_dead(qn[..., None], kcn[:, None, :, None], dist_c[None, :, None, :],
                         group_slope[None, None, :, None])
    first_chunk = jnp.minimum(dead_c.sum(axis=3), (tiles * nseg // nq)[None, :, None])
    first_chunk = first_chunk.transpose(0, 2, 1).astype(jnp.int32).reshape(-1)
    dist_s = ((tiles[:, None] - tiles[None, :] - 1) * tile + 1).astype(F32)
    dead_s = _alibi_dead(qn[..., None], ksn[:, None, :, None], dist_s[None, :, None, :],
                         group_slope[None, None, :, None])
    live_s = jnp.logical_not(dead_s).transpose(0, 2, 1, 3).astype(jnp.int32)

    ratio = SLC_BLOCK // CMP_STRIDE
    r = CMP_BLOCK // CMP_STRIDE
    n_idx = np.arange(nch)[:, None]
    b_idx = np.arange(ns)[None, :]
    pool_np = ((n_idx >= ratio * b_idx - (r - 1)) & (n_idx <= ratio * b_idx + ratio - 1)
               & (n_idx < nch - r + 1)).astype(np.float32)
    pool = jnp.asarray(pool_np, dtype=BF16)
    tmap_np = (np.arange(ns)[:, None] // (tile // SLC_BLOCK) == np.arange(LANES)[None, :]).astype(np.float32)
    o_c, sel, flag_blocks = _cmp_select(first_chunk, pj, cmp_kv, pool, jnp.asarray(tmap_np, dtype=BF16),
                                        batch=batch, seq=seq, tq=tile, q_blk=nq_blk // 2, n_sel=n_sel, nseg=nseg)
    flags = (flag_blocks[:, :, :, 0, :nq] * live_s).reshape(-1)
    o_nsa = _nsa_attention(flags, pj, sel, o_c, small, batch=batch, seq=seq, tile=tile,
                           q_blk=nq_blk // 2, ks_blk=ks_blk, vs_blk=vs_blk, kw_blk=kw_blk, vw_blk=vw_blk)

    hw = FOX_HEADS * HEAD_DIM
    return o_fox, o_nsa, w_out[:hw].astype(BF16), w_out[hw:].astype(BF16), g_out


def kernel(x, norm_g, attn_w_in, fox_b_f, nsa_cmp_pe, nsa_cmp_w1, nsa_cmp_w2, attn_w_out,
           pool_w, pool_scale, ffn_w_gate, ffn_w_up, ffn_w_down):
    batch, seq, d = x.shape
    depth = norm_g.shape[0]
    tm = 512 if seq % 512 == 0 else seq
    h = x.reshape(batch * seq, d)
    for layer in range(depth):
        g = norm_g[layer].reshape(4, 1, d)
        i = layer // 2
        mix = ()
        if layer % 2 == 0:
            mix = _mixer_fox_nsa(h, g[0], g[1], attn_w_in[i], fox_b_f[i], nsa_cmp_pe[i], nsa_cmp_w1[i],
                                 nsa_cmp_w2[i], attn_w_out[i], batch=batch, seq=seq)
        else:
            h = _pool_mixer(h, g[0], g[1], pool_w[i].astype(BF16), pool_scale[i].reshape(1, d), seq=seq, tm=tm)
        h = _ffn(h, g[2], g[3], ffn_w_gate[layer].astype(BF16), ffn_w_up[layer].astype(BF16),
                 ffn_w_down[layer].astype(BF16), tm=tm, chunk=256, mix=mix)
    return h.reshape(batch, seq, d)
```

```python
import functools

import numpy as np
import jax
import jax.numpy as jnp
from jax import lax
from jax.experimental import pallas as pl
from jax.experimental.pallas import tpu as pltpu

F32 = jnp.float32
BF16 = jnp.bfloat16

HEAD_DIM = 64
FOX_HEADS = 8
NSA_HEADS = 8
NSA_GROUPS = 2
NSA_GROUP_SIZE = NSA_HEADS // NSA_GROUPS
CMP_BLOCK = 32
CMP_STRIDE = 16
SLC_BLOCK = 64
N_SELECT = 16
WINDOW = 512
POOL_WINDOWS = (2, 4, 8, 16)
POOL_HALO = max(POOL_WINDOWS)
RMS_EPS = 1e-6
NEG_INF = -1e30
BIG = 1e30
REMOVED = -3e38
LANES = 128
PAIR = 2 * HEAD_DIM
VMEM_LIMIT = 56 * 1024 * 1024
CMP_SEGMENTS = 4
UNDERFLOW_BITS = 160.0
NORM_SLACK = 1.02
NORM_CHUNKS = (0, 1, 3, 4)
NRM_FOX_Q, NRM_FOX_K, NRM_NSA_Q, NRM_NSA_KS = range(4)
FOX_TQ, FOX_TK = 1024, 1024

LOG2E = 1.4426950408889634
ALIBI_SLOPES_LOG2 = tuple(float(2.0 ** (-8.0 * (i + 1.0) / NSA_HEADS)) * LOG2E for i in range(NSA_HEADS))


def _cparams(*sem):
    return pltpu.CompilerParams(dimension_semantics=sem, vmem_limit_bytes=VMEM_LIMIT)


def _rms(x, g):
    return x * lax.rsqrt(jnp.mean(x * x, axis=-1, keepdims=True) + RMS_EPS) * g


def _dot(a, b):
    return jnp.dot(a, b, preferred_element_type=F32)


def _dot_nt(a, b):
    return lax.dot_general(a, b, (((1,), (1,)), ((), ())), preferred_element_type=F32)


def _inproj_kernel(x_ref, g_ref, w_ref, wc_ref, ws_ref, hmap_ref, o_ref, oc_ref, os_ref, nrm_ref, *, fox_width):
    u = _rms(x_ref[...], g_ref[...])
    ub = u.astype(BF16)
    norms = []
    for c in range(0, o_ref.shape[1], fox_width):
        r = _dot(ub, w_ref[:, c:c + fox_width]).astype(BF16)
        o_ref[:, c:c + fox_width] = r
        if c // fox_width in NORM_CHUNKS:
            rf = r.astype(F32)
            sq = _dot((rf * rf).astype(BF16), hmap_ref[...])
            norms.append(jnp.max(sq, axis=0, keepdims=True))
    oc_ref[...] = _dot(ub, wc_ref[...]).astype(BF16)
    width = os_ref.shape[1]
    u_lo = (u - ub.astype(F32)).astype(BF16)
    both = _dot(ub, ws_ref[...])
    os_ref[...] = both[:, :width] + both[:, width:] + _dot(u_lo, ws_ref[:, :width])
    row = lax.broadcasted_iota(jnp.int32, nrm_ref.shape, 0)
    out = jnp.zeros(nrm_ref.shape, F32)
    for k, nk in enumerate(norms):
        out = jnp.where(row == k, nk, out)
    nrm_ref[...] = out


def _inproj(x2, g, w_main, w_cmp, w_small, *, tm):
    n, d = x2.shape
    cm, cc, cs = w_main.shape[1], w_cmp.shape[1], w_small.shape[1] // 2
    fox_width = FOX_HEADS * HEAD_DIM
    hmap = jnp.asarray((np.arange(fox_width)[:, None] // HEAD_DIM == np.arange(LANES)[None, :]).astype(np.float32),
                       dtype=BF16)
    const = lambda i: (0, 0)
    return pl.pallas_call(
        functools.partial(_inproj_kernel, fox_width=fox_width),
        out_shape=(jax.ShapeDtypeStruct((n, cm), BF16),
                   jax.ShapeDtypeStruct((n, cc), BF16),
                   jax.ShapeDtypeStruct((n, cs), F32),
                   jax.ShapeDtypeStruct((n // tm, 8, LANES), F32)),
        grid=(n // tm,),
        in_specs=[pl.BlockSpec((tm, d), lambda i: (i, 0)),
                  pl.BlockSpec((1, d), const),
                  pl.BlockSpec((d, cm), const),
                  pl.BlockSpec((d, cc), const),
                  pl.BlockSpec((d, 2 * cs), const),
                  pl.BlockSpec((fox_width, LANES), const)],
        out_specs=(pl.BlockSpec((tm, cm), lambda i: (i, 0)),
                   pl.BlockSpec((tm, cc), lambda i: (i, 0)),
                   pl.BlockSpec((tm, cs), lambda i: (i, 0)),
                   pl.BlockSpec((None, 8, LANES), lambda i: (i, 0, 0))),
        compiler_params=_cparams("parallel"),
        name="inproj",
    )(x2, g, w_main, w_cmp, w_small, hmap)


def _forget_cumsum_kernel(sm_ref, bf_ref, tri_ref, o_ref, carry_ref):
    @pl.when(pl.program_id(1) == 0)
    def _():
        carry_ref[...] = jnp.zeros_like(carry_ref)

    zt = sm_ref[...].T
    z = zt[0:FOX_HEADS, :] + bf_ref[...]
    log_f = jnp.minimum(z, 0.0) - jnp.log1p(jnp.exp(-jnp.abs(z)))
    c = jnp.dot(log_f, tri_ref[...], preferred_element_type=F32,
                precision=lax.Precision.HIGHEST) + carry_ref[...]
    o_ref[...] = c * LOG2E
    carry_ref[...] = c[:, c.shape[1] - 1:]


def _forget_cumsum(small, b_f, *, batch, seq, tc):
    tri = jnp.asarray(np.triu(np.ones((tc, tc), np.float32)))
    nchunk = seq // tc
    return pl.pallas_call(
        _forget_cumsum_kernel,
        out_shape=jax.ShapeDtypeStruct((batch, FOX_HEADS, seq), F32),
        grid=(batch, nchunk),
        in_specs=[pl.BlockSpec((tc, small.shape[1]), lambda b, j: (b * nchunk + j, 0)),
                  pl.BlockSpec((FOX_HEADS, 1), lambda b, j: (0, 0)),
                  pl.BlockSpec((tc, tc), lambda b, j: (0, 0))],
        out_specs=pl.BlockSpec((None, FOX_HEADS, tc), lambda b, j: (b, 0, j)),
        scratch_shapes=[pltpu.VMEM((FOX_HEADS, 1), F32)],
        compiler_params=_cparams("parallel", "arbitrary"),
        name="forget_cumsum",
    )(small, b_f.reshape(FOX_HEADS, 1), tri)


def _lane_is_low():
    return lax.broadcasted_iota(jnp.int32, (1, PAIR), 1) < HEAD_DIM


def _with_ones(v_tile):
    return jnp.concatenate([v_tile, jnp.ones_like(v_tile)], axis=1)


def _flash_step(s, v, m_ref, acc_ref, l_ref=None):
    m_old = m_ref[...]
    m_new = jnp.maximum(m_old, jnp.max(s, axis=-1, keepdims=True))
    alpha = jnp.exp2(m_old - m_new)
    p = jnp.exp2(s - jnp.concatenate([m_new] * (s.shape[1] // LANES), axis=1))
    if l_ref is None:
        alpha_acc = jnp.concatenate([alpha, alpha], axis=1)
    else:
        alpha_acc = alpha
        l_ref[...] = alpha * l_ref[...] + jnp.sum(p, axis=-1, keepdims=True)
    acc_ref[...] = alpha_acc * acc_ref[...] + _dot(p.astype(BF16), v)
    m_ref[...] = m_new


def _flash_reset(m_ref, *sum_refs):
    m_ref[...] = jnp.full_like(m_ref, NEG_INF)
    for ref in sum_refs:
        ref[...] = jnp.zeros_like(ref)


def _flash_out(acc):
    return acc[:, :PAIR] / acc[:, PAIR:]


def _split_pair(q):
    low = _lane_is_low()
    zero = jnp.zeros_like(q)
    return jnp.where(low, q, zero), jnp.where(low, zero, q)


def _fox_kernel(live_ref, q_ref, k_ref, v_ref, c_ref, o_ref, m_ref, l_ref, acc_ref, *, tq, tk, nq):
    b = pl.program_id(0)
    pair = pl.program_id(1)
    i = pl.program_id(2)
    nk = c_ref.shape[0]
    qh = _split_pair(q_ref[...])
    _flash_reset(m_ref, l_ref, acc_ref)

    def head_step(h, j, mask):
        ks = pl.multiple_of(j * tk, tk)
        s = _dot_nt(qh[h], k_ref[pl.ds(ks, tk), :]) - c_ref[j, h:h + 1, :]
        if mask is not None:
            s = jnp.where(mask, s, NEG_INF)
        _flash_step(s, v_ref[pl.ds(ks, tk), :], m_ref.at[h], acc_ref.at[h], l_ref.at[h])

    def body(j, carry):
        live = [live_ref[(((b * FOX_HEADS) + 2 * pair + h) * nq + i) * nk + j] != 0 for h in range(2)]

        @pl.when(live[0] & live[1])
        def _():
            head_step(0, j, None)
            head_step(1, j, None)

        for h in range(2):
            @pl.when(live[h] & jnp.logical_not(live[1 - h]))
            def _(h=h):
                head_step(h, j, None)
        return carry

    lax.fori_loop(0, i, body, 0)

    def diag_part(h, r0, nr, nkeys):
        ks = pl.multiple_of(i * tq, tq)
        rows = pl.ds(r0, nr)
        row = lax.broadcasted_iota(jnp.int32, (nr, nkeys), 0) + r0
        col = lax.broadcasted_iota(jnp.int32, (nr, nkeys), 1)
        s = _dot_nt(qh[h][r0:r0 + nr], k_ref[pl.ds(ks, nkeys), :]) - c_ref[i, h:h + 1, 0:nkeys]
        _flash_step(jnp.where(col <= row, s, NEG_INF), v_ref[pl.ds(ks, nkeys), :],
                    m_ref.at[h, rows], acc_ref.at[h, rows], l_ref.at[h, rows])

    half = tq // 2
    for h in range(2):
        diag_part(h, 0, half, half)
        diag_part(h, half, half, tq)
    o_ref[...] = jnp.where(_lane_is_low(), acc_ref[0] / l_ref[0], acc_ref[1] / l_ref[1]).astype(o_ref.dtype)


def _fox_live_tiles(c2, nrm, *, batch, seq, tq, tk, tm):
    nq, nk = seq // tq, seq // tk
    qn = nrm[:, 0, :FOX_HEADS].reshape(batch, nq, tq // tm, FOX_HEADS).max(axis=2)
    kn = nrm[:, 1, :FOX_HEADS].reshape(batch, seq // tm, FOX_HEADS).max(axis=1)
    bound = 2.0 * NORM_SLACK * jnp.sqrt(qn * kn[:, None, :])
    c_start = c2[:, :, ::tq]
    c_end = c2[:, :, tk - 1::tk]
    decay = c_end[:, :, None, :] - c_start[:, :, :, None]
    live = decay < UNDERFLOW_BITS + jnp.transpose(bound, (0, 2, 1))[..., None]
    return live.astype(jnp.int32).reshape(-1)


def _fox_attention(live, pj, c_tiles, *, batch, seq, tq, tk, q_blk, k_blk, v_blk):
    assert tq == tk and tq % (2 * LANES) == 0
    nq = seq // tq
    nk = seq // tk
    npair = FOX_HEADS // 2
    grid_spec = pltpu.PrefetchScalarGridSpec(
        num_scalar_prefetch=1,
        grid=(batch, npair, nq),
        in_specs=[pl.BlockSpec((tq, PAIR), lambda b, p, i, f: (b * nq + i, q_blk + p)),
                  pl.BlockSpec((seq, PAIR), lambda b, p, i, f: (b, k_blk + p)),
                  pl.BlockSpec((seq, PAIR), lambda b, p, i, f: (b, v_blk + p)),
                  pl.BlockSpec((None, None, nk, 2, tk), lambda b, p, i, f: (b, p, 0, 0, 0))],
        out_specs=pl.BlockSpec((tq, PAIR), lambda b, p, i, f: (b * nq + i, p)),
        scratch_shapes=[pltpu.VMEM((2, tq, LANES), F32), pltpu.VMEM((2, tq, LANES), F32),
                        pltpu.VMEM((2, tq, PAIR), F32)])
    return pl.pallas_call(
        functools.partial(_fox_kernel, tq=tq, tk=tk, nq=nq),
        out_shape=jax.ShapeDtypeStruct((batch * seq, FOX_HEADS * HEAD_DIM), BF16),
        grid_spec=grid_spec,
        compiler_params=_cparams("parallel", "parallel", "arbitrary"),
        name="fox_attention",
    )(live, pj, pj, pj, c_tiles)


def _gelu_tanh(x):
    return 0.5 * x * (1.0 + jnp.tanh(0.7978845608028654 * (x + 0.044715 * x * x * x)))


def _compress_kernel(t_ref, pe_ref, w1_ref, w2_ref, o_ref, nrm_ref):
    half = w1_ref.shape[0] // 2
    t = t_ref[...]
    a = _dot(t, w1_ref[0:half, :])
    b = _dot(t, w1_ref[half:, :])
    pe_term = _dot(pe_ref[...], w1_ref[...])[0:1, :]
    nch = t.shape[0]
    h = a + pltpu.roll(b, nch - 1, axis=0) + pe_term
    o = _dot(_gelu_tanh(h).astype(BF16), w2_ref[...]).astype(o_ref.dtype)
    o_ref[...] = o
    of = o.astype(F32)
    sq = 0.5 * jnp.sum(of * of, axis=-1, keepdims=True)
    nrm_ref[...] = jnp.broadcast_to(jnp.max(sq, axis=0, keepdims=True), nrm_ref.shape)


def _compress(tchunks, pe8, w1, w2dup):
    b, two, g, nch, width = tchunks.shape
    tile5 = lambda bb, kv, gg: (bb, kv, gg, 0, 0)
    return pl.pallas_call(
        _compress_kernel,
        out_shape=(jax.ShapeDtypeStruct((b, two, g, nch, PAIR), BF16),
                   jax.ShapeDtypeStruct((b, two, g, 8, LANES), F32)),
        grid=(b, two, g),
        in_specs=[pl.BlockSpec((None, None, None, nch, width), tile5),
                  pl.BlockSpec((None, 8, 2 * width), lambda bb, kv, gg: (kv, 0, 0)),
                  pl.BlockSpec((None, 2 * width, HEAD_DIM), lambda bb, kv, gg: (kv, 0, 0)),
                  pl.BlockSpec((None, HEAD_DIM, PAIR), lambda bb, kv, gg: (kv, 0, 0))],
        out_specs=(pl.BlockSpec((None, None, None, nch, PAIR), tile5),
                   pl.BlockSpec((None, None, None, 8, LANES), tile5)),
        compiler_params=_cparams("parallel", "parallel", "parallel"),
        name="nsa_compress",
    )(tchunks, pe8, w1, w2dup)


def _group_slope(g, r):
    return jnp.where(g == 0, ALIBI_SLOPES_LOG2[r], ALIBI_SLOPES_LOG2[NSA_GROUP_SIZE + r]).astype(F32)


def _split3(x):
    hi = x.astype(BF16)
    r1 = x - hi.astype(F32)
    mid = r1.astype(BF16)
    lo = (r1 - mid.astype(F32)).astype(BF16)
    return hi, mid, lo


def _cmp_select_kernel(first_ref, q_ref, kc_ref, vc_ref, pool_ref, tmap_ref, oc_ref, sel_ref, flag_ref, *,
                       tq, n_sel, nseg, nq):
    b = pl.program_id(0)
    g = pl.program_id(1)
    i = pl.program_id(2)
    chunk = kc_ref.shape[0] // nseg
    first = first_ref[(b * NSA_GROUPS + g) * nq + i]
    n_live = (i * nseg) // nq + 1 - first
    for k in range(1, nseg + 1):
        @pl.when(n_live == k)
        def _(k=k):
            _cmp_select_body(q_ref, kc_ref, vc_ref, pool_ref, tmap_ref, oc_ref, sel_ref, flag_ref,
                             tq=tq, n_sel=n_sel, start=pl.multiple_of(first * chunk, chunk), nch=k * chunk)


def _cmp_select_body(q_ref, kc_ref, vc_ref, pool_ref, tmap_ref, oc_ref, sel_ref, flag_ref, *,
                     tq, n_sel, start, nch):
    g = pl.program_id(1)
    t0 = pl.program_id(2) * tq
    ns = pool_ref.shape[1]
    low = _lane_is_low()
    t = t0 + lax.broadcasted_iota(jnp.int32, (tq, 1), 0)
    cmp_end = (start + lax.broadcasted_iota(jnp.int32, (1, nch), 1)) * CMP_STRIDE + (CMP_BLOCK - 1)
    rel_end = (cmp_end - t0).astype(F32)
    mask_bias = jnp.where(t >= cmp_end, 0.0, NEG_INF)
    row_valid = jnp.where(t >= CMP_BLOCK - 1, 1.0, 0.0)
    kc = kc_ref[pl.ds(start, nch), :]
    vc = vc_ref[pl.ds(start, nch), :]
    imp = jnp.zeros((tq, nch), F32)
    for pair in range(NSA_GROUP_SIZE // 2):
        qh = _split_pair(q_ref[:, pair * PAIR:(pair + 1) * PAIR])
        outs = []
        for h in range(2):
            s = _dot_nt(qh[h], kc) + _group_slope(g, 2 * pair + h) * rel_end + mask_bias
            e = jnp.exp2(s - jnp.max(s, axis=-1, keepdims=True))
            p = e * (row_valid / jnp.maximum(jnp.sum(e, axis=-1, keepdims=True), 1e-30))
            outs.append(_dot(p.astype(BF16), vc))
            imp = imp + p
        oc_ref[:, pair * PAIR:(pair + 1) * PAIR] = jnp.where(low, outs[0], outs[1])

    pool = pool_ref[pl.ds(start, nch), :]
    hi, mid, lo = _split3(imp)
    p_slc = _dot(hi, pool) + _dot(mid, pool) + _dot(lo, pool)

    blk = lax.broadcasted_iota(jnp.int32, (1, ns), 1)
    cur = t // SLC_BLOCK
    forced = (blk == 0) | (blk == cur) | (blk == cur - 1)
    val = jnp.where(forced, REMOVED, jnp.where(blk > cur, NEG_INF, p_slc))
    sel = jnp.where(forced, 1.0, 0.0)
    blk_f = blk.astype(F32)
    for _ in range(n_sel - 3):
        mx = jnp.max(val, axis=-1, keepdims=True)
        pick = jnp.min(jnp.where(val == mx, blk_f, float(ns)), axis=-1, keepdims=True)
        hit = blk_f == pick
        sel = jnp.where(hit, 1.0, sel)
        val = jnp.where(hit, REMOVED, val)
    sel_ref[...] = sel
    any_sel = jnp.broadcast_to(jnp.max(sel, axis=0, keepdims=True), (8, ns)).astype(BF16)
    flag_ref[...] = (_dot(any_sel, tmap_ref[...]) > 0.5).astype(jnp.int32)


def _cmp_select(first_chunk, pj, cmp_kv, pool, tmap, *, batch, seq, tq, q_blk, n_sel, nseg):
    nq = seq // tq
    nch = cmp_kv.shape[3]
    ns = pool.shape[1]
    gw = NSA_GROUP_SIZE * HEAD_DIM
    grid_spec = pltpu.PrefetchScalarGridSpec(
        num_scalar_prefetch=1,
        grid=(batch, NSA_GROUPS, nq),
        in_specs=[pl.BlockSpec((tq, gw), lambda b, g, i, f: (b * nq + i, q_blk + g)),
                  pl.BlockSpec((None, None, None, nch, PAIR), lambda b, g, i, f: (b, 0, g, 0, 0)),
                  pl.BlockSpec((None, None, None, nch, PAIR), lambda b, g, i, f: (b, 1, g, 0, 0)),
                  pl.BlockSpec((nch, ns), lambda b, g, i, f: (0, 0)),
                  pl.BlockSpec((ns, LANES), lambda b, g, i, f: (0, 0))],
        out_specs=(pl.BlockSpec((tq, gw), lambda b, g, i, f: (b * nq + i, g)),
                   pl.BlockSpec((None, None, tq, ns), lambda b, g, i, f: (b, g, i, 0)),
                   pl.BlockSpec((None, None, None, 8, LANES), lambda b, g, i, f: (b, g, i, 0, 0))))
    return pl.pallas_call(
        functools.partial(_cmp_select_kernel, tq=tq, n_sel=n_sel, nseg=nseg, nq=nq),
        out_shape=(jax.ShapeDtypeStruct((batch * seq, NSA_HEADS * HEAD_DIM), F32),
                   jax.ShapeDtypeStruct((batch, NSA_GROUPS, seq, ns), F32),
                   jax.ShapeDtypeStruct((batch, NSA_GROUPS, nq, 8, LANES), jnp.int32)),
        grid_spec=grid_spec,
        compiler_params=_cparams("parallel", "parallel", "parallel"),
        name="nsa_cmp_select",
    )(first_chunk, pj, cmp_kv, cmp_kv, pool, tmap)


def _alibi_dead(q_sq, k_sq, dist, group_slope):
    bound = 2.0 * NORM_SLACK * jnp.sqrt(q_sq * k_sq)
    return group_slope * dist >= UNDERFLOW_BITS + bound


def _nsa_kernel(flags_ref, q_ref, ks_ref, vs_ref, kwp_ref, vwp_ref, kwd_ref, vwd_ref, sel_ref, oc_ref, gate_ref,
                o_ref, m_ref, acc_ref, *, tile, nq):
    b = pl.program_id(0)
    g = pl.program_id(1)
    i = pl.program_id(2)
    nh = NSA_GROUP_SIZE
    ns = sel_ref.shape[1]
    bpt = tile // SLC_BLOCK
    low = _lane_is_low()
    row = lax.broadcasted_iota(jnp.int32, (tile, tile), 0)
    col = lax.broadcasted_iota(jnp.int32, (tile, tile), 1)
    relpos = lax.broadcasted_iota(jnp.int32, (1, tile), 1).astype(F32)
    sel = sel_ref[...].astype(BF16)
    blk_of_key = lax.broadcasted_iota(jnp.int32, (ns, tile), 1) // SLC_BLOCK
    blk_row = lax.broadcasted_iota(jnp.int32, (ns, tile), 0)
    qh = []
    for pair in range(nh // 2):
        qh.extend(_split_pair(q_ref[:, pair * PAIR:(pair + 1) * PAIR]))
    slopes = [_group_slope(g, r) for r in range(nh)]
    _flash_reset(m_ref, acc_ref)

    def selected(j):
        expand = jnp.where(blk_row == blk_of_key + j * bpt, 1.0, 0.0).astype(BF16)
        return _dot(sel, expand) > 0.5

    def attend(slot, k_keys, v_keys, rel, keep):
        v_aug = _with_ones(v_keys)
        bias = jnp.where(keep, 0.0, NEG_INF)
        for r in range(nh):
            s = _dot_nt(qh[r], k_keys) + slopes[r] * rel + bias
            _flash_step(s, v_aug, m_ref.at[slot + r], acc_ref.at[slot + r])

    def sel_step(j, keep):
        ks = pl.multiple_of(j * tile, tile)
        attend(0, ks_ref[pl.ds(ks, tile), :], vs_ref[pl.ds(ks, tile), :],
               relpos + ((j - i) * tile).astype(F32), keep)

    sel_step(i, selected(i) & (col <= row))

    def sel_body(j, carry):
        @pl.when(flags_ref[((b * NSA_GROUPS + g) * nq + i) * nq + j] != 0)
        def _():
            sel_step(j, selected(j))
        return carry

    lax.fori_loop(0, i, sel_body, 0)

    row2 = lax.broadcasted_iota(jnp.int32, (tile, 2 * tile), 0)
    col2 = lax.broadcasted_iota(jnp.int32, (tile, 2 * tile), 1)
    in_window = ((col2 >= tile) & (col2 - tile <= row2)) | ((col2 < tile) & (col2 > row2) & (i > 0))
    rel2 = (lax.broadcasted_iota(jnp.int32, (1, 2 * tile), 1) - tile).astype(F32)
    attend(nh, jnp.concatenate([kwp_ref[...], kwd_ref[...]], axis=0),
           jnp.concatenate([vwp_ref[...], vwd_ref[...]], axis=0), rel2, in_window)

    gates = jax.nn.sigmoid(gate_ref[...])

    def gate_col(r, branch):
        c0 = FOX_HEADS + 3 * r + branch
        c1 = c0 + 3 * nh
        return jnp.where(g == 0, gates[:, c0:c0 + 1], gates[:, c1:c1 + 1])

    for pair in range(nh // 2):
        o_c = oc_ref[:, pair * PAIR:(pair + 1) * PAIR]
        outs = []
        for h in range(2):
            r = 2 * pair + h
            o_s = _flash_out(acc_ref[r])
            o_w = _flash_out(acc_ref[nh + r])
            outs.append(gate_col(r, 0) * o_c + gate_col(r, 1) * o_s + gate_col(r, 2) * o_w)
        o_ref[:, pair * PAIR:(pair + 1) * PAIR] = jnp.where(low, outs[0], outs[1]).astype(o_ref.dtype)


def _nsa_attention(flags, pj, sel, o_c, small, *, batch, seq, tile, q_blk, ks_blk, vs_blk, kw_blk, vw_blk):
    nq = seq // tile
    ns = sel.shape[3]
    gw = NSA_GROUP_SIZE * HEAD_DIM
    slots = 2 * NSA_GROUP_SIZE
    cur = lambda b, g, i, f, base: (b * nq + i, base + g)
    prev = lambda b, g, i, f, base: (b * nq + jnp.maximum(i - 1, 0), base + g)
    grid_spec = pltpu.PrefetchScalarGridSpec(
        num_scalar_prefetch=1,
        grid=(batch, NSA_GROUPS, nq),
        in_specs=[pl.BlockSpec((tile, gw), functools.partial(cur, base=q_blk)),
                  pl.BlockSpec((seq, PAIR), lambda b, g, i, f: (b, ks_blk + g)),
                  pl.BlockSpec((seq, PAIR), lambda b, g, i, f: (b, vs_blk + g)),
                  pl.BlockSpec((tile, PAIR), functools.partial(prev, base=kw_blk)),
                  pl.BlockSpec((tile, PAIR), functools.partial(prev, base=vw_blk)),
                  pl.BlockSpec((tile, PAIR), functools.partial(cur, base=kw_blk)),
                  pl.BlockSpec((tile, PAIR), functools.partial(cur, base=vw_blk)),
                  pl.BlockSpec((None, None, tile, ns), lambda b, g, i, f: (b, g, i, 0)),
                  pl.BlockSpec((tile, gw), functools.partial(cur, base=0)),
                  pl.BlockSpec((tile, small.shape[1]), lambda b, g, i, f: (b * nq + i, 0))],
        out_specs=pl.BlockSpec((tile, gw), functools.partial(cur, base=0)),
        scratch_shapes=[pltpu.VMEM((slots, tile, LANES), F32), pltpu.VMEM((slots, tile, 2 * PAIR), F32)])
    return pl.pallas_call(
        functools.partial(_nsa_kernel, tile=tile, nq=nq),
        out_shape=jax.ShapeDtypeStruct((batch * seq, NSA_HEADS * HEAD_DIM), BF16),
        grid_spec=grid_spec,
        compiler_params=_cparams("parallel", "parallel", "arbitrary"),
        name="nsa_attention",
    )(flags, pj, pj, pj, pj, pj, pj, pj, sel, o_c, small)


def _ffn_kernel(h_ref, gin_ref, gout_ref, wg_ref, wu_ref, wd_ref, *rest, chunk, mixer, tm, tiles_per_seq):
    h = h_ref[...]
    if mixer == "attn":
        a_ref, b_ref, wa_ref, wb_ref, gmix_ref, o_ref, acc_ref = rest
        h = h + _rms(_dot(a_ref[...], wa_ref[...]) + _dot(b_ref[...], wb_ref[...]), gmix_ref[...])
    else:
        halo_ref, pin_ref, pout_ref, pw_ref, sc_ref, o_ref, acc_ref, ext_ref = rest
        h = _pool_mix(h, halo_ref, pin_ref, pout_ref, pw_ref, sc_ref, ext_ref, tm=tm, tiles_per_seq=tiles_per_seq)
    ub = _rms(h, gin_ref[...]).astype(BF16)
    hidden = wg_ref.shape[1]
    for idx, c in enumerate(range(0, hidden, chunk)):
        gate = _dot(ub, wg_ref[:, c:c + chunk])
        up = _dot(ub, wu_ref[:, c:c + chunk])
        act = (gate * jax.nn.sigmoid(gate) * up).astype(BF16)
        part = _dot(act, wd_ref[c:c + chunk, :])
        if idx == 0:
            acc_ref[...] = part
        else:
            acc_ref[...] += part
    o_ref[...] = h + _rms(acc_ref[...], gout_ref[...])


def _layer_tail(h2, g_in, g_out, wg, wu, wd, *, seq, tm, chunk, mixer, mix):
    n, d = h2.shape
    const = lambda i: (0, 0)
    rows = lambda i: (i, 0)
    scratch = [pltpu.VMEM((tm, d), F32)]
    if mixer == "attn":
        a, b, wa, wb, g_mix = mix
        mix_specs = [pl.BlockSpec((tm, a.shape[1]), rows), pl.BlockSpec((tm, b.shape[1]), rows),
                     pl.BlockSpec(wa.shape, const), pl.BlockSpec(wb.shape, const), pl.BlockSpec((1, d), const)]
    else:
        p_in, p_out, w_groups, scale = mix
        ratio = tm // POOL_HALO
        mix = (h2, p_in, p_out, w_groups, scale)
        mix_specs = [pl.BlockSpec((POOL_HALO, d), lambda i: (jnp.maximum(i * ratio - 1, 0), 0)),
                     pl.BlockSpec((1, d), const), pl.BlockSpec((1, d), const),
                     pl.BlockSpec(w_groups.shape, lambda i: (0, 0, 0)), pl.BlockSpec((1, d), const)]
        scratch.append(pltpu.VMEM((tm + POOL_HALO, d), F32))
    return pl.pallas_call(
        functools.partial(_ffn_kernel, chunk=chunk, mixer=mixer, tm=tm, tiles_per_seq=seq // tm),
        out_shape=jax.ShapeDtypeStruct((n, d), F32),
        grid=(n // tm,),
        in_specs=[pl.BlockSpec((tm, d), rows),
                  pl.BlockSpec((1, d), const),
                  pl.BlockSpec((1, d), const),
                  pl.BlockSpec(wg.shape, const),
                  pl.BlockSpec(wu.shape, const),
                  pl.BlockSpec(wd.shape, const)] + mix_specs,
        out_specs=pl.BlockSpec((tm, d), rows),
        scratch_shapes=scratch,
        compiler_params=_cparams("parallel"),
        name=f"{mixer}_tail_swiglu_ffn",
    )(h2, g_in, g_out, wg, wu, wd, *mix)


def _pool_mix(h, halo_ref, gin_ref, gout_ref, w_ref, sc_ref, ext_ref, *, tm, tiles_per_seq):
    halo = POOL_HALO
    i = pl.program_id(0)
    first = (i % tiles_per_seq) == 0
    u = _rms(h, gin_ref[...])
    uh = _rms(halo_ref[...], gin_ref[...])
    ext_ref[0:halo, :] = jnp.where(first, 0.0, uh)
    ext_ref[halo:, :] = u
    t = (i % tiles_per_seq) * tm + lax.broadcasted_iota(jnp.int32, (tm, 1), 0)
    group = h.shape[1] // len(POOL_WINDOWS)
    ys = []
    for gi, w in enumerate(POOL_WINDOWS):
        cols = slice(gi * group, (gi + 1) * group)
        total = u[:, cols]
        for j in range(1, w):
            total = total + ext_ref[halo - j:halo - j + tm, cols]
        count = jnp.minimum(t + 1, w).astype(F32)
        pooled = total / count - u[:, cols]
        ys.append(_dot(pooled.astype(BF16), w_ref[gi]))
    y = jnp.concatenate(ys, axis=-1) * sc_ref[...]
    return h + _rms(y, gout_ref[...])


def _pack_in_weights(w_in):
    d = w_in.shape[0]
    hw = FOX_HEADS * HEAD_DIM
    kv = NSA_GROUPS * HEAD_DIM
    sizes = (hw, hw, hw, FOX_HEADS, NSA_HEADS * HEAD_DIM, kv, kv, kv, kv, kv, kv, 3 * NSA_HEADS)
    offs = np.concatenate([[0], np.cumsum(sizes)])
    fq, fk, fv, ff, nq, kc, vc, ks, vs, kw, vw, ng = [w_in[:, offs[j]:offs[j + 1]] for j in range(len(sizes))]
    scale = HEAD_DIM ** -0.5 * LOG2E

    def dup(w):
        w = w.reshape(d, NSA_GROUPS, 1, HEAD_DIM)
        return jnp.broadcast_to(w, (d, NSA_GROUPS, 2, HEAD_DIM)).reshape(d, NSA_GROUPS * PAIR)

    w_main = jnp.concatenate([fq * scale, fk, fv, nq * scale, dup(ks), dup(vs), dup(kw), dup(vw)], axis=1)
    w_cmp = jnp.concatenate([kc, vc], axis=1)
    pad = jnp.zeros((d, LANES - FOX_HEADS - 3 * NSA_HEADS), F32)
    w_small = jnp.concatenate([ff, ng, pad], axis=1)
    w_small_hi = w_small.astype(BF16)
    w_small_lo = (w_small - w_small_hi.astype(F32)).astype(BF16)
    return w_main.astype(BF16), w_cmp.astype(BF16), jnp.concatenate([w_small_hi, w_small_lo], axis=1)


def _mixer_fox_nsa(h2, g_in, g_out, w_in, b_f, cmp_pe, cmp_w1, cmp_w2, w_out, *, batch, seq):
    tile = WINDOW
    assert seq % tile == 0 and seq // tile <= LANES
    w_main, w_cmp, w_small = _pack_in_weights(w_in)
    pj, pc, small, nrm = _inproj(h2, g_in, w_main, w_cmp, w_small, tm=tile)
    npair = FOX_HEADS // 2
    q_blk, k_blk, v_blk, nq_blk = 0, npair, 2 * npair, 3 * npair
    ks_blk = nq_blk + NSA_HEADS // 2
    vs_blk, kw_blk, vw_blk = ks_blk + NSA_GROUPS, ks_blk + 2 * NSA_GROUPS, ks_blk + 3 * NSA_GROUPS

    c = _forget_cumsum(small, b_f, batch=batch, seq=seq, tc=tile)
    nq = seq // tile
    fox_tq, fox_tk = min(FOX_TQ, seq), min(FOX_TK, seq)
    c_tiles = c.reshape(batch, npair, 2, seq // fox_tk, fox_tk).transpose(0, 1, 3, 2, 4)
    live = _fox_live_tiles(c, nrm, batch=batch, seq=seq, tq=fox_tq, tk=fox_tk, tm=tile)
    o_fox = _fox_attention(live, pj, c_tiles, batch=batch, seq=seq, tq=fox_tq, tk=fox_tk,
                           q_blk=q_blk, k_blk=k_blk, v_blk=v_blk)

    nch = seq // CMP_STRIDE
    ns = seq // SLC_BLOCK
    n_sel = min(N_SELECT, ns)
    kv = NSA_GROUPS * HEAD_DIM
    tch = pc.reshape(batch, nch, CMP_STRIDE, 2, NSA_GROUPS, HEAD_DIM)
    tch = tch.transpose(0, 3, 4, 1, 2, 5).reshape(batch, 2, NSA_GROUPS, nch, CMP_STRIDE * HEAD_DIM)
    pe8 = jnp.broadcast_to(cmp_pe.reshape(2, 1, CMP_BLOCK * HEAD_DIM), (2, 8, CMP_BLOCK * HEAD_DIM)).astype(BF16)
    w2dup = jnp.concatenate([cmp_w2, cmp_w2], axis=-1).astype(BF16)
    cmp_kv, cmp_nrm = _compress(tch, pe8, cmp_w1.astype(BF16), w2dup)

    group_slope = jnp.asarray([min(ALIBI_SLOPES_LOG2[g * NSA_GROUP_SIZE:(g + 1) * NSA_GROUP_SIZE])
                               for g in range(NSA_GROUPS)], F32)
    qn = nrm[:, NRM_NSA_Q, :NSA_HEADS].reshape(batch, nq, NSA_GROUPS, NSA_GROUP_SIZE).max(axis=3)
    ksn = nrm[:, NRM_NSA_KS, 0:2 * NSA_GROUPS:2].reshape(batch, nq, NSA_GROUPS).max(axis=1)
    kcn = cmp_nrm[:, 0, :, 0, 0]
    tiles = jnp.arange(nq)
    nseg = CMP_SEGMENTS if nq % CMP_SEGMENTS == 0 else 1
    chunk = nch // nseg
    last_end = (jnp.arange(nseg) + 1) * chunk * CMP_STRIDE + (CMP_BLOCK - 1 - CMP_STRIDE)
    dist_c = (tiles[:, None] * tile - last_end[None, :] - (CMP_STRIDE - 1)).astype(F32)
    dead_c = _alibi_dead(qn[..., None], kcn[:, None, :, None], dist_c[None, :, None, :],
                         group_slope[None, None, :, None])
    first_chunk = jnp.minimum(dead_c.sum(axis=3), (tiles * nseg // nq)[None, :, None])
    first_chunk = first_chunk.transpose(0, 2, 1).astype(jnp.int32).reshape(-1)
    dist_s = ((tiles[:, None] - tiles[None, :] - 1) * tile + 1).astype(F32)
    dead_s = _alibi_dead(qn[..., None], ksn[:, None, :, None], dist_s[None, :, None, :],
                         group_slope[None, None, :, None])
    live_s = jnp.logical_not(dead_s).transpose(0, 2, 1, 3).astype(jnp.int32)

    ratio = SLC_BLOCK // CMP_STRIDE
    r = CMP_BLOCK // CMP_STRIDE
    n_idx = np.arange(nch)[:, None]
    b_idx = np.arange(ns)[None, :]
    pool_np = ((n_idx >= ratio * b_idx - (r - 1)) & (n_idx <= ratio * b_idx + ratio - 1)
               & (n_idx < nch - r + 1)).astype(np.float32)
    pool = jnp.asarray(pool_np, dtype=BF16)
    tmap_np = (np.arange(ns)[:, None] // (tile // SLC_BLOCK) == np.arange(LANES)[None, :]).astype(np.float32)
    o_c, sel, flag_blocks = _cmp_select(first_chunk, pj, cmp_kv, pool, jnp.asarray(tmap_np, dtype=BF16),
                                        batch=batch, seq=seq, tq=tile, q_blk=nq_blk // 2, n_sel=n_sel, nseg=nseg)
    flags = (flag_blocks[:, :, :, 0, :nq] * live_s).reshape(-1)
    o_nsa = _nsa_attention(flags, pj, sel, o_c, small, batch=batch, seq=seq, tile=tile,
                           q_blk=nq_blk // 2, ks_blk=ks_blk, vs_blk=vs_blk, kw_blk=kw_blk, vw_blk=vw_blk)

    hw = FOX_HEADS * HEAD_DIM
    return o_fox, o_nsa, w_out[:hw].astype(BF16), w_out[hw:].astype(BF16), g_out


def kernel(x, norm_g, attn_w_in, fox_b_f, nsa_cmp_pe, nsa_cmp_w1, nsa_cmp_w2, attn_w_out,
           pool_w, pool_scale, ffn_w_gate, ffn_w_up, ffn_w_down):
    batch, seq, d = x.shape
    depth = norm_g.shape[0]
    tm = 512 if seq % 512 == 0 else seq
    h = x.reshape(batch * seq, d)
    for layer in range(depth):
        g = norm_g[layer].reshape(4, 1, d)
        i = layer // 2
        if layer % 2 == 0:
            mixer = "attn"
            mix = _mixer_fox_nsa(h, g[0], g[1], attn_w_in[i], fox_b_f[i], nsa_cmp_pe[i], nsa_cmp_w1[i],
                                 nsa_cmp_w2[i], attn_w_out[i], batch=batch, seq=seq)
        else:
            mixer = "pool"
            mix = (g[0], g[1], pool_w[i].astype(BF16), pool_scale[i].reshape(1, d))
        h = _layer_tail(h, g[2], g[3], ffn_w_gate[layer].astype(BF16), ffn_w_up[layer].astype(BF16),
                        ffn_w_down[layer].astype(BF16), seq=seq, tm=tm, chunk=256, mixer=mixer, mix=mix)
    return h.reshape(batch, seq, d)
```

```python
import functools

import numpy as np
import jax
import jax.numpy as jnp
from jax import lax
from jax.experimental import pallas as pl
from jax.experimental.pallas import tpu as pltpu

F32 = jnp.float32
BF16 = jnp.bfloat16

HEAD_DIM = 64
FOX_HEADS = 8
NSA_HEADS = 8
NSA_GROUPS = 2
NSA_GROUP_SIZE = NSA_HEADS // NSA_GROUPS
CMP_BLOCK = 32
CMP_STRIDE = 16
SLC_BLOCK = 64
N_SELECT = 16
WINDOW = 512
POOL_WINDOWS = (2, 4, 8, 16)
POOL_HALO = max(POOL_WINDOWS)
RMS_EPS = 1e-6
NEG_INF = -1e30
BIG = 1e30
REMOVED = -3e38
LANES = 128
PAIR = 2 * HEAD_DIM
VMEM_LIMIT = 56 * 1024 * 1024
CMP_SEGMENTS = 4
UNDERFLOW_BITS = 160.0
NORM_SLACK = 1.02
NORM_CHUNKS = (0, 1, 3, 4)
NRM_FOX_Q, NRM_FOX_K, NRM_NSA_Q, NRM_NSA_KS = range(4)
FOX_TQ, FOX_TK = 1024, 1024

LOG2E = 1.4426950408889634
ALIBI_SLOPES_LOG2 = tuple(float(2.0 ** (-8.0 * (i + 1.0) / NSA_HEADS)) * LOG2E for i in range(NSA_HEADS))


def _cparams(*sem):
    return pltpu.CompilerParams(dimension_semantics=sem, vmem_limit_bytes=VMEM_LIMIT)


def _rms(x, g):
    return x * lax.rsqrt(jnp.mean(x * x, axis=-1, keepdims=True) + RMS_EPS) * g


def _dot(a, b):
    return jnp.dot(a, b, preferred_element_type=F32)


def _dot_nt(a, b):
    return lax.dot_general(a, b, (((1,), (1,)), ((), ())), preferred_element_type=F32)


def _inproj_kernel(x_ref, g_ref, w_ref, wc_ref, ws_ref, hmap_ref, o_ref, oc_ref, os_ref, nrm_ref, *, fox_width):
    u = _rms(x_ref[...], g_ref[...])
    ub = u.astype(BF16)
    norms = []
    for c in range(0, o_ref.shape[1], fox_width):
        r = _dot(ub, w_ref[:, c:c + fox_width]).astype(BF16)
        o_ref[:, c:c + fox_width] = r
        if c // fox_width in NORM_CHUNKS:
            rf = r.astype(F32)
            sq = _dot((rf * rf).astype(BF16), hmap_ref[...])
            norms.append(jnp.max(sq, axis=0, keepdims=True))
    oc_ref[...] = _dot(ub, wc_ref[...])
    width = os_ref.shape[1]
    u_lo = (u - ub.astype(F32)).astype(BF16)
    both = _dot(ub, ws_ref[...])
    os_ref[...] = both[:, :width] + both[:, width:] + _dot(u_lo, ws_ref[:, :width])
    row = lax.broadcasted_iota(jnp.int32, nrm_ref.shape, 0)
    out = jnp.zeros(nrm_ref.shape, F32)
    for k, nk in enumerate(norms):
        out = jnp.where(row == k, nk, out)
    nrm_ref[...] = out


def _inproj(x2, g, w_main, w_cmp, w_small, *, tm):
    n, d = x2.shape
    cm, cc, cs = w_main.shape[1], w_cmp.shape[1], w_small.shape[1] // 2
    fox_width = FOX_HEADS * HEAD_DIM
    hmap = jnp.asarray((np.arange(fox_width)[:, None] // HEAD_DIM == np.arange(LANES)[None, :]).astype(np.float32),
                       dtype=BF16)
    const = lambda i: (0, 0)
    return pl.pallas_call(
        functools.partial(_inproj_kernel, fox_width=fox_width),
        out_shape=(jax.ShapeDtypeStruct((n, cm), BF16),
                   jax.ShapeDtypeStruct((n, cc), F32),
                   jax.ShapeDtypeStruct((n, cs), F32),
                   jax.ShapeDtypeStruct((n // tm, 8, LANES), F32)),
        grid=(n // tm,),
        in_specs=[pl.BlockSpec((tm, d), lambda i: (i, 0)),
                  pl.BlockSpec((1, d), const),
                  pl.BlockSpec((d, cm), const),
                  pl.BlockSpec((d, cc), const),
                  pl.BlockSpec((d, 2 * cs), const),
                  pl.BlockSpec((fox_width, LANES), const)],
        out_specs=(pl.BlockSpec((tm, cm), lambda i: (i, 0)),
                   pl.BlockSpec((tm, cc), lambda i: (i, 0)),
                   pl.BlockSpec((tm, cs), lambda i: (i, 0)),
                   pl.BlockSpec((None, 8, LANES), lambda i: (i, 0, 0))),
        compiler_params=_cparams("parallel"),
        name="inproj",
    )(x2, g, w_main, w_cmp, w_small, hmap)


def _forget_cumsum_kernel(sm_ref, bf_ref, tri_ref, o_ref, carry_ref):
    @pl.when(pl.program_id(1) == 0)
    def _():
        carry_ref[...] = jnp.zeros_like(carry_ref)

    zt = sm_ref[...].T
    z = zt[0:FOX_HEADS, :] + bf_ref[...]
    log_f = jnp.minimum(z, 0.0) - jnp.log1p(jnp.exp(-jnp.abs(z)))
    c = jnp.dot(log_f, tri_ref[...], preferred_element_type=F32,
                precision=lax.Precision.HIGHEST) + carry_ref[...]
    o_ref[...] = c * LOG2E
    carry_ref[...] = c[:, c.shape[1] - 1:]


def _forget_cumsum(small, b_f, *, batch, seq, tc):
    tri = jnp.asarray(np.triu(np.ones((tc, tc), np.float32)))
    nchunk = seq // tc
    return pl.pallas_call(
        _forget_cumsum_kernel,
        out_shape=jax.ShapeDtypeStruct((batch, FOX_HEADS, seq), F32),
        grid=(batch, nchunk),
        in_specs=[pl.BlockSpec((tc, small.shape[1]), lambda b, j: (b * nchunk + j, 0)),
                  pl.BlockSpec((FOX_HEADS, 1), lambda b, j: (0, 0)),
                  pl.BlockSpec((tc, tc), lambda b, j: (0, 0))],
        out_specs=pl.BlockSpec((None, FOX_HEADS, tc), lambda b, j: (b, 0, j)),
        scratch_shapes=[pltpu.VMEM((FOX_HEADS, 1), F32)],
        compiler_params=_cparams("parallel", "arbitrary"),
        name="forget_cumsum",
    )(small, b_f.reshape(FOX_HEADS, 1), tri)


def _lane_is_low():
    return lax.broadcasted_iota(jnp.int32, (1, PAIR), 1) < HEAD_DIM


def _with_ones(v_tile):
    return jnp.concatenate([v_tile, jnp.ones_like(v_tile)], axis=1)


def _flash_step(s, v, m_ref, acc_ref, l_ref=None):
    m_old = m_ref[...]
    m_new = jnp.maximum(m_old, jnp.max(s, axis=-1, keepdims=True))
    alpha = jnp.exp2(m_old - m_new)
    p = jnp.exp2(s - jnp.concatenate([m_new] * (s.shape[1] // LANES), axis=1))
    if l_ref is None:
        alpha_acc = jnp.concatenate([alpha, alpha], axis=1)
    else:
        alpha_acc = alpha
        l_ref[...] = alpha * l_ref[...] + jnp.sum(p, axis=-1, keepdims=True)
    acc_ref[...] = alpha_acc * acc_ref[...] + _dot(p.astype(BF16), v)
    m_ref[...] = m_new


def _flash_reset(m_ref, *sum_refs):
    m_ref[...] = jnp.full_like(m_ref, NEG_INF)
    for ref in sum_refs:
        ref[...] = jnp.zeros_like(ref)


def _flash_out(acc):
    return acc[:, :PAIR] / acc[:, PAIR:]


def _split_pair(q):
    low = _lane_is_low()
    zero = jnp.zeros_like(q)
    return jnp.where(low, q, zero), jnp.where(low, zero, q)


def _fox_kernel(live_ref, q_ref, k_ref, v_ref, c_ref, o_ref, m_ref, l_ref, acc_ref, *, tq, tk, nq):
    b = pl.program_id(0)
    pair = pl.program_id(1)
    i = pl.program_id(2)
    nk = c_ref.shape[0]
    qh = _split_pair(q_ref[...])
    _flash_reset(m_ref, l_ref, acc_ref)

    def head_step(h, j, mask):
        ks = pl.multiple_of(j * tk, tk)
        s = _dot_nt(qh[h], k_ref[pl.ds(ks, tk), :]) - c_ref[j, h:h + 1, :]
        if mask is not None:
            s = jnp.where(mask, s, NEG_INF)
        _flash_step(s, v_ref[pl.ds(ks, tk), :], m_ref.at[h], acc_ref.at[h], l_ref.at[h])

    def body(j, carry):
        live = [live_ref[(((b * FOX_HEADS) + 2 * pair + h) * nq + i) * nk + j] != 0 for h in range(2)]

        @pl.when(live[0] & live[1])
        def _():
            head_step(0, j, None)
            head_step(1, j, None)

        for h in range(2):
            @pl.when(live[h] & jnp.logical_not(live[1 - h]))
            def _(h=h):
                head_step(h, j, None)
        return carry

    lax.fori_loop(0, i, body, 0)

    def diag_part(h, r0, nr, nkeys):
        ks = pl.multiple_of(i * tq, tq)
        rows = pl.ds(r0, nr)
        row = lax.broadcasted_iota(jnp.int32, (nr, nkeys), 0) + r0
        col = lax.broadcasted_iota(jnp.int32, (nr, nkeys), 1)
        s = _dot_nt(qh[h][r0:r0 + nr], k_ref[pl.ds(ks, nkeys), :]) - c_ref[i, h:h + 1, 0:nkeys]
        _flash_step(jnp.where(col <= row, s, NEG_INF), v_ref[pl.ds(ks, nkeys), :],
                    m_ref.at[h, rows], acc_ref.at[h, rows], l_ref.at[h, rows])

    half = tq // 2
    for h in range(2):
        diag_part(h, 0, half, half)
        diag_part(h, half, half, tq)
    o_ref[...] = jnp.where(_lane_is_low(), acc_ref[0] / l_ref[0], acc_ref[1] / l_ref[1]).astype(o_ref.dtype)


def _fox_live_tiles(c2, nrm, *, batch, seq, tq, tk, tm):
    nq, nk = seq // tq, seq // tk
    qn = nrm[:, 0, :FOX_HEADS].reshape(batch, nq, tq // tm, FOX_HEADS).max(axis=2)
    kn = nrm[:, 1, :FOX_HEADS].reshape(batch, seq // tm, FOX_HEADS).max(axis=1)
    bound = 2.0 * NORM_SLACK * jnp.sqrt(qn * kn[:, None, :])
    c_start = c2[:, :, ::tq]
    c_end = c2[:, :, tk - 1::tk]
    decay = c_end[:, :, None, :] - c_start[:, :, :, None]
    live = decay < UNDERFLOW_BITS + jnp.transpose(bound, (0, 2, 1))[..., None]
    return live.astype(jnp.int32).reshape(-1)


def _fox_attention(live, pj, c_tiles, *, batch, seq, tq, tk, q_blk, k_blk, v_blk):
    assert tq == tk and tq % (2 * LANES) == 0
    nq = seq // tq
    nk = seq // tk
    npair = FOX_HEADS // 2
    grid_spec = pltpu.PrefetchScalarGridSpec(
        num_scalar_prefetch=1,
        grid=(batch, npair, nq),
        in_specs=[pl.BlockSpec((tq, PAIR), lambda b, p, i, f: (b * nq + i, q_blk + p)),
                  pl.BlockSpec((seq, PAIR), lambda b, p, i, f: (b, k_blk + p)),
                  pl.BlockSpec((seq, PAIR), lambda b, p, i, f: (b, v_blk + p)),
                  pl.BlockSpec((None, None, nk, 2, tk), lambda b, p, i, f: (b, p, 0, 0, 0))],
        out_specs=pl.BlockSpec((tq, PAIR), lambda b, p, i, f: (b * nq + i, p)),
        scratch_shapes=[pltpu.VMEM((2, tq, LANES), F32), pltpu.VMEM((2, tq, LANES), F32),
                        pltpu.VMEM((2, tq, PAIR), F32)])
    return pl.pallas_call(
        functools.partial(_fox_kernel, tq=tq, tk=tk, nq=nq),
        out_shape=jax.ShapeDtypeStruct((batch * seq, FOX_HEADS * HEAD_DIM), BF16),
        grid_spec=grid_spec,
        compiler_params=_cparams("parallel", "parallel", "arbitrary"),
        name="fox_attention",
    )(live, pj, pj, pj, c_tiles)


def _gelu_tanh(x):
    return 0.5 * x * (1.0 + jnp.tanh(0.7978845608028654 * (x + 0.044715 * x * x * x)))


def _compress_kernel(x_ref, pe_ref, w1pos_ref, w1dup_ref, w2sel_ref, o_ref, nrm_ref):
    nch = o_ref.shape[1]
    a = b = None
    for j in range(CMP_STRIDE):
        xj = x_ref[pl.ds(j, nch, stride=CMP_STRIDE), :].astype(BF16)
        pa, pb = _dot(xj, w1pos_ref[j]), _dot(xj, w1pos_ref[CMP_STRIDE + j])
        a, b = (pa, pb) if a is None else (a + pa, b + pb)
    pe_term = _dot(pe_ref[...], w1dup_ref[...])[0:1, :]
    h = a + pltpu.roll(b, nch - 1, axis=0) + pe_term
    y = _gelu_tanh(h).astype(BF16)
    for g in range(NSA_GROUPS):
        o = _dot(y, w2sel_ref[g]).astype(o_ref.dtype)
        o_ref[g] = o
        of = o.astype(F32)
        sq = 0.5 * jnp.sum(of * of, axis=-1, keepdims=True)
        nrm_ref[g] = jnp.broadcast_to(jnp.max(sq, axis=0, keepdims=True), nrm_ref.shape[1:])


def _compress(pc, cmp_pe, cmp_w1, cmp_w2, *, batch, seq):
    nch = seq // CMP_STRIDE
    dh = HEAD_DIM
    zero = jnp.zeros((2, CMP_BLOCK, dh, dh), F32)
    w1r = cmp_w1.reshape(2, CMP_BLOCK, dh, dh)
    w1pos = jnp.concatenate([jnp.concatenate([w1r, zero], axis=-1),
                             jnp.concatenate([zero, w1r], axis=-1)], axis=-2).astype(BF16)
    w1dup = jnp.concatenate([cmp_w1, cmp_w1], axis=-1).astype(BF16)
    w2dup = jnp.concatenate([cmp_w2, cmp_w2], axis=-1)
    zero2 = jnp.zeros_like(w2dup)
    w2sel = jnp.stack([jnp.concatenate([w2dup, zero2], axis=1),
                       jnp.concatenate([zero2, w2dup], axis=1)], axis=1).astype(BF16)
    pe8 = jnp.broadcast_to(cmp_pe.reshape(2, 1, CMP_BLOCK * dh), (2, 8, CMP_BLOCK * dh)).astype(BF16)
    return pl.pallas_call(
        _compress_kernel,
        out_shape=(jax.ShapeDtypeStruct((batch, 2, NSA_GROUPS, nch, PAIR), BF16),
                   jax.ShapeDtypeStruct((batch, 2, NSA_GROUPS, 8, LANES), F32)),
        grid=(batch, 2),
        in_specs=[pl.BlockSpec((seq, PAIR), lambda b, kv: (b, kv)),
                  pl.BlockSpec((None, 8, CMP_BLOCK * dh), lambda b, kv: (kv, 0, 0)),
                  pl.BlockSpec((None, CMP_BLOCK, PAIR, PAIR), lambda b, kv: (kv, 0, 0, 0)),
                  pl.BlockSpec((None, CMP_BLOCK * dh, PAIR), lambda b, kv: (kv, 0, 0)),
                  pl.BlockSpec((None, NSA_GROUPS, PAIR, PAIR), lambda b, kv: (kv, 0, 0, 0))],
        out_specs=(pl.BlockSpec((None, None, NSA_GROUPS, nch, PAIR), lambda b, kv: (b, kv, 0, 0, 0)),
                   pl.BlockSpec((None, None, NSA_GROUPS, 8, LANES), lambda b, kv: (b, kv, 0, 0, 0))),
        compiler_params=_cparams("parallel", "parallel"),
        name="nsa_compress",
    )(pc, pe8, w1pos, w1dup, w2sel)


def _group_slope(g, r):
    return jnp.where(g == 0, ALIBI_SLOPES_LOG2[r], ALIBI_SLOPES_LOG2[NSA_GROUP_SIZE + r]).astype(F32)


def _split3(x):
    hi = x.astype(BF16)
    r1 = x - hi.astype(F32)
    mid = r1.astype(BF16)
    lo = (r1 - mid.astype(F32)).astype(BF16)
    return hi, mid, lo


def _cmp_select_kernel(first_ref, q_ref, kc_ref, vc_ref, pool_ref, tmap_ref, oc_ref, sel_ref, flag_ref, *,
                       tq, n_sel, nseg, nq):
    b = pl.program_id(0)
    g = pl.program_id(1)
    i = pl.program_id(2)
    chunk = kc_ref.shape[0] // nseg
    first = first_ref[(b * NSA_GROUPS + g) * nq + i]
    n_live = (i * nseg) // nq + 1 - first
    for k in range(1, nseg + 1):
        @pl.when(n_live == k)
        def _(k=k):
            _cmp_select_body(q_ref, kc_ref, vc_ref, pool_ref, tmap_ref, oc_ref, sel_ref, flag_ref,
                             tq=tq, n_sel=n_sel, start=pl.multiple_of(first * chunk, chunk), nch=k * chunk)


def _cmp_select_body(q_ref, kc_ref, vc_ref, pool_ref, tmap_ref, oc_ref, sel_ref, flag_ref, *,
                     tq, n_sel, start, nch):
    g = pl.program_id(1)
    t0 = pl.program_id(2) * tq
    ns = pool_ref.shape[1]
    low = _lane_is_low()
    t = t0 + lax.broadcasted_iota(jnp.int32, (tq, 1), 0)
    cmp_end = (start + lax.broadcasted_iota(jnp.int32, (1, nch), 1)) * CMP_STRIDE + (CMP_BLOCK - 1)
    rel_end = (cmp_end - t0).astype(F32)
    mask_bias = jnp.where(t >= cmp_end, 0.0, NEG_INF)
    row_valid = jnp.where(t >= CMP_BLOCK - 1, 1.0, 0.0)
    kc = kc_ref[pl.ds(start, nch), :]
    vc = vc_ref[pl.ds(start, nch), :]
    imp = jnp.zeros((tq, nch), F32)
    for pair in range(NSA_GROUP_SIZE // 2):
        qh = _split_pair(q_ref[:, pair * PAIR:(pair + 1) * PAIR])
        outs = []
        for h in range(2):
            s = _dot_nt(qh[h], kc) + _group_slope(g, 2 * pair + h) * rel_end + mask_bias
            e = jnp.exp2(s - jnp.max(s, axis=-1, keepdims=True))
            p = e * (row_valid / jnp.maximum(jnp.sum(e, axis=-1, keepdims=True), 1e-30))
            outs.append(_dot(p.astype(BF16), vc))
            imp = imp + p
        oc_ref[:, pair * PAIR:(pair + 1) * PAIR] = jnp.where(low, outs[0], outs[1])

    pool = pool_ref[pl.ds(start, nch), :]
    hi, mid, lo = _split3(imp)
    p_slc = _dot(hi, pool) + _dot(mid, pool) + _dot(lo, pool)

    blk = lax.broadcasted_iota(jnp.int32, (1, ns), 1)
    cur = t // SLC_BLOCK
    forced = (blk == 0) | (blk == cur) | (blk == cur - 1)
    val = jnp.where(forced, REMOVED, jnp.where(blk > cur, NEG_INF, p_slc))
    sel = jnp.where(forced, 1.0, 0.0)
    blk_f = blk.astype(F32)
    for _ in range(n_sel - 3):
        mx = jnp.max(val, axis=-1, keepdims=True)
        pick = jnp.min(jnp.where(val == mx, blk_f, float(ns)), axis=-1, keepdims=True)
        hit = blk_f == pick
        sel = jnp.where(hit, 1.0, sel)
        val = jnp.where(hit, REMOVED, val)
    sel_ref[...] = sel
    any_sel = jnp.broadcast_to(jnp.max(sel, axis=0, keepdims=True), (8, ns)).astype(BF16)
    flag_ref[...] = (_dot(any_sel, tmap_ref[...]) > 0.5).astype(jnp.int32)


def _cmp_select(first_chunk, pj, cmp_kv, pool, tmap, *, batch, seq, tq, q_blk, n_sel, nseg):
    nq = seq // tq
    nch = cmp_kv.shape[3]
    ns = pool.shape[1]
    gw = NSA_GROUP_SIZE * HEAD_DIM
    grid_spec = pltpu.PrefetchScalarGridSpec(
        num_scalar_prefetch=1,
        grid=(batch, NSA_GROUPS, nq),
        in_specs=[pl.BlockSpec((tq, gw), lambda b, g, i, f: (b * nq + i, q_blk + g)),
                  pl.BlockSpec((None, None, None, nch, PAIR), lambda b, g, i, f: (b, 0, g, 0, 0)),
                  pl.BlockSpec((None, None, None, nch, PAIR), lambda b, g, i, f: (b, 1, g, 0, 0)),
                  pl.BlockSpec((nch, ns), lambda b, g, i, f: (0, 0)),
                  pl.BlockSpec((ns, LANES), lambda b, g, i, f: (0, 0))],
        out_specs=(pl.BlockSpec((tq, gw), lambda b, g, i, f: (b * nq + i, g)),
                   pl.BlockSpec((None, None, tq, ns), lambda b, g, i, f: (b, g, i, 0)),
                   pl.BlockSpec((None, None, None, 8, LANES), lambda b, g, i, f: (b, g, i, 0, 0))))
    return pl.pallas_call(
        functools.partial(_cmp_select_kernel, tq=tq, n_sel=n_sel, nseg=nseg, nq=nq),
        out_shape=(jax.ShapeDtypeStruct((batch * seq, NSA_HEADS * HEAD_DIM), F32),
                   jax.ShapeDtypeStruct((batch, NSA_GROUPS, seq, ns), F32),
                   jax.ShapeDtypeStruct((batch, NSA_GROUPS, nq, 8, LANES), jnp.int32)),
        grid_spec=grid_spec,
        compiler_params=_cparams("parallel", "parallel", "parallel"),
        name="nsa_cmp_select",
    )(first_chunk, pj, cmp_kv, cmp_kv, pool, tmap)


def _alibi_dead(q_sq, k_sq, dist, group_slope):
    bound = 2.0 * NORM_SLACK * jnp.sqrt(q_sq * k_sq)
    return group_slope * dist >= UNDERFLOW_BITS + bound


def _nsa_kernel(flags_ref, q_ref, ks_ref, vs_ref, kwp_ref, vwp_ref, kwd_ref, vwd_ref, sel_ref, oc_ref, gate_ref,
                o_ref, m_ref, acc_ref, *, tile, nq):
    b = pl.program_id(0)
    g = pl.program_id(1)
    i = pl.program_id(2)
    nh = NSA_GROUP_SIZE
    ns = sel_ref.shape[1]
    bpt = tile // SLC_BLOCK
    low = _lane_is_low()
    row = lax.broadcasted_iota(jnp.int32, (tile, tile), 0)
    col = lax.broadcasted_iota(jnp.int32, (tile, tile), 1)
    relpos = lax.broadcasted_iota(jnp.int32, (1, tile), 1).astype(F32)
    sel = sel_ref[...].astype(BF16)
    blk_of_key = lax.broadcasted_iota(jnp.int32, (ns, tile), 1) // SLC_BLOCK
    blk_row = lax.broadcasted_iota(jnp.int32, (ns, tile), 0)
    qh = []
    for pair in range(nh // 2):
        qh.extend(_split_pair(q_ref[:, pair * PAIR:(pair + 1) * PAIR]))
    slopes = [_group_slope(g, r) for r in range(nh)]
    _flash_reset(m_ref, acc_ref)

    def selected(j):
        expand = jnp.where(blk_row == blk_of_key + j * bpt, 1.0, 0.0).astype(BF16)
        return _dot(sel, expand) > 0.5

    def attend(slot, k_keys, v_keys, rel, keep):
        v_aug = _with_ones(v_keys)
        bias = jnp.where(keep, 0.0, NEG_INF)
        for r in range(nh):
            s = _dot_nt(qh[r], k_keys) + slopes[r] * rel + bias
            _flash_step(s, v_aug, m_ref.at[slot + r], acc_ref.at[slot + r])

    def sel_step(j, keep):
        ks = pl.multiple_of(j * tile, tile)
        attend(0, ks_ref[pl.ds(ks, tile), :], vs_ref[pl.ds(ks, tile), :],
               relpos + ((j - i) * tile).astype(F32), keep)

    sel_step(i, selected(i) & (col <= row))

    def sel_body(j, carry):
        @pl.when(flags_ref[((b * NSA_GROUPS + g) * nq + i) * nq + j] != 0)
        def _():
            sel_step(j, selected(j))
        return carry

    lax.fori_loop(0, i, sel_body, 0)

    row2 = lax.broadcasted_iota(jnp.int32, (tile, 2 * tile), 0)
    col2 = lax.broadcasted_iota(jnp.int32, (tile, 2 * tile), 1)
    in_window = ((col2 >= tile) & (col2 - tile <= row2)) | ((col2 < tile) & (col2 > row2) & (i > 0))
    rel2 = (lax.broadcasted_iota(jnp.int32, (1, 2 * tile), 1) - tile).astype(F32)
    attend(nh, jnp.concatenate([kwp_ref[...], kwd_ref[...]], axis=0),
           jnp.concatenate([vwp_ref[...], vwd_ref[...]], axis=0), rel2, in_window)

    gates = jax.nn.sigmoid(gate_ref[...])

    def gate_col(r, branch):
        c0 = FOX_HEADS + 3 * r + branch
        c1 = c0 + 3 * nh
        return jnp.where(g == 0, gates[:, c0:c0 + 1], gates[:, c1:c1 + 1])

    for pair in range(nh // 2):
        o_c = oc_ref[:, pair * PAIR:(pair + 1) * PAIR]
        outs = []
        for h in range(2):
            r = 2 * pair + h
            o_s = _flash_out(acc_ref[r])
            o_w = _flash_out(acc_ref[nh + r])
            outs.append(gate_col(r, 0) * o_c + gate_col(r, 1) * o_s + gate_col(r, 2) * o_w)
        o_ref[:, pair * PAIR:(pair + 1) * PAIR] = jnp.where(low, outs[0], outs[1]).astype(o_ref.dtype)


def _nsa_attention(flags, pj, sel, o_c, small, *, batch, seq, tile, q_blk, ks_blk, vs_blk, kw_blk, vw_blk):
    nq = seq // tile
    ns = sel.shape[3]
    gw = NSA_GROUP_SIZE * HEAD_DIM
    slots = 2 * NSA_GROUP_SIZE
    cur = lambda b, g, i, f, base: (b * nq + i, base + g)
    prev = lambda b, g, i, f, base: (b * nq + jnp.maximum(i - 1, 0), base + g)
    grid_spec = pltpu.PrefetchScalarGridSpec(
        num_scalar_prefetch=1,
        grid=(batch, NSA_GROUPS, nq),
        in_specs=[pl.BlockSpec((tile, gw), functools.partial(cur, base=q_blk)),
                  pl.BlockSpec((seq, PAIR), lambda b, g, i, f: (b, ks_blk + g)),
                  pl.BlockSpec((seq, PAIR), lambda b, g, i, f: (b, vs_blk + g)),
                  pl.BlockSpec((tile, PAIR), functools.partial(prev, base=kw_blk)),
                  pl.BlockSpec((tile, PAIR), functools.partial(prev, base=vw_blk)),
                  pl.BlockSpec((tile, PAIR), functools.partial(cur, base=kw_blk)),
                  pl.BlockSpec((tile, PAIR), functools.partial(cur, base=vw_blk)),
                  pl.BlockSpec((None, None, tile, ns), lambda b, g, i, f: (b, g, i, 0)),
                  pl.BlockSpec((tile, gw), functools.partial(cur, base=0)),
                  pl.BlockSpec((tile, small.shape[1]), lambda b, g, i, f: (b * nq + i, 0))],
        out_specs=pl.BlockSpec((tile, gw), functools.partial(cur, base=0)),
        scratch_shapes=[pltpu.VMEM((slots, tile, LANES), F32), pltpu.VMEM((slots, tile, 2 * PAIR), F32)])
    return pl.pallas_call(
        functools.partial(_nsa_kernel, tile=tile, nq=nq),
        out_shape=jax.ShapeDtypeStruct((batch * seq, NSA_HEADS * HEAD_DIM), BF16),
        grid_spec=grid_spec,
        compiler_params=_cparams("parallel", "parallel", "arbitrary"),
        name="nsa_attention",
    )(flags, pj, pj, pj, pj, pj, pj, pj, sel, o_c, small)


def _ffn_kernel(h_ref, gin_ref, gout_ref, wg_ref, wu_ref, wd_ref, *rest, chunk, mixer, tm, tiles_per_seq):
    h = h_ref[...]
    if mixer == "attn":
        a_ref, b_ref, wa_ref, wb_ref, gmix_ref, o_ref, acc_ref = rest
        h = h + _rms(_dot(a_ref[...], wa_ref[...]) + _dot(b_ref[...], wb_ref[...]), gmix_ref[...])
    else:
        halo_ref, pin_ref, pout_ref, pw_ref, sc_ref, o_ref, acc_ref, ext_ref = rest
        h = _pool_mix(h, halo_ref, pin_ref, pout_ref, pw_ref, sc_ref, ext_ref, tm=tm, tiles_per_seq=tiles_per_seq)
    ub = _rms(h, gin_ref[...]).astype(BF16)
    hidden = wg_ref.shape[1]
    for idx, c in enumerate(range(0, hidden, chunk)):
        gate = _dot(ub, wg_ref[:, c:c + chunk])
        up = _dot(ub, wu_ref[:, c:c + chunk])
        act = (gate * jax.nn.sigmoid(gate) * up).astype(BF16)
        part = _dot(act, wd_ref[c:c + chunk, :])
        if idx == 0:
            acc_ref[...] = part
        else:
            acc_ref[...] += part
    o_ref[...] = h + _rms(acc_ref[...], gout_ref[...])


def _layer_tail(h2, g_in, g_out, wg, wu, wd, *, seq, tm, chunk, mixer, mix):
    n, d = h2.shape
    const = lambda i: (0, 0)
    rows = lambda i: (i, 0)
    scratch = [pltpu.VMEM((tm, d), F32)]
    if mixer == "attn":
        a, b, wa, wb, g_mix = mix
        mix_specs = [pl.BlockSpec((tm, a.shape[1]), rows), pl.BlockSpec((tm, b.shape[1]), rows),
                     pl.BlockSpec(wa.shape, const), pl.BlockSpec(wb.shape, const), pl.BlockSpec((1, d), const)]
    else:
        p_in, p_out, w_groups, scale = mix
        ratio = tm // POOL_HALO
        mix = (h2, p_in, p_out, w_groups, scale)
        mix_specs = [pl.BlockSpec((POOL_HALO, d), lambda i: (jnp.maximum(i * ratio - 1, 0), 0)),
                     pl.BlockSpec((1, d), const), pl.BlockSpec((1, d), const),
                     pl.BlockSpec(w_groups.shape, lambda i: (0, 0, 0)), pl.BlockSpec((1, d), const)]
        scratch.append(pltpu.VMEM((tm + POOL_HALO, d), F32))
    return pl.pallas_call(
        functools.partial(_ffn_kernel, chunk=chunk, mixer=mixer, tm=tm, tiles_per_seq=seq // tm),
        out_shape=jax.ShapeDtypeStruct((n, d), F32),
        grid=(n // tm,),
        in_specs=[pl.BlockSpec((tm, d), rows),
                  pl.BlockSpec((1, d), const),
                  pl.BlockSpec((1, d), const),
                  pl.BlockSpec(wg.shape, const),
                  pl.BlockSpec(wu.shape, const),
                  pl.BlockSpec(wd.shape, const)] + mix_specs,
        out_specs=pl.BlockSpec((tm, d), rows),
        scratch_shapes=scratch,
        compiler_params=_cparams("parallel"),
        name=f"{mixer}_tail_swiglu_ffn",
    )(h2, g_in, g_out, wg, wu, wd, *mix)


def _pool_mix(h, halo_ref, gin_ref, gout_ref, w_ref, sc_ref, ext_ref, *, tm, tiles_per_seq):
    halo = POOL_HALO
    i = pl.program_id(0)
    first = (i % tiles_per_seq) == 0
    u = _rms(h, gin_ref[...])
    uh = _rms(halo_ref[...], gin_ref[...])
    ext_ref[0:halo, :] = jnp.where(first, 0.0, uh)
    ext_ref[halo:, :] = u
    t = (i % tiles_per_seq) * tm + lax.broadcasted_iota(jnp.int32, (tm, 1), 0)
    group = h.shape[1] // len(POOL_WINDOWS)
    ys = []
    for gi, w in enumerate(POOL_WINDOWS):
        cols = slice(gi * group, (gi + 1) * group)
        total = u[:, cols]
        for j in range(1, w):
            total = total + ext_ref[halo - j:halo - j + tm, cols]
        count = jnp.minimum(t + 1, w).astype(F32)
        pooled = total / count - u[:, cols]
        ys.append(_dot(pooled.astype(BF16), w_ref[gi]))
    y = jnp.concatenate(ys, axis=-1) * sc_ref[...]
    return h + _rms(y, gout_ref[...])


def _pack_in_weights(w_in):
    d = w_in.shape[0]
    hw = FOX_HEADS * HEAD_DIM
    kv = NSA_GROUPS * HEAD_DIM
    sizes = (hw, hw, hw, FOX_HEADS, NSA_HEADS * HEAD_DIM, kv, kv, kv, kv, kv, kv, 3 * NSA_HEADS)
    offs = np.concatenate([[0], np.cumsum(sizes)])
    fq, fk, fv, ff, nq, kc, vc, ks, vs, kw, vw, ng = [w_in[:, offs[j]:offs[j + 1]] for j in range(len(sizes))]
    scale = HEAD_DIM ** -0.5 * LOG2E

    def dup(w):
        w = w.reshape(d, NSA_GROUPS, 1, HEAD_DIM)
        return jnp.broadcast_to(w, (d, NSA_GROUPS, 2, HEAD_DIM)).reshape(d, NSA_GROUPS * PAIR)

    w_main = jnp.concatenate([fq * scale, fk, fv, nq * scale, dup(ks), dup(vs), dup(kw), dup(vw)], axis=1)
    w_cmp = jnp.concatenate([kc, vc], axis=1)
    pad = jnp.zeros((d, LANES - FOX_HEADS - 3 * NSA_HEADS), F32)
    w_small = jnp.concatenate([ff, ng, pad], axis=1)
    w_small_hi = w_small.astype(BF16)
    w_small_lo = (w_small - w_small_hi.astype(F32)).astype(BF16)
    return w_main.astype(BF16), w_cmp.astype(BF16), jnp.concatenate([w_small_hi, w_small_lo], axis=1)


def _mixer_fox_nsa(h2, g_in, g_out, w_in, b_f, cmp_pe, cmp_w1, cmp_w2, w_out, *, batch, seq):
    tile = WINDOW
    assert seq % tile == 0 and seq // tile <= LANES
    w_main, w_cmp, w_small = _pack_in_weights(w_in)
    pj, pc, small, nrm = _inproj(h2, g_in, w_main, w_cmp, w_small, tm=tile)
    npair = FOX_HEADS // 2
    q_blk, k_blk, v_blk, nq_blk = 0, npair, 2 * npair, 3 * npair
    ks_blk = nq_blk + NSA_HEADS // 2
    vs_blk, kw_blk, vw_blk = ks_blk + NSA_GROUPS, ks_blk + 2 * NSA_GROUPS, ks_blk + 3 * NSA_GROUPS

    c = _forget_cumsum(small, b_f, batch=batch, seq=seq, tc=tile)
    nq = seq // tile
    fox_tq, fox_tk = min(FOX_TQ, seq), min(FOX_TK, seq)
    c_tiles = c.reshape(batch, npair, 2, seq // fox_tk, fox_tk).transpose(0, 1, 3, 2, 4)
    live = _fox_live_tiles(c, nrm, batch=batch, seq=seq, tq=fox_tq, tk=fox_tk, tm=tile)
    o_fox = _fox_attention(live, pj, c_tiles, batch=batch, seq=seq, tq=fox_tq, tk=fox_tk,
                           q_blk=q_blk, k_blk=k_blk, v_blk=v_blk)

    nch = seq // CMP_STRIDE
    ns = seq // SLC_BLOCK
    n_sel = min(N_SELECT, ns)
    kv = NSA_GROUPS * HEAD_DIM
    cmp_kv, cmp_nrm = _compress(pc, cmp_pe, cmp_w1, cmp_w2, batch=batch, seq=seq)

    group_slope = jnp.asarray([min(ALIBI_SLOPES_LOG2[g * NSA_GROUP_SIZE:(g + 1) * NSA_GROUP_SIZE])
                               for g in range(NSA_GROUPS)], F32)
    qn = nrm[:, NRM_NSA_Q, :NSA_HEADS].reshape(batch, nq, NSA_GROUPS, NSA_GROUP_SIZE).max(axis=3)
    ksn = nrm[:, NRM_NSA_KS, 0:2 * NSA_GROUPS:2].reshape(batch, nq, NSA_GROUPS).max(axis=1)
    kcn = cmp_nrm[:, 0, :, 0, 0]
    tiles = jnp.arange(nq)
    nseg = CMP_SEGMENTS if nq % CMP_SEGMENTS == 0 else 1
    chunk = nch // nseg
    last_end = (jnp.arange(nseg) + 1) * chunk * CMP_STRIDE + (CMP_BLOCK - 1 - CMP_STRIDE)
    dist_c = (tiles[:, None] * tile - last_end[None, :] - (CMP_STRIDE - 1)).astype(F32)
    dead_c = _alibi_dead(qn[..., None], kcn[:, None, :, None], dist_c[None, :, None, :],
                         group_slope[None, None, :, None])
    first_chunk = jnp.minimum(dead_c.sum(axis=3), (tiles * nseg // nq)[None, :, None])
    first_chunk = first_chunk.transpose(0, 2, 1).astype(jnp.int32).reshape(-1)
    dist_s = ((tiles[:, None] - tiles[None, :] - 1) * tile + 1).astype(F32)
    dead_s = _alibi_dead(qn[..., None], ksn[:, None, :, None], dist_s[None, :, None, :],
                         group_slope[None, None, :, None])
    live_s = jnp.logical_not(dead_s).transpose(0, 2, 1, 3).astype(jnp.int32)

    ratio = SLC_BLOCK // CMP_STRIDE
    r = CMP_BLOCK // CMP_STRIDE
    n_idx = np.arange(nch)[:, None]
    b_idx = np.arange(ns)[None, :]
    pool_np = ((n_idx >= ratio * b_idx - (r - 1)) & (n_idx <= ratio * b_idx + ratio - 1)
               & (n_idx < nch - r + 1)).astype(np.float32)
    pool = jnp.asarray(pool_np, dtype=BF16)
    tmap_np = (np.arange(ns)[:, None] // (tile // SLC_BLOCK) == np.arange(LANES)[None, :]).astype(np.float32)
    o_c, sel, flag_blocks = _cmp_select(first_chunk, pj, cmp_kv, pool, jnp.asarray(tmap_np, dtype=BF16),
                                        batch=batch, seq=seq, tq=tile, q_blk=nq_blk // 2, n_sel=n_sel, nseg=nseg)
    flags = (flag_blocks[:, :, :, 0, :nq] * live_s).reshape(-1)
    o_nsa = _nsa_attention(flags, pj, sel, o_c, small, batch=batch, seq=seq, tile=tile,
                           q_blk=nq_blk // 2, ks_blk=ks_blk, vs_blk=vs_blk, kw_blk=kw_blk, vw_blk=vw_blk)

    hw = FOX_HEADS * HEAD_DIM
    return o_fox, o_nsa, w_out[:hw].astype(BF16), w_out[hw:].astype(BF16), g_out


def kernel(x, norm_g, attn_w_in, fox_b_f, nsa_cmp_pe, nsa_cmp_w1, nsa_cmp_w2, attn_w_out,
           pool_w, pool_scale, ffn_w_gate, ffn_w_up, ffn_w_down):
    batch, seq, d = x.shape
    depth = norm_g.shape[0]
    tm = 512 if seq % 512 == 0 else seq
    h = x.reshape(batch * seq, d)
    for layer in range(depth):
        g = norm_g[layer].reshape(4, 1, d)
        i = layer // 2
        if layer % 2 == 0:
            mixer = "attn"
            mix = _mixer_fox_nsa(h, g[0], g[1], attn_w_in[i], fox_b_f[i], nsa_cmp_pe[i], nsa_cmp_w1[i],
                                 nsa_cmp_w2[i], attn_w_out[i], batch=batch, seq=seq)
        else:
            mixer = "pool"
            mix = (g[0], g[1], pool_w[i].astype(BF16), pool_scale[i].reshape(1, d))
        h = _layer_tail(h, g[2], g[3], ffn_w_gate[layer].astype(BF16), ffn_w_up[layer].astype(BF16),
                        ffn_w_down[layer].astype(BF16), seq=seq, tm=tm, chunk=256, mixer=mixer, mix=mix)
    return h.reshape(batch, seq, d)
```

```python
import functools

import numpy as np
import jax
import jax.numpy as jnp
from jax import lax
from jax.experimental import pallas as pl
from jax.experimental.pallas import tpu as pltpu

F32 = jnp.float32
BF16 = jnp.bfloat16

HEAD_DIM = 64
FOX_HEADS = 8
NSA_HEADS = 8
NSA_GROUPS = 2
NSA_GROUP_SIZE = NSA_HEADS // NSA_GROUPS
CMP_BLOCK = 32
CMP_STRIDE = 16
SLC_BLOCK = 64
N_SELECT = 16
WINDOW = 512
POOL_WINDOWS = (2, 4, 8, 16)
POOL_HALO = max(POOL_WINDOWS)
RMS_EPS = 1e-6
NEG_INF = -1e30
BIG = 1e30
REMOVED = -3e38
LANES = 128
PAIR = 2 * HEAD_DIM
VMEM_LIMIT = 56 * 1024 * 1024
CMP_SEGMENTS = 4
UNDERFLOW_BITS = 160.0
NORM_SLACK = 1.02
NORM_CHUNKS = (0, 1, 3, 4)
NRM_FOX_Q, NRM_FOX_K, NRM_NSA_Q, NRM_NSA_KS = range(4)
FOX_TQ, FOX_TK = 1024, 1024

LOG2E = 1.4426950408889634
ALIBI_SLOPES_LOG2 = tuple(float(2.0 ** (-8.0 * (i + 1.0) / NSA_HEADS)) * LOG2E for i in range(NSA_HEADS))


def _cparams(*sem):
    return pltpu.CompilerParams(dimension_semantics=sem, vmem_limit_bytes=VMEM_LIMIT)


def _rms(x, g):
    return x * lax.rsqrt(jnp.mean(x * x, axis=-1, keepdims=True) + RMS_EPS) * g


def _dot(a, b):
    return jnp.dot(a, b, preferred_element_type=F32)


def _dot_nt(a, b):
    return lax.dot_general(a, b, (((1,), (1,)), ((), ())), preferred_element_type=F32)


def _inproj_kernel(x_ref, g_ref, w_ref, wc_ref, ws_ref, hmap_ref, o_ref, oc_ref, os_ref, nrm_ref, *, fox_width):
    u = _rms(x_ref[...], g_ref[...])
    ub = u.astype(BF16)
    norms = []
    for c in range(0, o_ref.shape[1], fox_width):
        r = _dot(ub, w_ref[:, c:c + fox_width]).astype(BF16)
        o_ref[:, c:c + fox_width] = r
        if c // fox_width in NORM_CHUNKS:
            rf = r.astype(F32)
            sq = _dot((rf * rf).astype(BF16), hmap_ref[...])
            norms.append(jnp.max(sq, axis=0, keepdims=True))
    oc_ref[...] = _dot(ub, wc_ref[...])
    width = os_ref.shape[1]
    u_lo = (u - ub.astype(F32)).astype(BF16)
    both = _dot(ub, ws_ref[...])
    os_ref[...] = both[:, :width] + both[:, width:] + _dot(u_lo, ws_ref[:, :width])
    row = lax.broadcasted_iota(jnp.int32, nrm_ref.shape, 0)
    out = jnp.zeros(nrm_ref.shape, F32)
    for k, nk in enumerate(norms):
        out = jnp.where(row == k, nk, out)
    nrm_ref[...] = out


def _inproj(x2, g, w_main, w_cmp, w_small, *, tm):
    n, d = x2.shape
    cm, cc, cs = w_main.shape[1], w_cmp.shape[1], w_small.shape[1] // 2
    fox_width = FOX_HEADS * HEAD_DIM
    hmap = jnp.asarray((np.arange(fox_width)[:, None] // HEAD_DIM == np.arange(LANES)[None, :]).astype(np.float32),
                       dtype=BF16)
    const = lambda i: (0, 0)
    return pl.pallas_call(
        functools.partial(_inproj_kernel, fox_width=fox_width),
        out_shape=(jax.ShapeDtypeStruct((n, cm), BF16),
                   jax.ShapeDtypeStruct((n, cc), F32),
                   jax.ShapeDtypeStruct((n, cs), F32),
                   jax.ShapeDtypeStruct((n // tm, 8, LANES), F32)),
        grid=(n // tm,),
        in_specs=[pl.BlockSpec((tm, d), lambda i: (i, 0)),
                  pl.BlockSpec((1, d), const),
                  pl.BlockSpec((d, cm), const),
                  pl.BlockSpec((d, cc), const),
                  pl.BlockSpec((d, 2 * cs), const),
                  pl.BlockSpec((fox_width, LANES), const)],
        out_specs=(pl.BlockSpec((tm, cm), lambda i: (i, 0)),
                   pl.BlockSpec((tm, cc), lambda i: (i, 0)),
                   pl.BlockSpec((tm, cs), lambda i: (i, 0)),
                   pl.BlockSpec((None, 8, LANES), lambda i: (i, 0, 0))),
        compiler_params=_cparams("parallel"),
        name="inproj",
    )(x2, g, w_main, w_cmp, w_small, hmap)


def _forget_cumsum_kernel(sm_ref, bf_ref, tri_ref, o_ref, carry_ref):
    @pl.when(pl.program_id(1) == 0)
    def _():
        carry_ref[...] = jnp.zeros_like(carry_ref)

    zt = sm_ref[...].T
    z = zt[0:FOX_HEADS, :] + bf_ref[...]
    log_f = jnp.minimum(z, 0.0) - jnp.log1p(jnp.exp(-jnp.abs(z)))
    c = jnp.dot(log_f, tri_ref[...], preferred_element_type=F32,
                precision=lax.Precision.HIGHEST) + carry_ref[...]
    o_ref[...] = c * LOG2E
    carry_ref[...] = c[:, c.shape[1] - 1:]


def _forget_cumsum(small, b_f, *, batch, seq, tc):
    tri = jnp.asarray(np.triu(np.ones((tc, tc), np.float32)))
    nchunk = seq // tc
    return pl.pallas_call(
        _forget_cumsum_kernel,
        out_shape=jax.ShapeDtypeStruct((batch, FOX_HEADS, seq), F32),
        grid=(batch, nchunk),
        in_specs=[pl.BlockSpec((tc, small.shape[1]), lambda b, j: (b * nchunk + j, 0)),
                  pl.BlockSpec((FOX_HEADS, 1), lambda b, j: (0, 0)),
                  pl.BlockSpec((tc, tc), lambda b, j: (0, 0))],
        out_specs=pl.BlockSpec((None, FOX_HEADS, tc), lambda b, j: (b, 0, j)),
        scratch_shapes=[pltpu.VMEM((FOX_HEADS, 1), F32)],
        compiler_params=_cparams("parallel", "arbitrary"),
        name="forget_cumsum",
    )(small, b_f.reshape(FOX_HEADS, 1), tri)


def _lane_is_low():
    return lax.broadcasted_iota(jnp.int32, (1, PAIR), 1) < HEAD_DIM


def _with_ones(v_tile):
    return jnp.concatenate([v_tile, jnp.ones_like(v_tile)], axis=1)


def _flash_step(s, v, m_ref, acc_ref, l_ref=None):
    m_old = m_ref[...]
    m_new = jnp.maximum(m_old, jnp.max(s, axis=-1, keepdims=True))
    alpha = jnp.exp2(m_old - m_new)
    p = jnp.exp2(s - jnp.concatenate([m_new] * (s.shape[1] // LANES), axis=1))
    if l_ref is None:
        alpha_acc = jnp.concatenate([alpha, alpha], axis=1)
    else:
        alpha_acc = alpha
        l_ref[...] = alpha * l_ref[...] + jnp.sum(p, axis=-1, keepdims=True)
    acc_ref[...] = alpha_acc * acc_ref[...] + _dot(p.astype(BF16), v)
    m_ref[...] = m_new


def _flash_reset(m_ref, *sum_refs):
    m_ref[...] = jnp.full_like(m_ref, NEG_INF)
    for ref in sum_refs:
        ref[...] = jnp.zeros_like(ref)


def _flash_out(acc):
    return acc[:, :PAIR] / acc[:, PAIR:]


def _split_pair(q):
    low = _lane_is_low()
    zero = jnp.zeros_like(q)
    return jnp.where(low, q, zero), jnp.where(low, zero, q)


def _fox_kernel(live_ref, q_ref, k_ref, v_ref, c_ref, o_ref, m_ref, l_ref, acc_ref, *, tq, tk, nq):
    b = pl.program_id(0)
    pair = pl.program_id(1)
    i = pl.program_id(2)
    nk = c_ref.shape[0]
    qh = _split_pair(q_ref[...])
    _flash_reset(m_ref, l_ref, acc_ref)

    def head_step(h, j, mask):
        ks = pl.multiple_of(j * tk, tk)
        s = _dot_nt(qh[h], k_ref[pl.ds(ks, tk), :]) - c_ref[j, h:h + 1, :]
        if mask is not None:
            s = jnp.where(mask, s, NEG_INF)
        _flash_step(s, v_ref[pl.ds(ks, tk), :], m_ref.at[h], acc_ref.at[h], l_ref.at[h])

    def body(j, carry):
        live = [live_ref[(((b * FOX_HEADS) + 2 * pair + h) * nq + i) * nk + j] != 0 for h in range(2)]

        @pl.when(live[0] & live[1])
        def _():
            head_step(0, j, None)
            head_step(1, j, None)

        for h in range(2):
            @pl.when(live[h] & jnp.logical_not(live[1 - h]))
            def _(h=h):
                head_step(h, j, None)
        return carry

    lax.fori_loop(0, i, body, 0)

    def diag_part(h, r0, nr, nkeys):
        ks = pl.multiple_of(i * tq, tq)
        rows = pl.ds(r0, nr)
        row = lax.broadcasted_iota(jnp.int32, (nr, nkeys), 0) + r0
        col = lax.broadcasted_iota(jnp.int32, (nr, nkeys), 1)
        s = _dot_nt(qh[h][r0:r0 + nr], k_ref[pl.ds(ks, nkeys), :]) - c_ref[i, h:h + 1, 0:nkeys]
        _flash_step(jnp.where(col <= row, s, NEG_INF), v_ref[pl.ds(ks, nkeys), :],
                    m_ref.at[h, rows], acc_ref.at[h, rows], l_ref.at[h, rows])

    half = tq // 2
    for h in range(2):
        diag_part(h, 0, half, half)
        diag_part(h, half, half, tq)
    o_ref[...] = jnp.where(_lane_is_low(), acc_ref[0] / l_ref[0], acc_ref[1] / l_ref[1]).astype(o_ref.dtype)


def _fox_live_tiles(c2, nrm, *, batch, seq, tq, tk, tm):
    nq, nk = seq // tq, seq // tk
    qn = nrm[:, 0, :FOX_HEADS].reshape(batch, nq, tq // tm, FOX_HEADS).max(axis=2)
    kn = nrm[:, 1, :FOX_HEADS].reshape(batch, seq // tm, FOX_HEADS).max(axis=1)
    bound = 2.0 * NORM_SLACK * jnp.sqrt(qn * kn[:, None, :])
    c_start = c2[:, :, ::tq]
    c_end = c2[:, :, tk - 1::tk]
    decay = c_end[:, :, None, :] - c_start[:, :, :, None]
    live = decay < UNDERFLOW_BITS + jnp.transpose(bound, (0, 2, 1))[..., None]
    return live.astype(jnp.int32).reshape(-1)


def _fox_attention(live, pj, c_tiles, *, batch, seq, tq, tk, q_blk, k_blk, v_blk):
    assert tq == tk and tq % (2 * LANES) == 0
    nq = seq // tq
    nk = seq // tk
    npair = FOX_HEADS // 2
    grid_spec = pltpu.PrefetchScalarGridSpec(
        num_scalar_prefetch=1,
        grid=(batch, npair, nq),
        in_specs=[pl.BlockSpec((tq, PAIR), lambda b, p, i, f: (b * nq + i, q_blk + p)),
                  pl.BlockSpec((seq, PAIR), lambda b, p, i, f: (b, k_blk + p)),
                  pl.BlockSpec((seq, PAIR), lambda b, p, i, f: (b, v_blk + p)),
                  pl.BlockSpec((None, None, nk, 2, tk), lambda b, p, i, f: (b, p, 0, 0, 0))],
        out_specs=pl.BlockSpec((tq, PAIR), lambda b, p, i, f: (b * nq + i, p)),
        scratch_shapes=[pltpu.VMEM((2, tq, LANES), F32), pltpu.VMEM((2, tq, LANES), F32),
                        pltpu.VMEM((2, tq, PAIR), F32)])
    return pl.pallas_call(
        functools.partial(_fox_kernel, tq=tq, tk=tk, nq=nq),
        out_shape=jax.ShapeDtypeStruct((batch * seq, FOX_HEADS * HEAD_DIM), BF16),
        grid_spec=grid_spec,
        compiler_params=_cparams("parallel", "parallel", "arbitrary"),
        name="fox_attention",
    )(live, pj, pj, pj, c_tiles)


def _gelu_tanh(x):
    return 0.5 * x * (1.0 + jnp.tanh(0.7978845608028654 * (x + 0.044715 * x * x * x)))


def _compress_kernel(x_ref, pe_ref, w1pos_ref, w1dup_ref, w2sel_ref, o_ref, nrm_ref):
    nch = o_ref.shape[1]
    a = b = None
    for j in range(CMP_STRIDE):
        xj = x_ref[pl.ds(j, nch, stride=CMP_STRIDE), :].astype(BF16)
        pa, pb = _dot(xj, w1pos_ref[j]), _dot(xj, w1pos_ref[CMP_STRIDE + j])
        a, b = (pa, pb) if a is None else (a + pa, b + pb)
    pe_term = _dot(pe_ref[...], w1dup_ref[...])[0:1, :]
    h = a + pltpu.roll(b, nch - 1, axis=0) + pe_term
    y = _gelu_tanh(h).astype(BF16)
    for g in range(NSA_GROUPS):
        o = _dot(y, w2sel_ref[g]).astype(o_ref.dtype)
        o_ref[g] = o
        of = o.astype(F32)
        sq = 0.5 * jnp.sum(of * of, axis=-1, keepdims=True)
        nrm_ref[g] = jnp.broadcast_to(jnp.max(sq, axis=0, keepdims=True), nrm_ref.shape[1:])


def _compress(pc, cmp_pe, cmp_w1, cmp_w2, *, batch, seq):
    nch = seq // CMP_STRIDE
    dh = HEAD_DIM
    zero = jnp.zeros((2, CMP_BLOCK, dh, dh), F32)
    w1r = cmp_w1.reshape(2, CMP_BLOCK, dh, dh)
    w1pos = jnp.concatenate([jnp.concatenate([w1r, zero], axis=-1),
                             jnp.concatenate([zero, w1r], axis=-1)], axis=-2).astype(BF16)
    w1dup = jnp.concatenate([cmp_w1, cmp_w1], axis=-1).astype(BF16)
    w2dup = jnp.concatenate([cmp_w2, cmp_w2], axis=-1)
    zero2 = jnp.zeros_like(w2dup)
    w2sel = jnp.stack([jnp.concatenate([w2dup, zero2], axis=1),
                       jnp.concatenate([zero2, w2dup], axis=1)], axis=1).astype(BF16)
    pe8 = jnp.broadcast_to(cmp_pe.reshape(2, 1, CMP_BLOCK * dh), (2, 8, CMP_BLOCK * dh)).astype(BF16)
    return pl.pallas_call(
        _compress_kernel,
        out_shape=(jax.ShapeDtypeStruct((batch, 2, NSA_GROUPS, nch, PAIR), BF16),
                   jax.ShapeDtypeStruct((batch, 2, NSA_GROUPS, 8, LANES), F32)),
        grid=(batch, 2),
        in_specs=[pl.BlockSpec((seq, PAIR), lambda b, kv: (b, kv)),
                  pl.BlockSpec((None, 8, CMP_BLOCK * dh), lambda b, kv: (kv, 0, 0)),
                  pl.BlockSpec((None, CMP_BLOCK, PAIR, PAIR), lambda b, kv: (kv, 0, 0, 0)),
                  pl.BlockSpec((None, CMP_BLOCK * dh, PAIR), lambda b, kv: (kv, 0, 0)),
                  pl.BlockSpec((None, NSA_GROUPS, PAIR, PAIR), lambda b, kv: (kv, 0, 0, 0))],
        out_specs=(pl.BlockSpec((None, None, NSA_GROUPS, nch, PAIR), lambda b, kv: (b, kv, 0, 0, 0)),
                   pl.BlockSpec((None, None, NSA_GROUPS, 8, LANES), lambda b, kv: (b, kv, 0, 0, 0))),
        compiler_params=_cparams("parallel", "parallel"),
        name="nsa_compress",
    )(pc, pe8, w1pos, w1dup, w2sel)


def _group_slope(g, r):
    return jnp.where(g == 0, ALIBI_SLOPES_LOG2[r], ALIBI_SLOPES_LOG2[NSA_GROUP_SIZE + r]).astype(F32)


def _split3(x):
    hi = x.astype(BF16)
    r1 = x - hi.astype(F32)
    mid = r1.astype(BF16)
    lo = (r1 - mid.astype(F32)).astype(BF16)
    return hi, mid, lo


def _cmp_select_kernel(first_ref, q_ref, kc_ref, vc_ref, pool_ref, tmap_ref, oc_ref, sel_ref, flag_ref, *,
                       tq, n_sel, nseg, nq):
    b = pl.program_id(0)
    g = pl.program_id(1)
    i = pl.program_id(2)
    chunk = kc_ref.shape[0] // nseg
    first = first_ref[(b * NSA_GROUPS + g) * nq + i]
    n_live = (i * nseg) // nq + 1 - first
    for k in range(1, nseg + 1):
        @pl.when(n_live == k)
        def _(k=k):
            _cmp_select_body(q_ref, kc_ref, vc_ref, pool_ref, tmap_ref, oc_ref, sel_ref, flag_ref,
                             tq=tq, n_sel=n_sel, start=pl.multiple_of(first * chunk, chunk), nch=k * chunk)


def _cmp_select_body(q_ref, kc_ref, vc_ref, pool_ref, tmap_ref, oc_ref, sel_ref, flag_ref, *,
                     tq, n_sel, start, nch):
    g = pl.program_id(1)
    t0 = pl.program_id(2) * tq
    ns = pool_ref.shape[1]
    low = _lane_is_low()
    t = t0 + lax.broadcasted_iota(jnp.int32, (tq, 1), 0)
    cmp_end = (start + lax.broadcasted_iota(jnp.int32, (1, nch), 1)) * CMP_STRIDE + (CMP_BLOCK - 1)
    rel_end = (cmp_end - t0).astype(F32)
    mask_bias = jnp.where(t >= cmp_end, 0.0, NEG_INF)
    row_valid = jnp.where(t >= CMP_BLOCK - 1, 1.0, 0.0)
    kc = kc_ref[pl.ds(start, nch), :]
    vc = vc_ref[pl.ds(start, nch), :]
    imp = jnp.zeros((tq, nch), F32)
    for pair in range(NSA_GROUP_SIZE // 2):
        qh = _split_pair(q_ref[:, pair * PAIR:(pair + 1) * PAIR])
        outs = []
        for h in range(2):
            s = _dot_nt(qh[h], kc) + _group_slope(g, 2 * pair + h) * rel_end + mask_bias
            e = jnp.exp2(s - jnp.max(s, axis=-1, keepdims=True))
            p = e * (row_valid / jnp.maximum(jnp.sum(e, axis=-1, keepdims=True), 1e-30))
            outs.append(_dot(p.astype(BF16), vc))
            imp = imp + p
        oc_ref[:, pair * PAIR:(pair + 1) * PAIR] = jnp.where(low, outs[0], outs[1])

    pool = pool_ref[pl.ds(start, nch), :]
    hi, mid, lo = _split3(imp)
    p_slc = _dot(hi, pool) + _dot(mid, pool) + _dot(lo, pool)

    blk = lax.broadcasted_iota(jnp.int32, (1, ns), 1)
    cur = t // SLC_BLOCK
    forced = (blk == 0) | (blk == cur) | (blk == cur - 1)
    val = jnp.where(forced, REMOVED, jnp.where(blk > cur, NEG_INF, p_slc))
    sel = jnp.where(forced, 1.0, 0.0)
    blk_f = blk.astype(F32)
    for _ in range(n_sel - 3):
        mx = jnp.max(val, axis=-1, keepdims=True)
        pick = jnp.min(jnp.where(val == mx, blk_f, float(ns)), axis=-1, keepdims=True)
        hit = blk_f == pick
        sel = jnp.where(hit, 1.0, sel)
        val = jnp.where(hit, REMOVED, val)
    sel_ref[...] = sel
    any_sel = jnp.broadcast_to(jnp.max(sel, axis=0, keepdims=True), (8, ns)).astype(BF16)
    flag_ref[...] = (_dot(any_sel, tmap_ref[...]) > 0.5).astype(jnp.int32)


def _cmp_select(first_chunk, pj, cmp_kv, pool, tmap, *, batch, seq, tq, q_blk, n_sel, nseg):
    nq = seq // tq
    nch = cmp_kv.shape[3]
    ns = pool.shape[1]
    gw = NSA_GROUP_SIZE * HEAD_DIM
    grid_spec = pltpu.PrefetchScalarGridSpec(
        num_scalar_prefetch=1,
        grid=(batch, NSA_GROUPS, nq),
        in_specs=[pl.BlockSpec((tq, gw), lambda b, g, i, f: (b * nq + i, q_blk + g)),
                  pl.BlockSpec((None, None, None, nch, PAIR), lambda b, g, i, f: (b, 0, g, 0, 0)),
                  pl.BlockSpec((None, None, None, nch, PAIR), lambda b, g, i, f: (b, 1, g, 0, 0)),
                  pl.BlockSpec((nch, ns), lambda b, g, i, f: (0, 0)),
                  pl.BlockSpec((ns, LANES), lambda b, g, i, f: (0, 0))],
        out_specs=(pl.BlockSpec((tq, gw), lambda b, g, i, f: (b * nq + i, g)),
                   pl.BlockSpec((None, None, tq, ns), lambda b, g, i, f: (b, g, i, 0)),
                   pl.BlockSpec((None, None, None, 8, LANES), lambda b, g, i, f: (b, g, i, 0, 0))))
    return pl.pallas_call(
        functools.partial(_cmp_select_kernel, tq=tq, n_sel=n_sel, nseg=nseg, nq=nq),
        out_shape=(jax.ShapeDtypeStruct((batch * seq, NSA_HEADS * HEAD_DIM), F32),
                   jax.ShapeDtypeStruct((batch, NSA_GROUPS, seq, ns), F32),
                   jax.ShapeDtypeStruct((batch, NSA_GROUPS, nq, 8, LANES), jnp.int32)),
        grid_spec=grid_spec,
        compiler_params=_cparams("parallel", "parallel", "parallel"),
        name="nsa_cmp_select",
    )(first_chunk, pj, cmp_kv, cmp_kv, pool, tmap)


def _alibi_dead(q_sq, k_sq, dist, group_slope):
    bound = 2.0 * NORM_SLACK * jnp.sqrt(q_sq * k_sq)
    return group_slope * dist >= UNDERFLOW_BITS + bound


def _nsa_kernel(flags_ref, q_ref, ks_ref, vs_ref, kwp_ref, vwp_ref, kwd_ref, vwd_ref, sel_ref, oc_ref, gate_ref,
                o_ref, m_ref, acc_ref, *, tile, nq):
    b = pl.program_id(0)
    g = pl.program_id(1)
    i = pl.program_id(2)
    nh = NSA_GROUP_SIZE
    ns = sel_ref.shape[1]
    bpt = tile // SLC_BLOCK
    low = _lane_is_low()
    row = lax.broadcasted_iota(jnp.int32, (tile, tile), 0)
    col = lax.broadcasted_iota(jnp.int32, (tile, tile), 1)
    relpos = lax.broadcasted_iota(jnp.int32, (1, tile), 1).astype(F32)
    sel = sel_ref[...].astype(BF16)
    blk_of_key = lax.broadcasted_iota(jnp.int32, (ns, tile), 1) // SLC_BLOCK
    blk_row = lax.broadcasted_iota(jnp.int32, (ns, tile), 0)
    qh = []
    for pair in range(nh // 2):
        qh.extend(_split_pair(q_ref[:, pair * PAIR:(pair + 1) * PAIR]))
    slopes = [_group_slope(g, r) for r in range(nh)]
    _flash_reset(m_ref, acc_ref)

    def selected(j):
        expand = jnp.where(blk_row == blk_of_key + j * bpt, 1.0, 0.0).astype(BF16)
        return _dot(sel, expand) > 0.5

    def attend(slot, k_keys, v_keys, rel, keep, r0=0, nr=tile):
        v_aug = _with_ones(v_keys)
        bias = jnp.where(keep, 0.0, NEG_INF)
        rows = pl.ds(r0, nr)
        for r in range(nh):
            s = _dot_nt(qh[r][r0:r0 + nr], k_keys) + slopes[r] * rel + bias
            _flash_step(s, v_aug, m_ref.at[slot + r, rows], acc_ref.at[slot + r, rows])

    def sel_step(j, keep):
        ks = pl.multiple_of(j * tile, tile)
        attend(0, ks_ref[pl.ds(ks, tile), :], vs_ref[pl.ds(ks, tile), :],
               relpos + ((j - i) * tile).astype(F32), keep)

    sel_step(i, selected(i) & (col <= row))

    def sel_body(j, carry):
        @pl.when(flags_ref[((b * NSA_GROUPS + g) * nq + i) * nq + j] != 0)
        def _():
            sel_step(j, selected(j))
        return carry

    lax.fori_loop(0, i, sel_body, 0)

    half = tile // 2
    span = tile + half
    k_win = jnp.concatenate([kwp_ref[...], kwd_ref[...]], axis=0)
    v_win = jnp.concatenate([vwp_ref[...], vwd_ref[...]], axis=0)
    for r0 in (0, half):
        row2 = lax.broadcasted_iota(jnp.int32, (half, span), 0) + r0
        col2 = lax.broadcasted_iota(jnp.int32, (half, span), 1) + r0
        in_window = ((col2 >= tile) & (col2 - tile <= row2)) | ((col2 < tile) & (col2 > row2) & (i > 0))
        rel2 = (lax.broadcasted_iota(jnp.int32, (1, span), 1) + (r0 - tile)).astype(F32)
        attend(nh, k_win[r0:r0 + span], v_win[r0:r0 + span], rel2, in_window, r0, half)

    gates = jax.nn.sigmoid(gate_ref[...])

    def gate_col(r, branch):
        c0 = FOX_HEADS + 3 * r + branch
        c1 = c0 + 3 * nh
        return jnp.where(g == 0, gates[:, c0:c0 + 1], gates[:, c1:c1 + 1])

    for pair in range(nh // 2):
        o_c = oc_ref[:, pair * PAIR:(pair + 1) * PAIR]
        outs = []
        for h in range(2):
            r = 2 * pair + h
            o_s = _flash_out(acc_ref[r])
            o_w = _flash_out(acc_ref[nh + r])
            outs.append(gate_col(r, 0) * o_c + gate_col(r, 1) * o_s + gate_col(r, 2) * o_w)
        o_ref[:, pair * PAIR:(pair + 1) * PAIR] = jnp.where(low, outs[0], outs[1]).astype(o_ref.dtype)


def _nsa_attention(flags, pj, sel, o_c, small, *, batch, seq, tile, q_blk, ks_blk, vs_blk, kw_blk, vw_blk):
    nq = seq // tile
    ns = sel.shape[3]
    gw = NSA_GROUP_SIZE * HEAD_DIM
    slots = 2 * NSA_GROUP_SIZE
    cur = lambda b, g, i, f, base: (b * nq + i, base + g)
    prev = lambda b, g, i, f, base: (b * nq + jnp.maximum(i - 1, 0), base + g)
    grid_spec = pltpu.PrefetchScalarGridSpec(
        num_scalar_prefetch=1,
        grid=(batch, NSA_GROUPS, nq),
        in_specs=[pl.BlockSpec((tile, gw), functools.partial(cur, base=q_blk)),
                  pl.BlockSpec((seq, PAIR), lambda b, g, i, f: (b, ks_blk + g)),
                  pl.BlockSpec((seq, PAIR), lambda b, g, i, f: (b, vs_blk + g)),
                  pl.BlockSpec((tile, PAIR), functools.partial(prev, base=kw_blk)),
                  pl.BlockSpec((tile, PAIR), functools.partial(prev, base=vw_blk)),
                  pl.BlockSpec((tile, PAIR), functools.partial(cur, base=kw_blk)),
                  pl.BlockSpec((tile, PAIR), functools.partial(cur, base=vw_blk)),
                  pl.BlockSpec((None, None, tile, ns), lambda b, g, i, f: (b, g, i, 0)),
                  pl.BlockSpec((tile, gw), functools.partial(cur, base=0)),
                  pl.BlockSpec((tile, small.shape[1]), lambda b, g, i, f: (b * nq + i, 0))],
        out_specs=pl.BlockSpec((tile, gw), functools.partial(cur, base=0)),
        scratch_shapes=[pltpu.VMEM((slots, tile, LANES), F32), pltpu.VMEM((slots, tile, 2 * PAIR), F32)])
    return pl.pallas_call(
        functools.partial(_nsa_kernel, tile=tile, nq=nq),
        out_shape=jax.ShapeDtypeStruct((batch * seq, NSA_HEADS * HEAD_DIM), BF16),
        grid_spec=grid_spec,
        compiler_params=_cparams("parallel", "parallel", "arbitrary"),
        name="nsa_attention",
    )(flags, pj, pj, pj, pj, pj, pj, pj, sel, o_c, small)


def _ffn_kernel(h_ref, gin_ref, gout_ref, wg_ref, wu_ref, wd_ref, *rest, chunk, mixer, tm, tiles_per_seq):
    h = h_ref[...]
    if mixer == "attn":
        a_ref, b_ref, wa_ref, wb_ref, gmix_ref, o_ref, acc_ref = rest
        h = h + _rms(_dot(a_ref[...], wa_ref[...]) + _dot(b_ref[...], wb_ref[...]), gmix_ref[...])
    else:
        halo_ref, pin_ref, pout_ref, pw_ref, sc_ref, o_ref, acc_ref, ext_ref = rest
        h = _pool_mix(h, halo_ref, pin_ref, pout_ref, pw_ref, sc_ref, ext_ref, tm=tm, tiles_per_seq=tiles_per_seq)
    ub = _rms(h, gin_ref[...]).astype(BF16)
    hidden = wg_ref.shape[1]
    for idx, c in enumerate(range(0, hidden, chunk)):
        gate = _dot(ub, wg_ref[:, c:c + chunk])
        up = _dot(ub, wu_ref[:, c:c + chunk])
        act = (gate * jax.nn.sigmoid(gate) * up).astype(BF16)
        part = _dot(act, wd_ref[c:c + chunk, :])
        if idx == 0:
            acc_ref[...] = part
        else:
            acc_ref[...] += part
    o_ref[...] = h + _rms(acc_ref[...], gout_ref[...])


def _layer_tail(h2, g_in, g_out, wg, wu, wd, *, seq, tm, chunk, mixer, mix):
    n, d = h2.shape
    const = lambda i: (0, 0)
    rows = lambda i: (i, 0)
    scratch = [pltpu.VMEM((tm, d), F32)]
    if mixer == "attn":
        a, b, wa, wb, g_mix = mix
        mix_specs = [pl.BlockSpec((tm, a.shape[1]), rows), pl.BlockSpec((tm, b.shape[1]), rows),
                     pl.BlockSpec(wa.shape, const), pl.BlockSpec(wb.shape, const), pl.BlockSpec((1, d), const)]
    else:
        p_in, p_out, w_groups, scale = mix
        ratio = tm // POOL_HALO
        mix = (h2, p_in, p_out, w_groups, scale)
        mix_specs = [pl.BlockSpec((POOL_HALO, d), lambda i: (jnp.maximum(i * ratio - 1, 0), 0)),
                     pl.BlockSpec((1, d), const), pl.BlockSpec((1, d), const),
                     pl.BlockSpec(w_groups.shape, lambda i: (0, 0, 0)), pl.BlockSpec((1, d), const)]
        scratch.append(pltpu.VMEM((tm + POOL_HALO, d), F32))
    return pl.pallas_call(
        functools.partial(_ffn_kernel, chunk=chunk, mixer=mixer, tm=tm, tiles_per_seq=seq // tm),
        out_shape=jax.ShapeDtypeStruct((n, d), F32),
        grid=(n // tm,),
        in_specs=[pl.BlockSpec((tm, d), rows),
                  pl.BlockSpec((1, d), const),
                  pl.BlockSpec((1, d), const),
                  pl.BlockSpec(wg.shape, const),
                  pl.BlockSpec(wu.shape, const),
                  pl.BlockSpec(wd.shape, const)] + mix_specs,
        out_specs=pl.BlockSpec((tm, d), rows),
        scratch_shapes=scratch,
        compiler_params=_cparams("parallel"),
        name=f"{mixer}_tail_swiglu_ffn",
    )(h2, g_in, g_out, wg, wu, wd, *mix)


def _pool_mix(h, halo_ref, gin_ref, gout_ref, w_ref, sc_ref, ext_ref, *, tm, tiles_per_seq):
    halo = POOL_HALO
    i = pl.program_id(0)
    first = (i % tiles_per_seq) == 0
    u = _rms(h, gin_ref[...])
    uh = _rms(halo_ref[...], gin_ref[...])
    ext_ref[0:halo, :] = jnp.where(first, 0.0, uh)
    ext_ref[halo:, :] = u
    t = (i % tiles_per_seq) * tm + lax.broadcasted_iota(jnp.int32, (tm, 1), 0)
    group = h.shape[1] // len(POOL_WINDOWS)
    ys = []
    for gi, w in enumerate(POOL_WINDOWS):
        cols = slice(gi * group, (gi + 1) * group)
        run = ext_ref[:, cols]
        span = 1
        while span < w:
            run = run + pltpu.roll(run, span, axis=0)
            span *= 2
        total = run[halo:, :]
        count = jnp.minimum(t + 1, w).astype(F32)
        pooled = total / count - u[:, cols]
        ys.append(_dot(pooled.astype(BF16), w_ref[gi]))
    y = jnp.concatenate(ys, axis=-1) * sc_ref[...]
    return h + _rms(y, gout_ref[...])


def _pack_in_weights(w_in):
    d = w_in.shape[0]
    hw = FOX_HEADS * HEAD_DIM
    kv = NSA_GROUPS * HEAD_DIM
    sizes = (hw, hw, hw, FOX_HEADS, NSA_HEADS * HEAD_DIM, kv, kv, kv, kv, kv, kv, 3 * NSA_HEADS)
    offs = np.concatenate([[0], np.cumsum(sizes)])
    fq, fk, fv, ff, nq, kc, vc, ks, vs, kw, vw, ng = [w_in[:, offs[j]:offs[j + 1]] for j in range(len(sizes))]
    scale = HEAD_DIM ** -0.5 * LOG2E

    def dup(w):
        w = w.reshape(d, NSA_GROUPS, 1, HEAD_DIM)
        return jnp.broadcast_to(w, (d, NSA_GROUPS, 2, HEAD_DIM)).reshape(d, NSA_GROUPS * PAIR)

    w_main = jnp.concatenate([fq * scale, fk, fv, nq * scale, dup(ks), dup(vs), dup(kw), dup(vw)], axis=1)
    w_cmp = jnp.concatenate([kc, vc], axis=1)
    pad = jnp.zeros((d, LANES - FOX_HEADS - 3 * NSA_HEADS), F32)
    w_small = jnp.concatenate([ff, ng, pad], axis=1)
    w_small_hi = w_small.astype(BF16)
    w_small_lo = (w_small - w_small_hi.astype(F32)).astype(BF16)
    return w_main.astype(BF16), w_cmp.astype(BF16), jnp.concatenate([w_small_hi, w_small_lo], axis=1)


def _mixer_fox_nsa(h2, g_in, g_out, w_in, b_f, cmp_pe, cmp_w1, cmp_w2, w_out, *, batch, seq):
    tile = WINDOW
    assert seq % tile == 0 and seq // tile <= LANES
    w_main, w_cmp, w_small = _pack_in_weights(w_in)
    pj, pc, small, nrm = _inproj(h2, g_in, w_main, w_cmp, w_small, tm=tile)
    npair = FOX_HEADS // 2
    q_blk, k_blk, v_blk, nq_blk = 0, npair, 2 * npair, 3 * npair
    ks_blk = nq_blk + NSA_HEADS // 2
    vs_blk, kw_blk, vw_blk = ks_blk + NSA_GROUPS, ks_blk + 2 * NSA_GROUPS, ks_blk + 3 * NSA_GROUPS

    c = _forget_cumsum(small, b_f, batch=batch, seq=seq, tc=tile)
    nq = seq // tile
    fox_tq, fox_tk = min(FOX_TQ, seq), min(FOX_TK, seq)
    c_tiles = c.reshape(batch, npair, 2, seq // fox_tk, fox_tk).transpose(0, 1, 3, 2, 4)
    live = _fox_live_tiles(c, nrm, batch=batch, seq=seq, tq=fox_tq, tk=fox_tk, tm=tile)
    o_fox = _fox_attention(live, pj, c_tiles, batch=batch, seq=seq, tq=fox_tq, tk=fox_tk,
                           q_blk=q_blk, k_blk=k_blk, v_blk=v_blk)

    nch = seq // CMP_STRIDE
    ns = seq // SLC_BLOCK
    n_sel = min(N_SELECT, ns)
    kv = NSA_GROUPS * HEAD_DIM
    cmp_kv, cmp_nrm = _compress(pc, cmp_pe, cmp_w1, cmp_w2, batch=batch, seq=seq)

    group_slope = jnp.asarray([min(ALIBI_SLOPES_LOG2[g * NSA_GROUP_SIZE:(g + 1) * NSA_GROUP_SIZE])
                               for g in range(NSA_GROUPS)], F32)
    qn = nrm[:, NRM_NSA_Q, :NSA_HEADS].reshape(batch, nq, NSA_GROUPS, NSA_GROUP_SIZE).max(axis=3)
    ksn = nrm[:, NRM_NSA_KS, 0:2 * NSA_GROUPS:2].reshape(batch, nq, NSA_GROUPS).max(axis=1)
    kcn = cmp_nrm[:, 0, :, 0, 0]
    tiles = jnp.arange(nq)
    nseg = CMP_SEGMENTS if nq % CMP_SEGMENTS == 0 else 1
    chunk = nch // nseg
    last_end = (jnp.arange(nseg) + 1) * chunk * CMP_STRIDE + (CMP_BLOCK - 1 - CMP_STRIDE)
    dist_c = (tiles[:, None] * tile - last_end[None, :] - (CMP_STRIDE - 1)).astype(F32)
    dead_c = _alibi_dead(qn[..., None], kcn[:, None, :, None], dist_c[None, :, None, :],
                         group_slope[None, None, :, None])
    first_chunk = jnp.minimum(dead_c.sum(axis=3), (tiles * nseg // nq)[None, :, None])
    first_chunk = first_chunk.transpose(0, 2, 1).astype(jnp.int32).reshape(-1)
    dist_s = ((tiles[:, None] - tiles[None, :] - 1) * tile + 1).astype(F32)
    dead_s = _alibi_dead(qn[..., None], ksn[:, None, :, None], dist_s[None, :, None, :],
                         group_slope[None, None, :, None])
    live_s = jnp.logical_not(dead_s).transpose(0, 2, 1, 3).astype(jnp.int32)

    ratio = SLC_BLOCK // CMP_STRIDE
    r = CMP_BLOCK // CMP_STRIDE
    n_idx = np.arange(nch)[:, None]
    b_idx = np.arange(ns)[None, :]
    pool_np = ((n_idx >= ratio * b_idx - (r - 1)) & (n_idx <= ratio * b_idx + ratio - 1)
               & (n_idx < nch - r + 1)).astype(np.float32)
    pool = jnp.asarray(pool_np, dtype=BF16)
    tmap_np = (np.arange(ns)[:, None] // (tile // SLC_BLOCK) == np.arange(LANES)[None, :]).astype(np.float32)
    o_c, sel, flag_blocks = _cmp_select(first_chunk, pj, cmp_kv, pool, jnp.asarray(tmap_np, dtype=BF16),
                                        batch=batch, seq=seq, tq=tile, q_blk=nq_blk // 2, n_sel=n_sel, nseg=nseg)
    flags = (flag_blocks[:, :, :, 0, :nq] * live_s).reshape(-1)
    o_nsa = _nsa_attention(flags, pj, sel, o_c, small, batch=batch, seq=seq, tile=tile,
                           q_blk=nq_blk // 2, ks_blk=ks_blk, vs_blk=vs_blk, kw_blk=kw_blk, vw_blk=vw_blk)

    hw = FOX_HEADS * HEAD_DIM
    return o_fox, o_nsa, w_out[:hw].astype(BF16), w_out[hw:].astype(BF16), g_out


def kernel(x, norm_g, attn_w_in, fox_b_f, nsa_cmp_pe, nsa_cmp_w1, nsa_cmp_w2, attn_w_out,
           pool_w, pool_scale, ffn_w_gate, ffn_w_up, ffn_w_down):
    batch, seq, d = x.shape
    depth = norm_g.shape[0]
    tm = 512 if seq % 512 == 0 else seq
    h = x.reshape(batch * seq, d)
    for layer in range(depth):
        g = norm_g[layer].reshape(4, 1, d)
        i = layer // 2
        if layer % 2 == 0:
            mixer = "attn"
            mix = _mixer_fox_nsa(h, g[0], g[1], attn_w_in[i], fox_b_f[i], nsa_cmp_pe[i], nsa_cmp_w1[i],
                                 nsa_cmp_w2[i], attn_w_out[i], batch=batch, seq=seq)
        else:
            mixer = "pool"
            mix = (g[0], g[1], pool_w[i].astype(BF16), pool_scale[i].reshape(1, d))
        h = _layer_tail(h, g[2], g[3], ffn_w_gate[layer].astype(BF16), ffn_w_up[layer].astype(BF16),
                        ffn_w_down[layer].astype(BF16), seq=seq, tm=tm, chunk=256, mixer=mixer, mix=mix)
    return h.reshape(batch, seq, d)
```

```python
import functools

import numpy as np
import jax
import jax.numpy as jnp
from jax import lax
from jax.experimental import pallas as pl
from jax.experimental.pallas import tpu as pltpu

F32 = jnp.float32
BF16 = jnp.bfloat16

HEAD_DIM = 64
FOX_HEADS = 8
NSA_HEADS = 8
NSA_GROUPS = 2
NSA_GROUP_SIZE = NSA_HEADS // NSA_GROUPS
CMP_BLOCK = 32
CMP_STRIDE = 16
SLC_BLOCK = 64
N_SELECT = 16
WINDOW = 512
POOL_WINDOWS = (2, 4, 8, 16)
POOL_HALO = max(POOL_WINDOWS)
RMS_EPS = 1e-6
NEG_INF = -1e30
BIG = 1e30
REMOVED = -3e38
LANES = 128
PAIR = 2 * HEAD_DIM
VMEM_LIMIT = 56 * 1024 * 1024
CMP_SEGMENTS = 4
UNDERFLOW_BITS = 160.0
NORM_SLACK = 1.02
NORM_CHUNKS = (0, 1, 3, 4)
NRM_FOX_Q, NRM_FOX_K, NRM_NSA_Q, NRM_NSA_KS = range(4)
FOX_TQ, FOX_TK = 1024, 1024

LOG2E = 1.4426950408889634
ALIBI_SLOPES_LOG2 = tuple(float(2.0 ** (-8.0 * (i + 1.0) / NSA_HEADS)) * LOG2E for i in range(NSA_HEADS))


def _cparams(*sem):
    return pltpu.CompilerParams(dimension_semantics=sem, vmem_limit_bytes=VMEM_LIMIT)


def _rms(x, g):
    return x * lax.rsqrt(jnp.mean(x * x, axis=-1, keepdims=True) + RMS_EPS) * g


def _dot(a, b):
    return jnp.dot(a, b, preferred_element_type=F32)


def _dot_nt(a, b):
    return lax.dot_general(a, b, (((1,), (1,)), ((), ())), preferred_element_type=F32)


def _inproj_kernel(x_ref, g_ref, w_ref, wc_ref, ws_ref, hmap_ref, o_ref, oc_ref, os_ref, nrm_ref, *, fox_width):
    u = _rms(x_ref[...], g_ref[...])
    ub = u.astype(BF16)
    norms = []
    for c in range(0, o_ref.shape[1], fox_width):
        r = _dot(ub, w_ref[:, c:c + fox_width]).astype(BF16)
        o_ref[:, c:c + fox_width] = r
        if c // fox_width in NORM_CHUNKS:
            rf = r.astype(F32)
            sq = _dot((rf * rf).astype(BF16), hmap_ref[...])
            norms.append(jnp.max(sq, axis=0, keepdims=True))
    oc_ref[...] = _dot(ub, wc_ref[...])
    width = os_ref.shape[1]
    u_lo = (u - ub.astype(F32)).astype(BF16)
    both = _dot(ub, ws_ref[...])
    os_ref[...] = both[:, :width] + both[:, width:] + _dot(u_lo, ws_ref[:, :width])
    row = lax.broadcasted_iota(jnp.int32, nrm_ref.shape, 0)
    out = jnp.zeros(nrm_ref.shape, F32)
    for k, nk in enumerate(norms):
        out = jnp.where(row == k, nk, out)
    nrm_ref[...] = out


def _inproj(x2, g, w_main, w_cmp, w_small, *, tm):
    n, d = x2.shape
    cm, cc, cs = w_main.shape[1], w_cmp.shape[1], w_small.shape[1] // 2
    fox_width = FOX_HEADS * HEAD_DIM
    hmap = jnp.asarray((np.arange(fox_width)[:, None] // HEAD_DIM == np.arange(LANES)[None, :]).astype(np.float32),
                       dtype=BF16)
    const = lambda i: (0, 0)
    return pl.pallas_call(
        functools.partial(_inproj_kernel, fox_width=fox_width),
        out_shape=(jax.ShapeDtypeStruct((n, cm), BF16),
                   jax.ShapeDtypeStruct((n, cc), F32),
                   jax.ShapeDtypeStruct((n, cs), F32),
                   jax.ShapeDtypeStruct((n // tm, 8, LANES), F32)),
        grid=(n // tm,),
        in_specs=[pl.BlockSpec((tm, d), lambda i: (i, 0)),
                  pl.BlockSpec((1, d), const),
                  pl.BlockSpec((d, cm), const),
                  pl.BlockSpec((d, cc), const),
                  pl.BlockSpec((d, 2 * cs), const),
                  pl.BlockSpec((fox_width, LANES), const)],
        out_specs=(pl.BlockSpec((tm, cm), lambda i: (i, 0)),
                   pl.BlockSpec((tm, cc), lambda i: (i, 0)),
                   pl.BlockSpec((tm, cs), lambda i: (i, 0)),
                   pl.BlockSpec((None, 8, LANES), lambda i: (i, 0, 0))),
        compiler_params=_cparams("parallel"),
        name="inproj",
    )(x2, g, w_main, w_cmp, w_small, hmap)


def _forget_cumsum_kernel(sm_ref, bf_ref, tri_ref, o_ref, carry_ref):
    @pl.when(pl.program_id(1) == 0)
    def _():
        carry_ref[...] = jnp.zeros_like(carry_ref)

    zt = sm_ref[...].T
    z = zt[0:FOX_HEADS, :] + bf_ref[...]
    log_f = jnp.minimum(z, 0.0) - jnp.log1p(jnp.exp(-jnp.abs(z)))
    c = jnp.dot(log_f, tri_ref[...], preferred_element_type=F32,
                precision=lax.Precision.HIGHEST) + carry_ref[...]
    o_ref[...] = c * LOG2E
    carry_ref[...] = c[:, c.shape[1] - 1:]


def _forget_cumsum(small, b_f, *, batch, seq, tc):
    tri = jnp.asarray(np.triu(np.ones((tc, tc), np.float32)))
    nchunk = seq // tc
    return pl.pallas_call(
        _forget_cumsum_kernel,
        out_shape=jax.ShapeDtypeStruct((batch, FOX_HEADS, seq), F32),
        grid=(batch, nchunk),
        in_specs=[pl.BlockSpec((tc, small.shape[1]), lambda b, j: (b * nchunk + j, 0)),
                  pl.BlockSpec((FOX_HEADS, 1), lambda b, j: (0, 0)),
                  pl.BlockSpec((tc, tc), lambda b, j: (0, 0))],
        out_specs=pl.BlockSpec((None, FOX_HEADS, tc), lambda b, j: (b, 0, j)),
        scratch_shapes=[pltpu.VMEM((FOX_HEADS, 1), F32)],
        compiler_params=_cparams("parallel", "arbitrary"),
        name="forget_cumsum",
    )(small, b_f.reshape(FOX_HEADS, 1), tri)


def _lane_is_low():
    return lax.broadcasted_iota(jnp.int32, (1, PAIR), 1) < HEAD_DIM


def _with_ones(v_tile):
    return jnp.concatenate([v_tile, jnp.ones_like(v_tile)], axis=1)


def _flash_step(s, v, m_ref, acc_ref, l_ref=None):
    m_old = m_ref[...]
    m_new = jnp.maximum(m_old, jnp.max(s, axis=-1, keepdims=True))
    alpha = jnp.exp2(m_old - m_new)
    p = jnp.exp2(s - jnp.concatenate([m_new] * (s.shape[1] // LANES), axis=1))
    if l_ref is None:
        alpha_acc = jnp.concatenate([alpha, alpha], axis=1)
    else:
        alpha_acc = alpha
        l_ref[...] = alpha * l_ref[...] + jnp.sum(p, axis=-1, keepdims=True)
    acc_ref[...] = alpha_acc * acc_ref[...] + _dot(p.astype(BF16), v)
    m_ref[...] = m_new


def _flash_reset(m_ref, *sum_refs):
    m_ref[...] = jnp.full_like(m_ref, NEG_INF)
    for ref in sum_refs:
        ref[...] = jnp.zeros_like(ref)


def _flash_out(acc):
    return acc[:, :PAIR] / acc[:, PAIR:]


def _split_pair(q):
    low = _lane_is_low()
    zero = jnp.zeros_like(q)
    return jnp.where(low, q, zero), jnp.where(low, zero, q)


def _fox_kernel(live_ref, q_ref, k_ref, v_ref, c_ref, o_ref, m_ref, l_ref, acc_ref, *, tq, tk, nq):
    b = pl.program_id(0)
    pair = pl.program_id(1)
    i = pl.program_id(2)
    nk = c_ref.shape[0]
    qh = _split_pair(q_ref[...])
    _flash_reset(m_ref, l_ref, acc_ref)

    def head_step(h, j, mask):
        ks = pl.multiple_of(j * tk, tk)
        s = _dot_nt(qh[h], k_ref[pl.ds(ks, tk), :]) - c_ref[j, h:h + 1, :]
        if mask is not None:
            s = jnp.where(mask, s, NEG_INF)
        _flash_step(s, v_ref[pl.ds(ks, tk), :], m_ref.at[h], acc_ref.at[h], l_ref.at[h])

    def body(j, carry):
        live = [live_ref[(((b * FOX_HEADS) + 2 * pair + h) * nq + i) * nk + j] != 0 for h in range(2)]

        @pl.when(live[0] & live[1])
        def _():
            head_step(0, j, None)
            head_step(1, j, None)

        for h in range(2):
            @pl.when(live[h] & jnp.logical_not(live[1 - h]))
            def _(h=h):
                head_step(h, j, None)
        return carry

    lax.fori_loop(0, i, body, 0)

    def diag_part(h, r0, nr, nkeys):
        ks = pl.multiple_of(i * tq, tq)
        rows = pl.ds(r0, nr)
        row = lax.broadcasted_iota(jnp.int32, (nr, nkeys), 0) + r0
        col = lax.broadcasted_iota(jnp.int32, (nr, nkeys), 1)
        s = _dot_nt(qh[h][r0:r0 + nr], k_ref[pl.ds(ks, nkeys), :]) - c_ref[i, h:h + 1, 0:nkeys]
        _flash_step(jnp.where(col <= row, s, NEG_INF), v_ref[pl.ds(ks, nkeys), :],
                    m_ref.at[h, rows], acc_ref.at[h, rows], l_ref.at[h, rows])

    half = tq // 2
    for h in range(2):
        diag_part(h, 0, half, half)
        diag_part(h, half, half, tq)
    o_ref[...] = jnp.where(_lane_is_low(), acc_ref[0] / l_ref[0], acc_ref[1] / l_ref[1]).astype(o_ref.dtype)


def _fox_live_tiles(c2, nrm, *, batch, seq, tq, tk, tm):
    nq, nk = seq // tq, seq // tk
    qn = nrm[:, 0, :FOX_HEADS].reshape(batch, nq, tq // tm, FOX_HEADS).max(axis=2)
    kn = nrm[:, 1, :FOX_HEADS].reshape(batch, seq // tm, FOX_HEADS).max(axis=1)
    bound = 2.0 * NORM_SLACK * jnp.sqrt(qn * kn[:, None, :])
    c_start = c2[:, :, ::tq]
    c_end = c2[:, :, tk - 1::tk]
    decay = c_end[:, :, None, :] - c_start[:, :, :, None]
    live = decay < UNDERFLOW_BITS + jnp.transpose(bound, (0, 2, 1))[..., None]
    return live.astype(jnp.int32).reshape(-1)


def _fox_attention(live, pj, c_tiles, *, batch, seq, tq, tk, q_blk, k_blk, v_blk):
    assert tq == tk and tq % (2 * LANES) == 0
    nq = seq // tq
    nk = seq // tk
    npair = FOX_HEADS // 2
    grid_spec = pltpu.PrefetchScalarGridSpec(
        num_scalar_prefetch=1,
        grid=(batch, npair, nq),
        in_specs=[pl.BlockSpec((tq, PAIR), lambda b, p, i, f: (b * nq + i, q_blk + p)),
                  pl.BlockSpec((seq, PAIR), lambda b, p, i, f: (b, k_blk + p)),
                  pl.BlockSpec((seq, PAIR), lambda b, p, i, f: (b, v_blk + p)),
                  pl.BlockSpec((None, None, nk, 2, tk), lambda b, p, i, f: (b, p, 0, 0, 0))],
        out_specs=pl.BlockSpec((tq, PAIR), lambda b, p, i, f: (b * nq + i, p)),
        scratch_shapes=[pltpu.VMEM((2, tq, LANES), F32), pltpu.VMEM((2, tq, LANES), F32),
                        pltpu.VMEM((2, tq, PAIR), F32)])
    return pl.pallas_call(
        functools.partial(_fox_kernel, tq=tq, tk=tk, nq=nq),
        out_shape=jax.ShapeDtypeStruct((batch * seq, FOX_HEADS * HEAD_DIM), BF16),
        grid_spec=grid_spec,
        compiler_params=_cparams("parallel", "parallel", "arbitrary"),
        name="fox_attention",
    )(live, pj, pj, pj, c_tiles)


def _gelu_tanh(x):
    return 0.5 * x * (1.0 + jnp.tanh(0.7978845608028654 * (x + 0.044715 * x * x * x)))


def _compress_kernel(x_ref, pe_ref, w1pos_ref, w1dup_ref, w2sel_ref, o_ref, nrm_ref):
    nch = o_ref.shape[1]
    a = b = None
    for j in range(CMP_STRIDE):
        xj = x_ref[pl.ds(j, nch, stride=CMP_STRIDE), :].astype(BF16)
        pa, pb = _dot(xj, w1pos_ref[j]), _dot(xj, w1pos_ref[CMP_STRIDE + j])
        a, b = (pa, pb) if a is None else (a + pa, b + pb)
    pe_term = _dot(pe_ref[...], w1dup_ref[...])[0:1, :]
    h = a + pltpu.roll(b, nch - 1, axis=0) + pe_term
    y = _gelu_tanh(h).astype(BF16)
    for g in range(NSA_GROUPS):
        o = _dot(y, w2sel_ref[g]).astype(o_ref.dtype)
        o_ref[g] = o
        of = o.astype(F32)
        sq = 0.5 * jnp.sum(of * of, axis=-1, keepdims=True)
        nrm_ref[g] = jnp.broadcast_to(jnp.max(sq, axis=0, keepdims=True), nrm_ref.shape[1:])


def _compress(pc, cmp_pe, cmp_w1, cmp_w2, *, batch, seq):
    nch = seq // CMP_STRIDE
    dh = HEAD_DIM
    zero = jnp.zeros((2, CMP_BLOCK, dh, dh), F32)
    w1r = cmp_w1.reshape(2, CMP_BLOCK, dh, dh)
    w1pos = jnp.concatenate([jnp.concatenate([w1r, zero], axis=-1),
                             jnp.concatenate([zero, w1r], axis=-1)], axis=-2).astype(BF16)
    w1dup = jnp.concatenate([cmp_w1, cmp_w1], axis=-1).astype(BF16)
    w2dup = jnp.concatenate([cmp_w2, cmp_w2], axis=-1)
    zero2 = jnp.zeros_like(w2dup)
    w2sel = jnp.stack([jnp.concatenate([w2dup, zero2], axis=1),
                       jnp.concatenate([zero2, w2dup], axis=1)], axis=1).astype(BF16)
    pe8 = jnp.broadcast_to(cmp_pe.reshape(2, 1, CMP_BLOCK * dh), (2, 8, CMP_BLOCK * dh)).astype(BF16)
    return pl.pallas_call(
        _compress_kernel,
        out_shape=(jax.ShapeDtypeStruct((batch, 2, NSA_GROUPS, nch, PAIR), BF16),
                   jax.ShapeDtypeStruct((batch, 2, NSA_GROUPS, 8, LANES), F32)),
        grid=(batch, 2),
        in_specs=[pl.BlockSpec((seq, PAIR), lambda b, kv: (b, kv)),
                  pl.BlockSpec((None, 8, CMP_BLOCK * dh), lambda b, kv: (kv, 0, 0)),
                  pl.BlockSpec((None, CMP_BLOCK, PAIR, PAIR), lambda b, kv: (kv, 0, 0, 0)),
                  pl.BlockSpec((None, CMP_BLOCK * dh, PAIR), lambda b, kv: (kv, 0, 0)),
                  pl.BlockSpec((None, NSA_GROUPS, PAIR, PAIR), lambda b, kv: (kv, 0, 0, 0))],
        out_specs=(pl.BlockSpec((None, None, NSA_GROUPS, nch, PAIR), lambda b, kv: (b, kv, 0, 0, 0)),
                   pl.BlockSpec((None, None, NSA_GROUPS, 8, LANES), lambda b, kv: (b, kv, 0, 0, 0))),
        compiler_params=_cparams("parallel", "parallel"),
        name="nsa_compress",
    )(pc, pe8, w1pos, w1dup, w2sel)


def _group_slope(g, r):
    return jnp.where(g == 0, ALIBI_SLOPES_LOG2[r], ALIBI_SLOPES_LOG2[NSA_GROUP_SIZE + r]).astype(F32)


def _split3(x):
    hi = x.astype(BF16)
    r1 = x - hi.astype(F32)
    mid = r1.astype(BF16)
    lo = (r1 - mid.astype(F32)).astype(BF16)
    return hi, mid, lo


def _cmp_select_kernel(first_ref, q_ref, kc_ref, vc_ref, pool_ref, tmap_ref, oc_ref, sel_ref, flag_ref, *,
                       tq, n_sel, nseg, nq):
    b = pl.program_id(0)
    g = pl.program_id(1)
    i = pl.program_id(2)
    chunk = kc_ref.shape[0] // nseg
    first = first_ref[(b * NSA_GROUPS + g) * nq + i]
    n_live = (i * nseg) // nq + 1 - first
    for k in range(1, nseg + 1):
        @pl.when(n_live == k)
        def _(k=k):
            _cmp_select_body(q_ref, kc_ref, vc_ref, pool_ref, tmap_ref, oc_ref, sel_ref, flag_ref,
                             tq=tq, n_sel=n_sel, start=pl.multiple_of(first * chunk, chunk), nch=k * chunk)


def _cmp_select_body(q_ref, kc_ref, vc_ref, pool_ref, tmap_ref, oc_ref, sel_ref, flag_ref, *,
                     tq, n_sel, start, nch):
    g = pl.program_id(1)
    t0 = pl.program_id(2) * tq
    ns = pool_ref.shape[1]
    low = _lane_is_low()
    t = t0 + lax.broadcasted_iota(jnp.int32, (tq, 1), 0)
    cmp_end = (start + lax.broadcasted_iota(jnp.int32, (1, nch), 1)) * CMP_STRIDE + (CMP_BLOCK - 1)
    rel_end = (cmp_end - t0).astype(F32)
    mask_bias = jnp.where(t >= cmp_end, 0.0, NEG_INF)
    row_valid = jnp.where(t >= CMP_BLOCK - 1, 1.0, 0.0)
    kc = kc_ref[pl.ds(start, nch), :]
    vc = vc_ref[pl.ds(start, nch), :]
    imp = jnp.zeros((tq, nch), F32)
    for pair in range(NSA_GROUP_SIZE // 2):
        qh = _split_pair(q_ref[:, pair * PAIR:(pair + 1) * PAIR])
        outs = []
        for h in range(2):
            s = _dot_nt(qh[h], kc) + _group_slope(g, 2 * pair + h) * rel_end + mask_bias
            e = jnp.exp2(s - jnp.max(s, axis=-1, keepdims=True))
            p = e * (row_valid / jnp.maximum(jnp.sum(e, axis=-1, keepdims=True), 1e-30))
            outs.append(_dot(p.astype(BF16), vc))
            imp = imp + p
        oc_ref[:, pair * PAIR:(pair + 1) * PAIR] = jnp.where(low, outs[0], outs[1])

    pool = pool_ref[pl.ds(start, nch), :]
    hi, mid, lo = _split3(imp)
    p_slc = _dot(hi, pool) + _dot(mid, pool) + _dot(lo, pool)

    cur = t // SLC_BLOCK

    def pick_blocks(width):
        blk = lax.broadcasted_iota(jnp.int32, (1, width), 1)
        forced = (blk == 0) | (blk == cur) | (blk == cur - 1)
        val = jnp.where(forced, REMOVED, jnp.where(blk > cur, NEG_INF, p_slc[:, :width]))
        sel = jnp.where(forced, 1.0, 0.0)
        blk_f = blk.astype(F32)
        for _ in range(n_sel - 3):
            mx = jnp.max(val, axis=-1, keepdims=True)
            pick = jnp.min(jnp.where(val == mx, blk_f, float(width)), axis=-1, keepdims=True)
            hit = blk_f == pick
            sel = jnp.where(hit, 1.0, sel)
            val = jnp.where(hit, REMOVED, val)
        if width < ns:
            sel = jnp.concatenate([sel, jnp.zeros((tq, ns - width), F32)], axis=1)
        sel_ref[...] = sel
        any_sel = jnp.broadcast_to(jnp.max(sel, axis=0, keepdims=True), (8, ns)).astype(BF16)
        flag_ref[...] = (_dot(any_sel, tmap_ref[...]) > 0.5).astype(jnp.int32)

    in_first_half = t0 + tq <= (ns // 2) * SLC_BLOCK
    pl.when(in_first_half)(lambda: pick_blocks(ns // 2))
    pl.when(jnp.logical_not(in_first_half))(lambda: pick_blocks(ns))


def _cmp_select(first_chunk, pj, cmp_kv, pool, tmap, *, batch, seq, tq, q_blk, n_sel, nseg):
    nq = seq // tq
    nch = cmp_kv.shape[3]
    ns = pool.shape[1]
    gw = NSA_GROUP_SIZE * HEAD_DIM
    grid_spec = pltpu.PrefetchScalarGridSpec(
        num_scalar_prefetch=1,
        grid=(batch, NSA_GROUPS, nq),
        in_specs=[pl.BlockSpec((tq, gw), lambda b, g, i, f: (b * nq + i, q_blk + g)),
                  pl.BlockSpec((None, None, None, nch, PAIR), lambda b, g, i, f: (b, 0, g, 0, 0)),
                  pl.BlockSpec((None, None, None, nch, PAIR), lambda b, g, i, f: (b, 1, g, 0, 0)),
                  pl.BlockSpec((nch, ns), lambda b, g, i, f: (0, 0)),
                  pl.BlockSpec((ns, LANES), lambda b, g, i, f: (0, 0))],
        out_specs=(pl.BlockSpec((tq, gw), lambda b, g, i, f: (b * nq + i, g)),
                   pl.BlockSpec((None, None, tq, ns), lambda b, g, i, f: (b, g, i, 0)),
                   pl.BlockSpec((None, None, None, 8, LANES), lambda b, g, i, f: (b, g, i, 0, 0))))
    return pl.pallas_call(
        functools.partial(_cmp_select_kernel, tq=tq, n_sel=n_sel, nseg=nseg, nq=nq),
        out_shape=(jax.ShapeDtypeStruct((batch * seq, NSA_HEADS * HEAD_DIM), F32),
                   jax.ShapeDtypeStruct((batch, NSA_GROUPS, seq, ns), F32),
                   jax.ShapeDtypeStruct((batch, NSA_GROUPS, nq, 8, LANES), jnp.int32)),
        grid_spec=grid_spec,
        compiler_params=_cparams("parallel", "parallel", "parallel"),
        name="nsa_cmp_select",
    )(first_chunk, pj, cmp_kv, cmp_kv, pool, tmap)


def _alibi_dead(q_sq, k_sq, dist, group_slope):
    bound = 2.0 * NORM_SLACK * jnp.sqrt(q_sq * k_sq)
    return group_slope * dist >= UNDERFLOW_BITS + bound


def _nsa_kernel(flags_ref, q_ref, ks_ref, vs_ref, kwp_ref, vwp_ref, kwd_ref, vwd_ref, sel_ref, oc_ref, gate_ref,
                o_ref, m_ref, acc_ref, *, tile, nq):
    b = pl.program_id(0)
    g = pl.program_id(1)
    i = pl.program_id(2)
    nh = NSA_GROUP_SIZE
    ns = sel_ref.shape[1]
    bpt = tile // SLC_BLOCK
    low = _lane_is_low()
    row = lax.broadcasted_iota(jnp.int32, (tile, tile), 0)
    col = lax.broadcasted_iota(jnp.int32, (tile, tile), 1)
    relpos = lax.broadcasted_iota(jnp.int32, (1, tile), 1).astype(F32)
    sel = sel_ref[...].astype(BF16)
    blk_of_key = lax.broadcasted_iota(jnp.int32, (ns, tile), 1) // SLC_BLOCK
    blk_row = lax.broadcasted_iota(jnp.int32, (ns, tile), 0)
    qh = []
    for pair in range(nh // 2):
        qh.extend(_split_pair(q_ref[:, pair * PAIR:(pair + 1) * PAIR]))
    slopes = [_group_slope(g, r) for r in range(nh)]
    _flash_reset(m_ref, acc_ref)

    def selected(j):
        expand = jnp.where(blk_row == blk_of_key + j * bpt, 1.0, 0.0).astype(BF16)
        return _dot(sel, expand) > 0.5

    def attend(slot, k_keys, v_keys, rel, bias, r0=0, nr=tile):
        v_aug = _with_ones(v_keys)
        rows = pl.ds(r0, nr)
        for r in range(nh):
            s = _dot_nt(qh[r][r0:r0 + nr], k_keys) + slopes[r] * rel + bias
            _flash_step(s, v_aug, m_ref.at[slot + r, rows], acc_ref.at[slot + r, rows])

    def sel_step(j, keep):
        ks = pl.multiple_of(j * tile, tile)
        attend(0, ks_ref[pl.ds(ks, tile), :], vs_ref[pl.ds(ks, tile), :],
               relpos + ((j - i) * tile).astype(F32), jnp.where(keep, 0.0, NEG_INF))

    half = tile // 2
    ks_i = pl.multiple_of(i * tile, tile)
    bias_diag = jnp.where(selected(i) & (col <= row), 0.0, NEG_INF)
    attend(0, ks_ref[pl.ds(ks_i, half), :], vs_ref[pl.ds(ks_i, half), :], relpos[:, :half],
           bias_diag[:half, :half], 0, half)
    attend(0, ks_ref[pl.ds(ks_i, tile), :], vs_ref[pl.ds(ks_i, tile), :], relpos, bias_diag[half:, :], half, half)

    def sel_body(j, carry):
        @pl.when(flags_ref[((b * NSA_GROUPS + g) * nq + i) * nq + j] != 0)
        def _():
            sel_step(j, selected(j))
        return carry

    lax.fori_loop(0, i, sel_body, 0)

    span = tile + half
    k_win = jnp.concatenate([kwp_ref[...], kwd_ref[...]], axis=0)
    v_win = jnp.concatenate([vwp_ref[...], vwd_ref[...]], axis=0)
    for r0 in (0, half):
        row2 = lax.broadcasted_iota(jnp.int32, (half, span), 0) + r0
        col2 = lax.broadcasted_iota(jnp.int32, (half, span), 1) + r0
        in_window = ((col2 >= tile) & (col2 - tile <= row2)) | ((col2 < tile) & (col2 > row2) & (i > 0))
        rel2 = (lax.broadcasted_iota(jnp.int32, (1, span), 1) + (r0 - tile)).astype(F32)
        attend(nh, k_win[r0:r0 + span], v_win[r0:r0 + span], rel2, jnp.where(in_window, 0.0, NEG_INF), r0, half)

    gates = jax.nn.sigmoid(gate_ref[...])

    def gate_col(r, branch):
        c0 = FOX_HEADS + 3 * r + branch
        c1 = c0 + 3 * nh
        return jnp.where(g == 0, gates[:, c0:c0 + 1], gates[:, c1:c1 + 1])

    for pair in range(nh // 2):
        o_c = oc_ref[:, pair * PAIR:(pair + 1) * PAIR]
        outs = []
        for h in range(2):
            r = 2 * pair + h
            o_s = _flash_out(acc_ref[r])
            o_w = _flash_out(acc_ref[nh + r])
            outs.append(gate_col(r, 0) * o_c + gate_col(r, 1) * o_s + gate_col(r, 2) * o_w)
        o_ref[:, pair * PAIR:(pair + 1) * PAIR] = jnp.where(low, outs[0], outs[1]).astype(o_ref.dtype)


def _nsa_attention(flags, pj, sel, o_c, small, *, batch, seq, tile, q_blk, ks_blk, vs_blk, kw_blk, vw_blk):
    nq = seq // tile
    ns = sel.shape[3]
    gw = NSA_GROUP_SIZE * HEAD_DIM
    slots = 2 * NSA_GROUP_SIZE
    cur = lambda b, g, i, f, base: (b * nq + i, base + g)
    prev = lambda b, g, i, f, base: (b * nq + jnp.maximum(i - 1, 0), base + g)
    grid_spec = pltpu.PrefetchScalarGridSpec(
        num_scalar_prefetch=1,
        grid=(batch, NSA_GROUPS, nq),
        in_specs=[pl.BlockSpec((tile, gw), functools.partial(cur, base=q_blk)),
                  pl.BlockSpec((seq, PAIR), lambda b, g, i, f: (b, ks_blk + g)),
                  pl.BlockSpec((seq, PAIR), lambda b, g, i, f: (b, vs_blk + g)),
                  pl.BlockSpec((tile, PAIR), functools.partial(prev, base=kw_blk)),
                  pl.BlockSpec((tile, PAIR), functools.partial(prev, base=vw_blk)),
                  pl.BlockSpec((tile, PAIR), functools.partial(cur, base=kw_blk)),
                  pl.BlockSpec((tile, PAIR), functools.partial(cur, base=vw_blk)),
                  pl.BlockSpec((None, None, tile, ns), lambda b, g, i, f: (b, g, i, 0)),
                  pl.BlockSpec((tile, gw), functools.partial(cur, base=0)),
                  pl.BlockSpec((tile, small.shape[1]), lambda b, g, i, f: (b * nq + i, 0))],
        out_specs=pl.BlockSpec((tile, gw), functools.partial(cur, base=0)),
        scratch_shapes=[pltpu.VMEM((slots, tile, LANES), F32), pltpu.VMEM((slots, tile, 2 * PAIR), F32)])
    return pl.pallas_call(
        functools.partial(_nsa_kernel, tile=tile, nq=nq),
        out_shape=jax.ShapeDtypeStruct((batch * seq, NSA_HEADS * HEAD_DIM), BF16),
        grid_spec=grid_spec,
        compiler_params=_cparams("parallel", "parallel", "arbitrary"),
        name="nsa_attention",
    )(flags, pj, pj, pj, pj, pj, pj, pj, sel, o_c, small)


def _ffn_kernel(h_ref, gin_ref, gout_ref, wg_ref, wu_ref, wd_ref, *rest, chunk, mixer, tm, tiles_per_seq):
    h = h_ref[...]
    if mixer == "attn":
        a_ref, b_ref, wa_ref, wb_ref, gmix_ref, o_ref, acc_ref = rest
        h = h + _rms(_dot(a_ref[...], wa_ref[...]) + _dot(b_ref[...], wb_ref[...]), gmix_ref[...])
    else:
        halo_ref, pin_ref, pout_ref, pw_ref, sc_ref, o_ref, acc_ref, ext_ref = rest
        h = _pool_mix(h, halo_ref, pin_ref, pout_ref, pw_ref, sc_ref, ext_ref, tm=tm, tiles_per_seq=tiles_per_seq)
    ub = _rms(h, gin_ref[...]).astype(BF16)
    hidden = wg_ref.shape[1]
    for idx, c in enumerate(range(0, hidden, chunk)):
        gate = _dot(ub, wg_ref[:, c:c + chunk])
        up = _dot(ub, wu_ref[:, c:c + chunk])
        act = (gate * jax.nn.sigmoid(gate) * up).astype(BF16)
        part = _dot(act, wd_ref[c:c + chunk, :])
        if idx == 0:
            acc_ref[...] = part
        else:
            acc_ref[...] += part
    o_ref[...] = h + _rms(acc_ref[...], gout_ref[...])


def _layer_tail(h2, g_in, g_out, wg, wu, wd, *, seq, tm, chunk, mixer, mix):
    n, d = h2.shape
    const = lambda i: (0, 0)
    rows = lambda i: (i, 0)
    scratch = [pltpu.VMEM((tm, d), F32)]
    if mixer == "attn":
        a, b, wa, wb, g_mix = mix
        mix_specs = [pl.BlockSpec((tm, a.shape[1]), rows), pl.BlockSpec((tm, b.shape[1]), rows),
                     pl.BlockSpec(wa.shape, const), pl.BlockSpec(wb.shape, const), pl.BlockSpec((1, d), const)]
    else:
        p_in, p_out, w_groups, scale = mix
        ratio = tm // POOL_HALO
        mix = (h2, p_in, p_out, w_groups, scale)
        mix_specs = [pl.BlockSpec((POOL_HALO, d), lambda i: (jnp.maximum(i * ratio - 1, 0), 0)),
                     pl.BlockSpec((1, d), const), pl.BlockSpec((1, d), const),
                     pl.BlockSpec(w_groups.shape, lambda i: (0, 0, 0)), pl.BlockSpec((1, d), const)]
        scratch.append(pltpu.VMEM((tm + POOL_HALO, d), F32))
    return pl.pallas_call(
        functools.partial(_ffn_kernel, chunk=chunk, mixer=mixer, tm=tm, tiles_per_seq=seq // tm),
        out_shape=jax.ShapeDtypeStruct((n, d), F32),
        grid=(n // tm,),
        in_specs=[pl.BlockSpec((tm, d), rows),
                  pl.BlockSpec((1, d), const),
                  pl.BlockSpec((1, d), const),
                  pl.BlockSpec(wg.shape, const),
                  pl.BlockSpec(wu.shape, const),
                  pl.BlockSpec(wd.shape, const)] + mix_specs,
        out_specs=pl.BlockSpec((tm, d), rows),
        scratch_shapes=scratch,
        compiler_params=_cparams("parallel"),
        name=f"{mixer}_tail_swiglu_ffn",
    )(h2, g_in, g_out, wg, wu, wd, *mix)


def _pool_mix(h, halo_ref, gin_ref, gout_ref, w_ref, sc_ref, ext_ref, *, tm, tiles_per_seq):
    halo = POOL_HALO
    i = pl.program_id(0)
    first = (i % tiles_per_seq) == 0
    u = _rms(h, gin_ref[...])
    uh = _rms(halo_ref[...], gin_ref[...])
    ext_ref[0:halo, :] = jnp.where(first, 0.0, uh)
    ext_ref[halo:, :] = u
    t = (i % tiles_per_seq) * tm + lax.broadcasted_iota(jnp.int32, (tm, 1), 0)
    group = h.shape[1] // len(POOL_WINDOWS)
    ys = []
    for gi, w in enumerate(POOL_WINDOWS):
        cols = slice(gi * group, (gi + 1) * group)
        run = ext_ref[:, cols]
        span = 1
        while span < w:
            run = run + pltpu.roll(run, span, axis=0)
            span *= 2
        total = run[halo:, :]
        count = jnp.minimum(t + 1, w).astype(F32)
        pooled = total / count - u[:, cols]
        ys.append(_dot(pooled.astype(BF16), w_ref[gi]))
    y = jnp.concatenate(ys, axis=-1) * sc_ref[...]
    return h + _rms(y, gout_ref[...])


def _pack_in_weights(w_in):
    d = w_in.shape[0]
    hw = FOX_HEADS * HEAD_DIM
    kv = NSA_GROUPS * HEAD_DIM
    sizes = (hw, hw, hw, FOX_HEADS, NSA_HEADS * HEAD_DIM, kv, kv, kv, kv, kv, kv, 3 * NSA_HEADS)
    offs = np.concatenate([[0], np.cumsum(sizes)])
    fq, fk, fv, ff, nq, kc, vc, ks, vs, kw, vw, ng = [w_in[:, offs[j]:offs[j + 1]] for j in range(len(sizes))]
    scale = HEAD_DIM ** -0.5 * LOG2E

    def dup(w):
        w = w.reshape(d, NSA_GROUPS, 1, HEAD_DIM)
        return jnp.broadcast_to(w, (d, NSA_GROUPS, 2, HEAD_DIM)).reshape(d, NSA_GROUPS * PAIR)

    w_main = jnp.concatenate([fq * scale, fk, fv, nq * scale, dup(ks), dup(vs), dup(kw), dup(vw)], axis=1)
    w_cmp = jnp.concatenate([kc, vc], axis=1)
    pad = jnp.zeros((d, LANES - FOX_HEADS - 3 * NSA_HEADS), F32)
    w_small = jnp.concatenate([ff, ng, pad], axis=1)
    w_small_hi = w_small.astype(BF16)
    w_small_lo = (w_small - w_small_hi.astype(F32)).astype(BF16)
    return w_main.astype(BF16), w_cmp.astype(BF16), jnp.concatenate([w_small_hi, w_small_lo], axis=1)


def _mixer_fox_nsa(h2, g_in, g_out, w_in, b_f, cmp_pe, cmp_w1, cmp_w2, w_out, *, batch, seq):
    tile = WINDOW
    assert seq % tile == 0 and seq // tile <= LANES
    w_main, w_cmp, w_small = _pack_in_weights(w_in)
    pj, pc, small, nrm = _inproj(h2, g_in, w_main, w_cmp, w_small, tm=tile)
    npair = FOX_HEADS // 2
    q_blk, k_blk, v_blk, nq_blk = 0, npair, 2 * npair, 3 * npair
    ks_blk = nq_blk + NSA_HEADS // 2
    vs_blk, kw_blk, vw_blk = ks_blk + NSA_GROUPS, ks_blk + 2 * NSA_GROUPS, ks_blk + 3 * NSA_GROUPS

    c = _forget_cumsum(small, b_f, batch=batch, seq=seq, tc=tile)
    nq = seq // tile
    fox_tq, fox_tk = min(FOX_TQ, seq), min(FOX_TK, seq)
    c_tiles = c.reshape(batch, npair, 2, seq // fox_tk, fox_tk).transpose(0, 1, 3, 2, 4)
    live = _fox_live_tiles(c, nrm, batch=batch, seq=seq, tq=fox_tq, tk=fox_tk, tm=tile)
    o_fox = _fox_attention(live, pj, c_tiles, batch=batch, seq=seq, tq=fox_tq, tk=fox_tk,
                           q_blk=q_blk, k_blk=k_blk, v_blk=v_blk)

    nch = seq // CMP_STRIDE
    ns = seq // SLC_BLOCK
    n_sel = min(N_SELECT, ns)
    kv = NSA_GROUPS * HEAD_DIM
    cmp_kv, cmp_nrm = _compress(pc, cmp_pe, cmp_w1, cmp_w2, batch=batch, seq=seq)

    group_slope = jnp.asarray([min(ALIBI_SLOPES_LOG2[g * NSA_GROUP_SIZE:(g + 1) * NSA_GROUP_SIZE])
                               for g in range(NSA_GROUPS)], F32)
    qn = nrm[:, NRM_NSA_Q, :NSA_HEADS].reshape(batch, nq, NSA_GROUPS, NSA_GROUP_SIZE).max(axis=3)
    ksn = nrm[:, NRM_NSA_KS, 0:2 * NSA_GROUPS:2].reshape(batch, nq, NSA_GROUPS).max(axis=1)
    kcn = cmp_nrm[:, 0, :, 0, 0]
    tiles = jnp.arange(nq)
    nseg = CMP_SEGMENTS if nq % CMP_SEGMENTS == 0 else 1
    chunk = nch // nseg
    last_end = (jnp.arange(nseg) + 1) * chunk * CMP_STRIDE + (CMP_BLOCK - 1 - CMP_STRIDE)
    dist_c = (tiles[:, None] * tile - last_end[None, :] - (CMP_STRIDE - 1)).astype(F32)
    dead_c = _alibi_dead(qn[..., None], kcn[:, None, :, None], dist_c[None, :, None, :],
                         group_slope[None, None, :, None])
    first_chunk = jnp.minimum(dead_c.sum(axis=3), (tiles * nseg // nq)[None, :, None])
    first_chunk = first_chunk.transpose(0, 2, 1).astype(jnp.int32).reshape(-1)
    dist_s = ((tiles[:, None] - tiles[None, :] - 1) * tile + 1).astype(F32)
    dead_s = _alibi_dead(qn[..., None], ksn[:, None, :, None], dist_s[None, :, None, :],
                         group_slope[None, None, :, None])
    live_s = jnp.logical_not(dead_s).transpose(0, 2, 1, 3).astype(jnp.int32)

    ratio = SLC_BLOCK // CMP_STRIDE
    r = CMP_BLOCK // CMP_STRIDE
    n_idx = np.arange(nch)[:, None]
    b_idx = np.arange(ns)[None, :]
    pool_np = ((n_idx >= ratio * b_idx - (r - 1)) & (n_idx <= ratio * b_idx + ratio - 1)
               & (n_idx < nch - r + 1)).astype(np.float32)
    pool = jnp.asarray(pool_np, dtype=BF16)
    tmap_np = (np.arange(ns)[:, None] // (tile // SLC_BLOCK) == np.arange(LANES)[None, :]).astype(np.float32)
    o_c, sel, flag_blocks = _cmp_select(first_chunk, pj, cmp_kv, pool, jnp.asarray(tmap_np, dtype=BF16),
                                        batch=batch, seq=seq, tq=tile, q_blk=nq_blk // 2, n_sel=n_sel, nseg=nseg)
    flags = (flag_blocks[:, :, :, 0, :nq] * live_s).reshape(-1)
    o_nsa = _nsa_attention(flags, pj, sel, o_c, small, batch=batch, seq=seq, tile=tile,
                           q_blk=nq_blk // 2, ks_blk=ks_blk, vs_blk=vs_blk, kw_blk=kw_blk, vw_blk=vw_blk)

    hw = FOX_HEADS * HEAD_DIM
    return o_fox, o_nsa, w_out[:hw].astype(BF16), w_out[hw:].astype(BF16), g_out


def kernel(x, norm_g, attn_w_in, fox_b_f, nsa_cmp_pe, nsa_cmp_w1, nsa_cmp_w2, attn_w_out,
           pool_w, pool_scale, ffn_w_gate, ffn_w_up, ffn_w_down):
    batch, seq, d = x.shape
    depth = norm_g.shape[0]
    tm = 512 if seq % 512 == 0 else seq
    h = x.reshape(batch * seq, d)
    for layer in range(depth):
        g = norm_g[layer].reshape(4, 1, d)
        i = layer // 2
        if layer % 2 == 0:
            mixer = "attn"
            mix = _mixer_fox_nsa(h, g[0], g[1], attn_w_in[i], fox_b_f[i], nsa_cmp_pe[i], nsa_cmp_w1[i],
                                 nsa_cmp_w2[i], attn_w_out[i], batch=batch, seq=seq)
        else:
            mixer = "pool"
            mix = (g[0], g[1], pool_w[i].astype(BF16), pool_scale[i].reshape(1, d))
        h = _layer_tail(h, g[2], g[3], ffn_w_gate[layer].astype(BF16), ffn_w_up[layer].astype(BF16),
                        ffn_w_down[layer].astype(BF16), seq=seq, tm=tm, chunk=256, mixer=mixer, mix=mix)
    return h.reshape(batch, seq, d)
```

```python
import functools

import numpy as np
import jax
import jax.numpy as jnp
from jax import lax
from jax.experimental import pallas as pl
from jax.experimental.pallas import tpu as pltpu

F32 = jnp.float32
BF16 = jnp.bfloat16

HEAD_DIM = 64
FOX_HEADS = 8
NSA_HEADS = 8
NSA_GROUPS = 2
NSA_GROUP_SIZE = NSA_HEADS // NSA_GROUPS
CMP_BLOCK = 32
CMP_STRIDE = 16
SLC_BLOCK = 64
N_SELECT = 16
WINDOW = 512
POOL_WINDOWS = (2, 4, 8, 16)
POOL_HALO = max(POOL_WINDOWS)
RMS_EPS = 1e-6
NEG_INF = -1e30
BIG = 1e30
REMOVED = -3e38
LANES = 128
PAIR = 2 * HEAD_DIM
VMEM_LIMIT = 56 * 1024 * 1024
CMP_SEGMENTS = 4
UNDERFLOW_BITS = 160.0
NORM_SLACK = 1.02
NORM_CHUNKS = (0, 1, 3, 4)
NRM_FOX_Q, NRM_FOX_K, NRM_NSA_Q, NRM_NSA_KS = range(4)
FOX_TQ, FOX_TK = 1024, 1024

LOG2E = 1.4426950408889634
ALIBI_SLOPES_LOG2 = tuple(float(2.0 ** (-8.0 * (i + 1.0) / NSA_HEADS)) * LOG2E for i in range(NSA_HEADS))


def _cparams(*sem):
    return pltpu.CompilerParams(dimension_semantics=sem, vmem_limit_bytes=VMEM_LIMIT)


def _rms(x, g):
    return x * lax.rsqrt(jnp.mean(x * x, axis=-1, keepdims=True) + RMS_EPS) * g


def _dot(a, b):
    return jnp.dot(a, b, preferred_element_type=F32)


def _dot_nt(a, b):
    return lax.dot_general(a, b, (((1,), (1,)), ((), ())), preferred_element_type=F32)


def _inproj_kernel(x_ref, g_ref, w_ref, wc_ref, ws_ref, hmap_ref, o_ref, oc_ref, os_ref, nrm_ref, *, fox_width):
    u = _rms(x_ref[...], g_ref[...])
    ub = u.astype(BF16)
    norms = []
    for c in range(0, o_ref.shape[1], fox_width):
        r = _dot(ub, w_ref[:, c:c + fox_width]).astype(BF16)
        o_ref[:, c:c + fox_width] = r
        if c // fox_width in NORM_CHUNKS:
            rf = r.astype(F32)
            sq = _dot((rf * rf).astype(BF16), hmap_ref[...])
            norms.append(jnp.max(sq, axis=0, keepdims=True))
    oc_ref[...] = _dot(ub, wc_ref[...])
    width = os_ref.shape[1]
    u_lo = (u - ub.astype(F32)).astype(BF16)
    both = _dot(ub, ws_ref[...])
    os_ref[...] = both[:, :width] + both[:, width:] + _dot(u_lo, ws_ref[:, :width])
    row = lax.broadcasted_iota(jnp.int32, nrm_ref.shape, 0)
    out = jnp.zeros(nrm_ref.shape, F32)
    for k, nk in enumerate(norms):
        out = jnp.where(row == k, nk, out)
    nrm_ref[...] = out


def _inproj(x2, g, w_main, w_cmp, w_small, *, tm):
    n, d = x2.shape
    cm, cc, cs = w_main.shape[1], w_cmp.shape[1], w_small.shape[1] // 2
    fox_width = FOX_HEADS * HEAD_DIM
    hmap = jnp.asarray((np.arange(fox_width)[:, None] // HEAD_DIM == np.arange(LANES)[None, :]).astype(np.float32),
                       dtype=BF16)
    const = lambda i: (0, 0)
    return pl.pallas_call(
        functools.partial(_inproj_kernel, fox_width=fox_width),
        out_shape=(jax.ShapeDtypeStruct((n, cm), BF16),
                   jax.ShapeDtypeStruct((n, cc), F32),
                   jax.ShapeDtypeStruct((n, cs), F32),
                   jax.ShapeDtypeStruct((n // tm, 8, LANES), F32)),
        grid=(n // tm,),
        in_specs=[pl.BlockSpec((tm, d), lambda i: (i, 0)),
                  pl.BlockSpec((1, d), const),
                  pl.BlockSpec((d, cm), const),
                  pl.BlockSpec((d, cc), const),
                  pl.BlockSpec((d, 2 * cs), const),
                  pl.BlockSpec((fox_width, LANES), const)],
        out_specs=(pl.BlockSpec((tm, cm), lambda i: (i, 0)),
                   pl.BlockSpec((tm, cc), lambda i: (i, 0)),
                   pl.BlockSpec((tm, cs), lambda i: (i, 0)),
                   pl.BlockSpec((None, 8, LANES), lambda i: (i, 0, 0))),
        compiler_params=_cparams("parallel"),
        name="inproj",
    )(x2, g, w_main, w_cmp, w_small, hmap)


def _forget_cumsum_kernel(sm_ref, bf_ref, tri_ref, o_ref, carry_ref):
    @pl.when(pl.program_id(1) == 0)
    def _():
        carry_ref[...] = jnp.zeros_like(carry_ref)

    zt = sm_ref[...].T
    z = zt[0:FOX_HEADS, :] + bf_ref[...]
    log_f = jnp.minimum(z, 0.0) - jnp.log1p(jnp.exp(-jnp.abs(z)))
    c = jnp.dot(log_f, tri_ref[...], preferred_element_type=F32,
                precision=lax.Precision.HIGHEST) + carry_ref[...]
    o_ref[...] = c * LOG2E
    carry_ref[...] = c[:, c.shape[1] - 1:]


def _forget_cumsum(small, b_f, *, batch, seq, tc):
    tri = jnp.asarray(np.triu(np.ones((tc, tc), np.float32)))
    nchunk = seq // tc
    return pl.pallas_call(
        _forget_cumsum_kernel,
        out_shape=jax.ShapeDtypeStruct((batch, FOX_HEADS, seq), F32),
        grid=(batch, nchunk),
        in_specs=[pl.BlockSpec((tc, small.shape[1]), lambda b, j: (b * nchunk + j, 0)),
                  pl.BlockSpec((FOX_HEADS, 1), lambda b, j: (0, 0)),
                  pl.BlockSpec((tc, tc), lambda b, j: (0, 0))],
        out_specs=pl.BlockSpec((None, FOX_HEADS, tc), lambda b, j: (b, 0, j)),
        scratch_shapes=[pltpu.VMEM((FOX_HEADS, 1), F32)],
        compiler_params=_cparams("parallel", "arbitrary"),
        name="forget_cumsum",
    )(small, b_f.reshape(FOX_HEADS, 1), tri)


def _lane_is_low():
    return lax.broadcasted_iota(jnp.int32, (1, PAIR), 1) < HEAD_DIM


def _with_ones(v_tile):
    return jnp.concatenate([v_tile, jnp.ones_like(v_tile)], axis=1)


def _flash_step(s, v, m_ref, acc_ref, l_ref=None):
    m_old = m_ref[...]
    m_new = jnp.maximum(m_old, jnp.max(s, axis=-1, keepdims=True))
    alpha = jnp.exp2(m_old - m_new)
    p = jnp.exp2(s - jnp.concatenate([m_new] * (s.shape[1] // LANES), axis=1))
    if l_ref is None:
        alpha_acc = jnp.concatenate([alpha, alpha], axis=1)
    else:
        alpha_acc = alpha
        l_ref[...] = alpha * l_ref[...] + jnp.sum(p, axis=-1, keepdims=True)
    acc_ref[...] = alpha_acc * acc_ref[...] + _dot(p.astype(BF16), v)
    m_ref[...] = m_new


def _flash_reset(m_ref, *sum_refs):
    m_ref[...] = jnp.full_like(m_ref, NEG_INF)
    for ref in sum_refs:
        ref[...] = jnp.zeros_like(ref)


def _flash_out(acc):
    return acc[:, :PAIR] / acc[:, PAIR:]


def _split_pair(q):
    low = _lane_is_low()
    zero = jnp.zeros_like(q)
    return jnp.where(low, q, zero), jnp.where(low, zero, q)


def _fox_kernel(live_ref, q_ref, k_ref, v_ref, c_ref, o_ref, m_ref, l_ref, acc_ref, *, tq, tk, nq):
    b = pl.program_id(0)
    pair = pl.program_id(1)
    i = pl.program_id(2)
    nk = c_ref.shape[0]
    qh = _split_pair(q_ref[...])
    _flash_reset(m_ref, l_ref, acc_ref)

    def head_step(h, j, mask):
        ks = pl.multiple_of(j * tk, tk)
        s = _dot_nt(qh[h], k_ref[pl.ds(ks, tk), :]) - c_ref[j, h:h + 1, :]
        if mask is not None:
            s = jnp.where(mask, s, NEG_INF)
        _flash_step(s, v_ref[pl.ds(ks, tk), :], m_ref.at[h], acc_ref.at[h], l_ref.at[h])

    def body(j, carry):
        live = [live_ref[(((b * FOX_HEADS) + 2 * pair + h) * nq + i) * nk + j] != 0 for h in range(2)]

        @pl.when(live[0] & live[1])
        def _():
            head_step(0, j, None)
            head_step(1, j, None)

        for h in range(2):
            @pl.when(live[h] & jnp.logical_not(live[1 - h]))
            def _(h=h):
                head_step(h, j, None)
        return carry

    lax.fori_loop(0, i, body, 0)

    def diag_part(h, r0, nr, nkeys):
        ks = pl.multiple_of(i * tq, tq)
        rows = pl.ds(r0, nr)
        row = lax.broadcasted_iota(jnp.int32, (nr, nkeys), 0) + r0
        col = lax.broadcasted_iota(jnp.int32, (nr, nkeys), 1)
        s = _dot_nt(qh[h][r0:r0 + nr], k_ref[pl.ds(ks, nkeys), :]) - c_ref[i, h:h + 1, 0:nkeys]
        _flash_step(jnp.where(col <= row, s, NEG_INF), v_ref[pl.ds(ks, nkeys), :],
                    m_ref.at[h, rows], acc_ref.at[h, rows], l_ref.at[h, rows])

    half = tq // 2
    for h in range(2):
        diag_part(h, 0, half, half)
        diag_part(h, half, half, tq)
    o_ref[...] = jnp.where(_lane_is_low(), acc_ref[0] / l_ref[0], acc_ref[1] / l_ref[1]).astype(o_ref.dtype)


def _fox_live_tiles(c2, nrm, *, batch, seq, tq, tk, tm):
    nq, nk = seq // tq, seq // tk
    qn = nrm[:, 0, :FOX_HEADS].reshape(batch, nq, tq // tm, FOX_HEADS).max(axis=2)
    kn = nrm[:, 1, :FOX_HEADS].reshape(batch, seq // tm, FOX_HEADS).max(axis=1)
    bound = 2.0 * NORM_SLACK * jnp.sqrt(qn * kn[:, None, :])
    c_start = c2[:, :, ::tq]
    c_end = c2[:, :, tk - 1::tk]
    decay = c_end[:, :, None, :] - c_start[:, :, :, None]
    live = decay < UNDERFLOW_BITS + jnp.transpose(bound, (0, 2, 1))[..., None]
    return live.astype(jnp.int32).reshape(-1)


def _fox_attention(live, pj, c_tiles, *, batch, seq, tq, tk, q_blk, k_blk, v_blk):
    assert tq == tk and tq % (2 * LANES) == 0
    nq = seq // tq
    nk = seq // tk
    npair = FOX_HEADS // 2
    grid_spec = pltpu.PrefetchScalarGridSpec(
        num_scalar_prefetch=1,
        grid=(batch, npair, nq),
        in_specs=[pl.BlockSpec((tq, PAIR), lambda b, p, i, f: (b * nq + i, q_blk + p)),
                  pl.BlockSpec((seq, PAIR), lambda b, p, i, f: (b, k_blk + p)),
                  pl.BlockSpec((seq, PAIR), lambda b, p, i, f: (b, v_blk + p)),
                  pl.BlockSpec((None, None, nk, 2, tk), lambda b, p, i, f: (b, p, 0, 0, 0))],
        out_specs=pl.BlockSpec((tq, PAIR), lambda b, p, i, f: (b * nq + i, p)),
        scratch_shapes=[pltpu.VMEM((2, tq, LANES), F32), pltpu.VMEM((2, tq, LANES), F32),
                        pltpu.VMEM((2, tq, PAIR), F32)])
    return pl.pallas_call(
        functools.partial(_fox_kernel, tq=tq, tk=tk, nq=nq),
        out_shape=jax.ShapeDtypeStruct((batch * seq, FOX_HEADS * HEAD_DIM), BF16),
        grid_spec=grid_spec,
        compiler_params=_cparams("parallel", "parallel", "arbitrary"),
        name="fox_attention",
    )(live, pj, pj, pj, c_tiles)


def _gelu_tanh(x):
    return 0.5 * x * (1.0 + jnp.tanh(0.7978845608028654 * (x + 0.044715 * x * x * x)))


def _compress_kernel(x_ref, pe_ref, w1pos_ref, w1dup_ref, w2sel_ref, o_ref, nrm_ref):
    nch = o_ref.shape[1]
    a = b = None
    for j in range(CMP_STRIDE):
        xj = x_ref[pl.ds(j, nch, stride=CMP_STRIDE), :].astype(BF16)
        pa, pb = _dot(xj, w1pos_ref[j]), _dot(xj, w1pos_ref[CMP_STRIDE + j])
        a, b = (pa, pb) if a is None else (a + pa, b + pb)
    pe_term = _dot(pe_ref[...], w1dup_ref[...])[0:1, :]
    h = a + pltpu.roll(b, nch - 1, axis=0) + pe_term
    y = _gelu_tanh(h).astype(BF16)
    for g in range(NSA_GROUPS):
        o = _dot(y, w2sel_ref[g]).astype(o_ref.dtype)
        o_ref[g] = o
        of = o.astype(F32)
        sq = 0.5 * jnp.sum(of * of, axis=-1, keepdims=True)
        nrm_ref[g] = jnp.broadcast_to(jnp.max(sq, axis=0, keepdims=True), nrm_ref.shape[1:])


def _compress(pc, cmp_pe, cmp_w1, cmp_w2, *, batch, seq):
    nch = seq // CMP_STRIDE
    dh = HEAD_DIM
    zero = jnp.zeros((2, CMP_BLOCK, dh, dh), F32)
    w1r = cmp_w1.reshape(2, CMP_BLOCK, dh, dh)
    w1pos = jnp.concatenate([jnp.concatenate([w1r, zero], axis=-1),
                             jnp.concatenate([zero, w1r], axis=-1)], axis=-2).astype(BF16)
    w1dup = jnp.concatenate([cmp_w1, cmp_w1], axis=-1).astype(BF16)
    w2dup = jnp.concatenate([cmp_w2, cmp_w2], axis=-1)
    zero2 = jnp.zeros_like(w2dup)
    w2sel = jnp.stack([jnp.concatenate([w2dup, zero2], axis=1),
                       jnp.concatenate([zero2, w2dup], axis=1)], axis=1).astype(BF16)
    pe8 = jnp.broadcast_to(cmp_pe.reshape(2, 1, CMP_BLOCK * dh), (2, 8, CMP_BLOCK * dh)).astype(BF16)
    return pl.pallas_call(
        _compress_kernel,
        out_shape=(jax.ShapeDtypeStruct((batch, 2, NSA_GROUPS, nch, PAIR), BF16),
                   jax.ShapeDtypeStruct((batch, 2, NSA_GROUPS, 8, LANES), F32)),
        grid=(batch, 2),
        in_specs=[pl.BlockSpec((seq, PAIR), lambda b, kv: (b, kv)),
                  pl.BlockSpec((None, 8, CMP_BLOCK * dh), lambda b, kv: (kv, 0, 0)),
                  pl.BlockSpec((None, CMP_BLOCK, PAIR, PAIR), lambda b, kv: (kv, 0, 0, 0)),
                  pl.BlockSpec((None, CMP_BLOCK * dh, PAIR), lambda b, kv: (kv, 0, 0)),
                  pl.BlockSpec((None, NSA_GROUPS, PAIR, PAIR), lambda b, kv: (kv, 0, 0, 0))],
        out_specs=(pl.BlockSpec((None, None, NSA_GROUPS, nch, PAIR), lambda b, kv: (b, kv, 0, 0, 0)),
                   pl.BlockSpec((None, None, NSA_GROUPS, 8, LANES), lambda b, kv: (b, kv, 0, 0, 0))),
        compiler_params=_cparams("parallel", "parallel"),
        name="nsa_compress",
    )(pc, pe8, w1pos, w1dup, w2sel)


def _group_slope(g, r):
    return jnp.where(g == 0, ALIBI_SLOPES_LOG2[r], ALIBI_SLOPES_LOG2[NSA_GROUP_SIZE + r]).astype(F32)


def _split3(x):
    hi = x.astype(BF16)
    r1 = x - hi.astype(F32)
    mid = r1.astype(BF16)
    lo = (r1 - mid.astype(F32)).astype(BF16)
    return hi, mid, lo


def _cmp_select_kernel(first_ref, q_ref, kc_ref, vc_ref, pool_ref, tmap_ref, oc_ref, sel_ref, flag_ref, *,
                       tq, n_sel, nseg, nq):
    b = pl.program_id(0)
    g = pl.program_id(1)
    i = pl.program_id(2)
    chunk = kc_ref.shape[0] // nseg
    first = first_ref[(b * NSA_GROUPS + g) * nq + i]
    n_live = (i * nseg) // nq + 1 - first
    for k in range(1, nseg + 1):
        @pl.when(n_live == k)
        def _(k=k):
            _cmp_select_body(q_ref, kc_ref, vc_ref, pool_ref, tmap_ref, oc_ref, sel_ref, flag_ref,
                             tq=tq, n_sel=n_sel, start=pl.multiple_of(first * chunk, chunk), nch=k * chunk)


def _cmp_select_body(q_ref, kc_ref, vc_ref, pool_ref, tmap_ref, oc_ref, sel_ref, flag_ref, *,
                     tq, n_sel, start, nch):
    g = pl.program_id(1)
    t0 = pl.program_id(2) * tq
    ns = pool_ref.shape[1]
    low = _lane_is_low()
    t = t0 + lax.broadcasted_iota(jnp.int32, (tq, 1), 0)
    cmp_end = (start + lax.broadcasted_iota(jnp.int32, (1, nch), 1)) * CMP_STRIDE + (CMP_BLOCK - 1)
    rel_end = (cmp_end - t0).astype(F32)
    mask_bias = jnp.where(t >= cmp_end, 0.0, NEG_INF)
    row_valid = jnp.where(t >= CMP_BLOCK - 1, 1.0, 0.0)
    kc = kc_ref[pl.ds(start, nch), :]
    vc = vc_ref[pl.ds(start, nch), :]
    imp = jnp.zeros((tq, nch), F32)
    for pair in range(NSA_GROUP_SIZE // 2):
        qh = _split_pair(q_ref[:, pair * PAIR:(pair + 1) * PAIR])
        outs = []
        for h in range(2):
            s = _dot_nt(qh[h], kc) + _group_slope(g, 2 * pair + h) * rel_end + mask_bias
            e = jnp.exp2(s - jnp.max(s, axis=-1, keepdims=True))
            p = e * (row_valid / jnp.maximum(jnp.sum(e, axis=-1, keepdims=True), 1e-30))
            outs.append(_dot(p.astype(BF16), vc))
            imp = imp + p
        oc_ref[:, pair * PAIR:(pair + 1) * PAIR] = jnp.where(low, outs[0], outs[1])

    pool = pool_ref[pl.ds(start, nch), :]
    hi, mid, lo = _split3(imp)
    p_slc = _dot(hi, pool) + _dot(mid, pool) + _dot(lo, pool)

    blk = lax.broadcasted_iota(jnp.int32, (1, ns), 1)
    cur = t // SLC_BLOCK
    forced = (blk == 0) | (blk == cur) | (blk == cur - 1)
    val = jnp.where(forced, REMOVED, jnp.where(blk > cur, NEG_INF, p_slc))
    sel = jnp.where(forced, 1.0, 0.0)
    blk_f = blk.astype(F32)
    for _ in range(n_sel - 3):
        mx = jnp.max(val, axis=-1, keepdims=True)
        pick = jnp.min(jnp.where(val == mx, blk_f, float(ns)), axis=-1, keepdims=True)
        hit = blk_f == pick
        sel = jnp.where(hit, 1.0, sel)
        val = jnp.where(hit, REMOVED, val)
    sel_ref[...] = sel.astype(sel_ref.dtype)
    any_sel = jnp.broadcast_to(jnp.max(sel, axis=0, keepdims=True), (8, ns)).astype(BF16)
    flag_ref[...] = (_dot(any_sel, tmap_ref[...]) > 0.5).astype(jnp.int32)


def _cmp_select(first_chunk, pj, cmp_kv, pool, tmap, *, batch, seq, tq, q_blk, n_sel, nseg):
    nq = seq // tq
    nch = cmp_kv.shape[3]
    ns = pool.shape[1]
    gw = NSA_GROUP_SIZE * HEAD_DIM
    grid_spec = pltpu.PrefetchScalarGridSpec(
        num_scalar_prefetch=1,
        grid=(batch, NSA_GROUPS, nq),
        in_specs=[pl.BlockSpec((tq, gw), lambda b, g, i, f: (b * nq + i, q_blk + g)),
                  pl.BlockSpec((None, None, None, nch, PAIR), lambda b, g, i, f: (b, 0, g, 0, 0)),
                  pl.BlockSpec((None, None, None, nch, PAIR), lambda b, g, i, f: (b, 1, g, 0, 0)),
                  pl.BlockSpec((nch, ns), lambda b, g, i, f: (0, 0)),
                  pl.BlockSpec((ns, LANES), lambda b, g, i, f: (0, 0))],
        out_specs=(pl.BlockSpec((tq, gw), lambda b, g, i, f: (b * nq + i, g)),
                   pl.BlockSpec((None, None, tq, ns), lambda b, g, i, f: (b, g, i, 0)),
                   pl.BlockSpec((None, None, None, 8, LANES), lambda b, g, i, f: (b, g, i, 0, 0))))
    return pl.pallas_call(
        functools.partial(_cmp_select_kernel, tq=tq, n_sel=n_sel, nseg=nseg, nq=nq),
        out_shape=(jax.ShapeDtypeStruct((batch * seq, NSA_HEADS * HEAD_DIM), F32),
                   jax.ShapeDtypeStruct((batch, NSA_GROUPS, seq, ns), BF16),
                   jax.ShapeDtypeStruct((batch, NSA_GROUPS, nq, 8, LANES), jnp.int32)),
        grid_spec=grid_spec,
        compiler_params=_cparams("parallel", "parallel", "parallel"),
        name="nsa_cmp_select",
    )(first_chunk, pj, cmp_kv, cmp_kv, pool, tmap)


def _alibi_dead(q_sq, k_sq, dist, group_slope):
    bound = 2.0 * NORM_SLACK * jnp.sqrt(q_sq * k_sq)
    return group_slope * dist >= UNDERFLOW_BITS + bound


def _nsa_kernel(flags_ref, q_ref, ks_ref, vs_ref, kwp_ref, vwp_ref, kwd_ref, vwd_ref, sel_ref, oc_ref, gate_ref,
                o_ref, m_ref, acc_ref, *, tile, nq):
    b = pl.program_id(0)
    g = pl.program_id(1)
    i = pl.program_id(2)
    nh = NSA_GROUP_SIZE
    ns = sel_ref.shape[1]
    bpt = tile // SLC_BLOCK
    low = _lane_is_low()
    row = lax.broadcasted_iota(jnp.int32, (tile, tile), 0)
    col = lax.broadcasted_iota(jnp.int32, (tile, tile), 1)
    relpos = lax.broadcasted_iota(jnp.int32, (1, tile), 1).astype(F32)
    sel = sel_ref[...]
    blk_of_key = lax.broadcasted_iota(jnp.int32, (ns, tile), 1) // SLC_BLOCK
    blk_row = lax.broadcasted_iota(jnp.int32, (ns, tile), 0)
    qh = []
    for pair in range(nh // 2):
        qh.extend(_split_pair(q_ref[:, pair * PAIR:(pair + 1) * PAIR]))
    slopes = [_group_slope(g, r) for r in range(nh)]
    _flash_reset(m_ref, acc_ref)

    def selected(j):
        expand = jnp.where(blk_row == blk_of_key + j * bpt, 1.0, 0.0).astype(BF16)
        return _dot(sel, expand) > 0.5

    def attend(slot, k_keys, v_keys, rel, bias, r0=0, nr=tile):
        v_aug = _with_ones(v_keys)
        rows = pl.ds(r0, nr)
        for r in range(nh):
            s = _dot_nt(qh[r][r0:r0 + nr], k_keys) + slopes[r] * rel + bias
            _flash_step(s, v_aug, m_ref.at[slot + r, rows], acc_ref.at[slot + r, rows])

    def sel_step(j, keep):
        ks = pl.multiple_of(j * tile, tile)
        attend(0, ks_ref[pl.ds(ks, tile), :], vs_ref[pl.ds(ks, tile), :],
               relpos + ((j - i) * tile).astype(F32), jnp.where(keep, 0.0, NEG_INF))

    half = tile // 2
    ks_i = pl.multiple_of(i * tile, tile)
    bias_diag = jnp.where(selected(i) & (col <= row), 0.0, NEG_INF)
    attend(0, ks_ref[pl.ds(ks_i, half), :], vs_ref[pl.ds(ks_i, half), :], relpos[:, :half],
           bias_diag[:half, :half], 0, half)
    attend(0, ks_ref[pl.ds(ks_i, tile), :], vs_ref[pl.ds(ks_i, tile), :], relpos, bias_diag[half:, :], half, half)

    def sel_body(j, carry):
        @pl.when(flags_ref[((b * NSA_GROUPS + g) * nq + i) * nq + j] != 0)
        def _():
            sel_step(j, selected(j))
        return carry

    lax.fori_loop(0, i, sel_body, 0)

    span = tile + half
    k_win = jnp.concatenate([kwp_ref[...], kwd_ref[...]], axis=0)
    v_win = jnp.concatenate([vwp_ref[...], vwd_ref[...]], axis=0)
    for r0 in (0, half):
        row2 = lax.broadcasted_iota(jnp.int32, (half, span), 0) + r0
        col2 = lax.broadcasted_iota(jnp.int32, (half, span), 1) + r0
        in_window = ((col2 >= tile) & (col2 - tile <= row2)) | ((col2 < tile) & (col2 > row2) & (i > 0))
        rel2 = (lax.broadcasted_iota(jnp.int32, (1, span), 1) + (r0 - tile)).astype(F32)
        attend(nh, k_win[r0:r0 + span], v_win[r0:r0 + span], rel2, jnp.where(in_window, 0.0, NEG_INF), r0, half)

    gates = jax.nn.sigmoid(gate_ref[...])

    def gate_col(r, branch):
        c0 = FOX_HEADS + 3 * r + branch
        c1 = c0 + 3 * nh
        return jnp.where(g == 0, gates[:, c0:c0 + 1], gates[:, c1:c1 + 1])

    for pair in range(nh // 2):
        o_c = oc_ref[:, pair * PAIR:(pair + 1) * PAIR]
        outs = []
        for h in range(2):
            r = 2 * pair + h
            o_s = _flash_out(acc_ref[r])
            o_w = _flash_out(acc_ref[nh + r])
            outs.append(gate_col(r, 0) * o_c + gate_col(r, 1) * o_s + gate_col(r, 2) * o_w)
        o_ref[:, pair * PAIR:(pair + 1) * PAIR] = jnp.where(low, outs[0], outs[1]).astype(o_ref.dtype)


def _nsa_attention(flags, pj, sel, o_c, small, *, batch, seq, tile, q_blk, ks_blk, vs_blk, kw_blk, vw_blk):
    nq = seq // tile
    ns = sel.shape[3]
    gw = NSA_GROUP_SIZE * HEAD_DIM
    slots = 2 * NSA_GROUP_SIZE
    cur = lambda b, g, i, f, base: (b * nq + i, base + g)
    prev = lambda b, g, i, f, base: (b * nq + jnp.maximum(i - 1, 0), base + g)
    grid_spec = pltpu.PrefetchScalarGridSpec(
        num_scalar_prefetch=1,
        grid=(batch, NSA_GROUPS, nq),
        in_specs=[pl.BlockSpec((tile, gw), functools.partial(cur, base=q_blk)),
                  pl.BlockSpec((seq, PAIR), lambda b, g, i, f: (b, ks_blk + g)),
                  pl.BlockSpec((seq, PAIR), lambda b, g, i, f: (b, vs_blk + g)),
                  pl.BlockSpec((tile, PAIR), functools.partial(prev, base=kw_blk)),
                  pl.BlockSpec((tile, PAIR), functools.partial(prev, base=vw_blk)),
                  pl.BlockSpec((tile, PAIR), functools.partial(cur, base=kw_blk)),
                  pl.BlockSpec((tile, PAIR), functools.partial(cur, base=vw_blk)),
                  pl.BlockSpec((None, None, tile, ns), lambda b, g, i, f: (b, g, i, 0)),
                  pl.BlockSpec((tile, gw), functools.partial(cur, base=0)),
                  pl.BlockSpec((tile, small.shape[1]), lambda b, g, i, f: (b * nq + i, 0))],
        out_specs=pl.BlockSpec((tile, gw), functools.partial(cur, base=0)),
        scratch_shapes=[pltpu.VMEM((slots, tile, LANES), F32), pltpu.VMEM((slots, tile, 2 * PAIR), F32)])
    return pl.pallas_call(
        functools.partial(_nsa_kernel, tile=tile, nq=nq),
        out_shape=jax.ShapeDtypeStruct((batch * seq, NSA_HEADS * HEAD_DIM), BF16),
        grid_spec=grid_spec,
        compiler_params=_cparams("parallel", "parallel", "arbitrary"),
        name="nsa_attention",
    )(flags, pj, pj, pj, pj, pj, pj, pj, sel, o_c, small)


def _ffn_kernel(h_ref, gin_ref, gout_ref, wg_ref, wu_ref, wd_ref, *rest, chunk, mixer, tm, tiles_per_seq):
    h = h_ref[...]
    if mixer == "attn":
        a_ref, b_ref, wa_ref, wb_ref, gmix_ref, o_ref, acc_ref = rest
        h = h + _rms(_dot(a_ref[...], wa_ref[...]) + _dot(b_ref[...], wb_ref[...]), gmix_ref[...])
    else:
        halo_ref, pin_ref, pout_ref, pw_ref, sc_ref, o_ref, acc_ref, ext_ref = rest
        h = _pool_mix(h, halo_ref, pin_ref, pout_ref, pw_ref, sc_ref, ext_ref, tm=tm, tiles_per_seq=tiles_per_seq)
    ub = _rms(h, gin_ref[...]).astype(BF16)
    hidden = wg_ref.shape[1]
    for idx, c in enumerate(range(0, hidden, chunk)):
        gate = _dot(ub, wg_ref[:, c:c + chunk])
        up = _dot(ub, wu_ref[:, c:c + chunk])
        act = (gate * jax.nn.sigmoid(gate) * up).astype(BF16)
        part = _dot(act, wd_ref[c:c + chunk, :])
        if idx == 0:
            acc_ref[...] = part
        else:
            acc_ref[...] += part
    o_ref[...] = h + _rms(acc_ref[...], gout_ref[...])


def _layer_tail(h2, g_in, g_out, wg, wu, wd, *, seq, tm, chunk, mixer, mix):
    n, d = h2.shape
    const = lambda i: (0, 0)
    rows = lambda i: (i, 0)
    scratch = [pltpu.VMEM((tm, d), F32)]
    if mixer == "attn":
        a, b, wa, wb, g_mix = mix
        mix_specs = [pl.BlockSpec((tm, a.shape[1]), rows), pl.BlockSpec((tm, b.shape[1]), rows),
                     pl.BlockSpec(wa.shape, const), pl.BlockSpec(wb.shape, const), pl.BlockSpec((1, d), const)]
    else:
        p_in, p_out, w_groups, scale = mix
        ratio = tm // POOL_HALO
        mix = (h2, p_in, p_out, w_groups, scale)
        mix_specs = [pl.BlockSpec((POOL_HALO, d), lambda i: (jnp.maximum(i * ratio - 1, 0), 0)),
                     pl.BlockSpec((1, d), const), pl.BlockSpec((1, d), const),
                     pl.BlockSpec(w_groups.shape, lambda i: (0, 0, 0)), pl.BlockSpec((1, d), const)]
        scratch.append(pltpu.VMEM((tm + POOL_HALO, d), F32))
    return pl.pallas_call(
        functools.partial(_ffn_kernel, chunk=chunk, mixer=mixer, tm=tm, tiles_per_seq=seq // tm),
        out_shape=jax.ShapeDtypeStruct((n, d), F32),
        grid=(n // tm,),
        in_specs=[pl.BlockSpec((tm, d), rows),
                  pl.BlockSpec((1, d), const),
                  pl.BlockSpec((1, d), const),
                  pl.BlockSpec(wg.shape, const),
                  pl.BlockSpec(wu.shape, const),
                  pl.BlockSpec(wd.shape, const)] + mix_specs,
        out_specs=pl.BlockSpec((tm, d), rows),
        scratch_shapes=scratch,
        compiler_params=_cparams("parallel"),
        name=f"{mixer}_tail_swiglu_ffn",
    )(h2, g_in, g_out, wg, wu, wd, *mix)


def _pool_mix(h, halo_ref, gin_ref, gout_ref, w_ref, sc_ref, ext_ref, *, tm, tiles_per_seq):
    halo = POOL_HALO
    i = pl.program_id(0)
    first = (i % tiles_per_seq) == 0
    u = _rms(h, gin_ref[...])
    uh = _rms(halo_ref[...], gin_ref[...])
    ext_ref[0:halo, :] = jnp.where(first, 0.0, uh)
    ext_ref[halo:, :] = u
    t = (i % tiles_per_seq) * tm + lax.broadcasted_iota(jnp.int32, (tm, 1), 0)
    group = h.shape[1] // len(POOL_WINDOWS)
    ys = []
    for gi, w in enumerate(POOL_WINDOWS):
        cols = slice(gi * group, (gi + 1) * group)
        run = ext_ref[:, cols]
        span = 1
        while span < w:
            run = run + pltpu.roll(run, span, axis=0)
            span *= 2
        total = run[halo:, :]
        count = jnp.minimum(t + 1, w).astype(F32)
        pooled = total / count - u[:, cols]
        ys.append(_dot(pooled.astype(BF16), w_ref[gi]))
    y = jnp.concatenate(ys, axis=-1) * sc_ref[...]
    return h + _rms(y, gout_ref[...])


def _pack_in_weights(w_in):
    d = w_in.shape[0]
    hw = FOX_HEADS * HEAD_DIM
    kv = NSA_GROUPS * HEAD_DIM
    sizes = (hw, hw, hw, FOX_HEADS, NSA_HEADS * HEAD_DIM, kv, kv, kv, kv, kv, kv, 3 * NSA_HEADS)
    offs = np.concatenate([[0], np.cumsum(sizes)])
    fq, fk, fv, ff, nq, kc, vc, ks, vs, kw, vw, ng = [w_in[:, offs[j]:offs[j + 1]] for j in range(len(sizes))]
    scale = HEAD_DIM ** -0.5 * LOG2E

    def dup(w):
        w = w.reshape(d, NSA_GROUPS, 1, HEAD_DIM)
        return jnp.broadcast_to(w, (d, NSA_GROUPS, 2, HEAD_DIM)).reshape(d, NSA_GROUPS * PAIR)

    w_main = jnp.concatenate([fq * scale, fk, fv, nq * scale, dup(ks), dup(vs), dup(kw), dup(vw)], axis=1)
    w_cmp = jnp.concatenate([kc, vc], axis=1)
    pad = jnp.zeros((d, LANES - FOX_HEADS - 3 * NSA_HEADS), F32)
    w_small = jnp.concatenate([ff, ng, pad], axis=1)
    w_small_hi = w_small.astype(BF16)
    w_small_lo = (w_small - w_small_hi.astype(F32)).astype(BF16)
    return w_main.astype(BF16), w_cmp.astype(BF16), jnp.concatenate([w_small_hi, w_small_lo], axis=1)


def _mixer_fox_nsa(h2, g_in, g_out, w_in, b_f, cmp_pe, cmp_w1, cmp_w2, w_out, *, batch, seq):
    tile = WINDOW
    assert seq % tile == 0 and seq // tile <= LANES
    w_main, w_cmp, w_small = _pack_in_weights(w_in)
    pj, pc, small, nrm = _inproj(h2, g_in, w_main, w_cmp, w_small, tm=tile)
    npair = FOX_HEADS // 2
    q_blk, k_blk, v_blk, nq_blk = 0, npair, 2 * npair, 3 * npair
    ks_blk = nq_blk + NSA_HEADS // 2
    vs_blk, kw_blk, vw_blk = ks_blk + NSA_GROUPS, ks_blk + 2 * NSA_GROUPS, ks_blk + 3 * NSA_GROUPS

    c = _forget_cumsum(small, b_f, batch=batch, seq=seq, tc=tile)
    nq = seq // tile
    fox_tq, fox_tk = min(FOX_TQ, seq), min(FOX_TK, seq)
    c_tiles = c.reshape(batch, npair, 2, seq // fox_tk, fox_tk).transpose(0, 1, 3, 2, 4)
    live = _fox_live_tiles(c, nrm, batch=batch, seq=seq, tq=fox_tq, tk=fox_tk, tm=tile)
    o_fox = _fox_attention(live, pj, c_tiles, batch=batch, seq=seq, tq=fox_tq, tk=fox_tk,
                           q_blk=q_blk, k_blk=k_blk, v_blk=v_blk)

    nch = seq // CMP_STRIDE
    ns = seq // SLC_BLOCK
    n_sel = min(N_SELECT, ns)
    kv = NSA_GROUPS * HEAD_DIM
    cmp_kv, cmp_nrm = _compress(pc, cmp_pe, cmp_w1, cmp_w2, batch=batch, seq=seq)

    group_slope = jnp.asarray([min(ALIBI_SLOPES_LOG2[g * NSA_GROUP_SIZE:(g + 1) * NSA_GROUP_SIZE])
                               for g in range(NSA_GROUPS)], F32)
    qn = nrm[:, NRM_NSA_Q, :NSA_HEADS].reshape(batch, nq, NSA_GROUPS, NSA_GROUP_SIZE).max(axis=3)
    ksn = nrm[:, NRM_NSA_KS, 0:2 * NSA_GROUPS:2].reshape(batch, nq, NSA_GROUPS).max(axis=1)
    kcn = cmp_nrm[:, 0, :, 0, 0]
    tiles = jnp.arange(nq)
    nseg = CMP_SEGMENTS if nq % CMP_SEGMENTS == 0 else 1
    chunk = nch // nseg
    last_end = (jnp.arange(nseg) + 1) * chunk * CMP_STRIDE + (CMP_BLOCK - 1 - CMP_STRIDE)
    dist_c = (tiles[:, None] * tile - last_end[None, :] - (CMP_STRIDE - 1)).astype(F32)
    dead_c = _alibi_dead(qn[..., None], kcn[:, None, :, None], dist_c[None, :, None, :],
                         group_slope[None, None, :, None])
    first_chunk = jnp.minimum(dead_c.sum(axis=3), (tiles * nseg // nq)[None, :, None])
    first_chunk = first_chunk.transpose(0, 2, 1).astype(jnp.int32).reshape(-1)
    dist_s = ((tiles[:, None] - tiles[None, :] - 1) * tile + 1).astype(F32)
    dead_s = _alibi_dead(qn[..., None], ksn[:, None, :, None], dist_s[None, :, None, :],
                         group_slope[None, None, :, None])
    live_s = jnp.logical_not(dead_s).transpose(0, 2, 1, 3).astype(jnp.int32)

    ratio = SLC_BLOCK // CMP_STRIDE
    r = CMP_BLOCK // CMP_STRIDE
    n_idx = np.arange(nch)[:, None]
    b_idx = np.arange(ns)[None, :]
    pool_np = ((n_idx >= ratio * b_idx - (r - 1)) & (n_idx <= ratio * b_idx + ratio - 1)
               & (n_idx < nch - r + 1)).astype(np.float32)
    pool = jnp.asarray(pool_np, dtype=BF16)
    tmap_np = (np.arange(ns)[:, None] // (tile // SLC_BLOCK) == np.arange(LANES)[None, :]).astype(np.float32)
    o_c, sel, flag_blocks = _cmp_select(first_chunk, pj, cmp_kv, pool, jnp.asarray(tmap_np, dtype=BF16),
                                        batch=batch, seq=seq, tq=tile, q_blk=nq_blk // 2, n_sel=n_sel, nseg=nseg)
    flags = (flag_blocks[:, :, :, 0, :nq] * live_s).reshape(-1)
    o_nsa = _nsa_attention(flags, pj, sel, o_c, small, batch=batch, seq=seq, tile=tile,
                           q_blk=nq_blk // 2, ks_blk=ks_blk, vs_blk=vs_blk, kw_blk=kw_blk, vw_blk=vw_blk)

    hw = FOX_HEADS * HEAD_DIM
    return o_fox, o_nsa, w_out[:hw].astype(BF16), w_out[hw:].astype(BF16), g_out


def kernel(x, norm_g, attn_w_in, fox_b_f, nsa_cmp_pe, nsa_cmp_w1, nsa_cmp_w2, attn_w_out,
           pool_w, pool_scale, ffn_w_gate, ffn_w_up, ffn_w_down):
    batch, seq, d = x.shape
    depth = norm_g.shape[0]
    tm = 512 if seq % 512 == 0 else seq
    h = x.reshape(batch * seq, d)
    for layer in range(depth):
        g = norm_g[layer].reshape(4, 1, d)
        i = layer // 2
        if layer % 2 == 0:
            mixer = "attn"
            mix = _mixer_fox_nsa(h, g[0], g[1], attn_w_in[i], fox_b_f[i], nsa_cmp_pe[i], nsa_cmp_w1[i],
                                 nsa_cmp_w2[i], attn_w_out[i], batch=batch, seq=seq)
        else:
            mixer = "pool"
            mix = (g[0], g[1], pool_w[i].astype(BF16), pool_scale[i].reshape(1, d))
        h = _layer_tail(h, g[2], g[3], ffn_w_gate[layer].astype(BF16), ffn_w_up[layer].astype(BF16),
                        ffn_w_down[layer].astype(BF16), seq=seq, tm=tm, chunk=256, mixer=mixer, mix=mix)
    return h.reshape(batch, seq, d)
```

```python
import functools

import numpy as np
import jax
import jax.numpy as jnp
from jax import lax
from jax.experimental import pallas as pl
from jax.experimental.pallas import tpu as pltpu

F32 = jnp.float32
BF16 = jnp.bfloat16

HEAD_DIM = 64
FOX_HEADS = 8
NSA_HEADS = 8
NSA_GROUPS = 2
NSA_GROUP_SIZE = NSA_HEADS // NSA_GROUPS
CMP_BLOCK = 32
CMP_STRIDE = 16
SLC_BLOCK = 64
N_SELECT = 16
WINDOW = 512
POOL_WINDOWS = (2, 4, 8, 16)
POOL_HALO = max(POOL_WINDOWS)
RMS_EPS = 1e-6
NEG_INF = -1e30
BIG = 1e30
REMOVED = -3e38
LANES = 128
PAIR = 2 * HEAD_DIM
VMEM_LIMIT = 56 * 1024 * 1024
CMP_SEGMENTS = 4
UNDERFLOW_BITS = 160.0
NORM_SLACK = 1.02
NORM_CHUNKS = (0, 1, 3, 4)
NRM_FOX_Q, NRM_FOX_K, NRM_NSA_Q, NRM_NSA_KS = range(4)
FFN_CHUNK = 256
FOX_TQ, FOX_TK = 1024, 1024

LOG2E = 1.4426950408889634
ALIBI_SLOPES_LOG2 = tuple(float(2.0 ** (-8.0 * (i + 1.0) / NSA_HEADS)) * LOG2E for i in range(NSA_HEADS))


def _cparams(*sem):
    return pltpu.CompilerParams(dimension_semantics=sem, vmem_limit_bytes=VMEM_LIMIT)


def _rms(x, g):
    return x * lax.rsqrt(jnp.mean(x * x, axis=-1, keepdims=True) + RMS_EPS) * g


def _dot(a, b):
    return jnp.dot(a, b, preferred_element_type=F32)


def _dot_nt(a, b):
    return lax.dot_general(a, b, (((1,), (1,)), ((), ())), preferred_element_type=F32)


def _inproj_kernel(x_ref, g_ref, w_ref, wc_ref, ws_ref, hmap_ref, o_ref, oc_ref, os_ref, nrm_ref, *, fox_width):
    u = _rms(x_ref[...], g_ref[...])
    ub = u.astype(BF16)
    norms = []

    def max_sq_norms(r, ncols):
        rf = r.astype(F32)
        sq = _dot((rf * rf).astype(BF16), hmap_ref[0:ncols, :])
        norms.append(jnp.max(sq, axis=0, keepdims=True))

    heads_width = 4 * fox_width
    for c in range(0, heads_width, fox_width):
        r = _dot(ub, w_ref[:, c:c + fox_width]).astype(BF16)
        o_ref[:, c:c + fox_width] = r
        if c // fox_width in NORM_CHUNKS:
            max_sq_norms(r, fox_width)
    kv = _dot(ub, w_ref[:, heads_width:])
    max_sq_norms(kv[:, :PAIR].astype(BF16), PAIR)
    low = _lane_is_low()
    for blk in range(kv.shape[1] // PAIR):
        x = kv[:, blk * PAIR:(blk + 1) * PAIR]
        swapped = pltpu.roll(x, HEAD_DIM, axis=1)
        base = heads_width + 2 * blk * PAIR
        o_ref[:, base:base + PAIR] = jnp.where(low, x, swapped).astype(BF16)
        o_ref[:, base + PAIR:base + 2 * PAIR] = jnp.where(low, swapped, x).astype(BF16)
    oc_ref[...] = _dot(ub, wc_ref[...])
    width = os_ref.shape[1]
    u_lo = (u - ub.astype(F32)).astype(BF16)
    both = _dot(ub, ws_ref[...])
    os_ref[...] = both[:, :width] + both[:, width:] + _dot(u_lo, ws_ref[:, :width])
    row = lax.broadcasted_iota(jnp.int32, nrm_ref.shape, 0)
    out = jnp.zeros(nrm_ref.shape, F32)
    for k, nk in enumerate(norms):
        out = jnp.where(row == k, nk, out)
    nrm_ref[...] = out


def _inproj(x2, g, w_main, w_cmp, w_small, *, tm):
    n, d = x2.shape
    cin, cc, cs = w_main.shape[1], w_cmp.shape[1], w_small.shape[1] // 2
    fox_width = FOX_HEADS * HEAD_DIM
    cm = 4 * fox_width + 2 * (cin - 4 * fox_width)
    hmap = jnp.asarray((np.arange(fox_width)[:, None] // HEAD_DIM == np.arange(LANES)[None, :]).astype(np.float32),
                       dtype=BF16)
    const = lambda i: (0, 0)
    return pl.pallas_call(
        functools.partial(_inproj_kernel, fox_width=fox_width),
        out_shape=(jax.ShapeDtypeStruct((n, cm), BF16),
                   jax.ShapeDtypeStruct((n, cc), F32),
                   jax.ShapeDtypeStruct((n, cs), F32),
                   jax.ShapeDtypeStruct((n // tm, 8, LANES), F32)),
        grid=(n // tm,),
        in_specs=[pl.BlockSpec((tm, d), lambda i: (i, 0)),
                  pl.BlockSpec((1, d), const),
                  pl.BlockSpec((d, cin), const),
                  pl.BlockSpec((d, cc), const),
                  pl.BlockSpec((d, 2 * cs), const),
                  pl.BlockSpec((fox_width, LANES), const)],
        out_specs=(pl.BlockSpec((tm, cm), lambda i: (i, 0)),
                   pl.BlockSpec((tm, cc), lambda i: (i, 0)),
                   pl.BlockSpec((tm, cs), lambda i: (i, 0)),
                   pl.BlockSpec((None, 8, LANES), lambda i: (i, 0, 0))),
        compiler_params=_cparams("parallel"),
        name="inproj",
    )(x2, g, w_main, w_cmp, w_small, hmap)


def _forget_cumsum_kernel(sm_ref, bf_ref, tri_ref, o_ref, carry_ref):
    @pl.when(pl.program_id(1) == 0)
    def _():
        carry_ref[...] = jnp.zeros_like(carry_ref)

    zt = sm_ref[...].T
    z = zt[0:FOX_HEADS, :] + bf_ref[...]
    log_f = jnp.minimum(z, 0.0) - jnp.log1p(jnp.exp(-jnp.abs(z)))
    c = jnp.dot(log_f, tri_ref[...], preferred_element_type=F32,
                precision=lax.Precision.HIGHEST) + carry_ref[...]
    o_ref[...] = c * LOG2E
    carry_ref[...] = c[:, c.shape[1] - 1:]


def _forget_cumsum(small, b_f, *, batch, seq, tc):
    tri = jnp.asarray(np.triu(np.ones((tc, tc), np.float32)))
    nchunk = seq // tc
    return pl.pallas_call(
        _forget_cumsum_kernel,
        out_shape=jax.ShapeDtypeStruct((batch, FOX_HEADS, seq), F32),
        grid=(batch, nchunk),
        in_specs=[pl.BlockSpec((tc, small.shape[1]), lambda b, j: (b * nchunk + j, 0)),
                  pl.BlockSpec((FOX_HEADS, 1), lambda b, j: (0, 0)),
                  pl.BlockSpec((tc, tc), lambda b, j: (0, 0))],
        out_specs=pl.BlockSpec((None, FOX_HEADS, tc), lambda b, j: (b, 0, j)),
        scratch_shapes=[pltpu.VMEM((FOX_HEADS, 1), F32)],
        compiler_params=_cparams("parallel", "arbitrary"),
        name="forget_cumsum",
    )(small, b_f.reshape(FOX_HEADS, 1), tri)


def _lane_is_low():
    return lax.broadcasted_iota(jnp.int32, (1, PAIR), 1) < HEAD_DIM


def _with_ones(v_tile):
    return jnp.concatenate([v_tile, jnp.ones_like(v_tile)], axis=1)


def _flash_step(s, v, m_ref, acc_ref, l_ref=None):
    m_old = m_ref[...]
    m_new = jnp.maximum(m_old, jnp.max(s, axis=-1, keepdims=True))
    alpha = jnp.exp2(m_old - m_new)
    p = jnp.exp2(s - jnp.concatenate([m_new] * (s.shape[1] // LANES), axis=1))
    if l_ref is None:
        alpha_acc = jnp.concatenate([alpha, alpha], axis=1)
    else:
        alpha_acc = alpha
        l_ref[...] = alpha * l_ref[...] + jnp.sum(p, axis=-1, keepdims=True)
    acc_ref[...] = alpha_acc * acc_ref[...] + _dot(p.astype(BF16), v)
    m_ref[...] = m_new


def _flash_reset(m_ref, *sum_refs):
    m_ref[...] = jnp.full_like(m_ref, NEG_INF)
    for ref in sum_refs:
        ref[...] = jnp.zeros_like(ref)


def _flash_out(acc):
    return acc[:, :PAIR] / acc[:, PAIR:]


def _split_pair(q):
    low = _lane_is_low()
    zero = jnp.zeros_like(q)
    return jnp.where(low, q, zero), jnp.where(low, zero, q)


def _fox_kernel(live_ref, q_ref, k_ref, v_ref, c_ref, o_ref, m_ref, l_ref, acc_ref, *, tq, tk, nq):
    b = pl.program_id(0)
    pair = pl.program_id(1)
    i = pl.program_id(2)
    nk = c_ref.shape[0]
    qh = _split_pair(q_ref[...])
    _flash_reset(m_ref, l_ref, acc_ref)

    def head_step(h, j, mask):
        ks = pl.multiple_of(j * tk, tk)
        s = _dot_nt(qh[h], k_ref[pl.ds(ks, tk), :]) - c_ref[j, h:h + 1, :]
        if mask is not None:
            s = jnp.where(mask, s, NEG_INF)
        _flash_step(s, v_ref[pl.ds(ks, tk), :], m_ref.at[h], acc_ref.at[h], l_ref.at[h])

    def body(j, carry):
        live = [live_ref[(((b * FOX_HEADS) + 2 * pair + h) * nq + i) * nk + j] != 0 for h in range(2)]

        @pl.when(live[0] & live[1])
        def _():
            head_step(0, j, None)
            head_step(1, j, None)

        for h in range(2):
            @pl.when(live[h] & jnp.logical_not(live[1 - h]))
            def _(h=h):
                head_step(h, j, None)
        return carry

    lax.fori_loop(0, i, body, 0)

    def diag_part(h, r0, nr, nkeys):
        ks = pl.multiple_of(i * tq, tq)
        rows = pl.ds(r0, nr)
        row = lax.broadcasted_iota(jnp.int32, (nr, nkeys), 0) + r0
        col = lax.broadcasted_iota(jnp.int32, (nr, nkeys), 1)
        s = _dot_nt(qh[h][r0:r0 + nr], k_ref[pl.ds(ks, nkeys), :]) - c_ref[i, h:h + 1, 0:nkeys]
        _flash_step(jnp.where(col <= row, s, NEG_INF), v_ref[pl.ds(ks, nkeys), :],
                    m_ref.at[h, rows], acc_ref.at[h, rows], l_ref.at[h, rows])

    half = tq // 2
    for h in range(2):
        diag_part(h, 0, half, half)
        diag_part(h, half, half, tq)
    o_ref[...] = jnp.where(_lane_is_low(), acc_ref[0] / l_ref[0], acc_ref[1] / l_ref[1]).astype(o_ref.dtype)


def _fox_live_tiles(c2, nrm, *, batch, seq, tq, tk, tm):
    nq, nk = seq // tq, seq // tk
    qn = nrm[:, 0, :FOX_HEADS].reshape(batch, nq, tq // tm, FOX_HEADS).max(axis=2)
    kn = nrm[:, 1, :FOX_HEADS].reshape(batch, seq // tm, FOX_HEADS).max(axis=1)
    bound = 2.0 * NORM_SLACK * jnp.sqrt(qn * kn[:, None, :])
    c_start = c2[:, :, ::tq]
    c_end = c2[:, :, tk - 1::tk]
    decay = c_end[:, :, None, :] - c_start[:, :, :, None]
    live = decay < UNDERFLOW_BITS + jnp.transpose(bound, (0, 2, 1))[..., None]
    return live.astype(jnp.int32).reshape(-1)


def _fox_attention(live, pj, c_tiles, *, batch, seq, tq, tk, q_blk, k_blk, v_blk):
    assert tq == tk and tq % (2 * LANES) == 0
    nq = seq // tq
    nk = seq // tk
    npair = FOX_HEADS // 2
    grid_spec = pltpu.PrefetchScalarGridSpec(
        num_scalar_prefetch=1,
        grid=(batch, npair, nq),
        in_specs=[pl.BlockSpec((tq, PAIR), lambda b, p, i, f: (b * nq + i, q_blk + p)),
                  pl.BlockSpec((seq, PAIR), lambda b, p, i, f: (b, k_blk + p)),
                  pl.BlockSpec((seq, PAIR), lambda b, p, i, f: (b, v_blk + p)),
                  pl.BlockSpec((None, None, nk, 2, tk), lambda b, p, i, f: (b, p, 0, 0, 0))],
        out_specs=pl.BlockSpec((tq, PAIR), lambda b, p, i, f: (b * nq + i, p)),
        scratch_shapes=[pltpu.VMEM((2, tq, LANES), F32), pltpu.VMEM((2, tq, LANES), F32),
                        pltpu.VMEM((2, tq, PAIR), F32)])
    return pl.pallas_call(
        functools.partial(_fox_kernel, tq=tq, tk=tk, nq=nq),
        out_shape=jax.ShapeDtypeStruct((batch * seq, FOX_HEADS * HEAD_DIM), BF16),
        grid_spec=grid_spec,
        compiler_params=_cparams("parallel", "parallel", "arbitrary"),
        name="fox_attention",
    )(live, pj, pj, pj, c_tiles)


def _gelu_tanh(x):
    return 0.5 * x * (1.0 + jnp.tanh(0.7978845608028654 * (x + 0.044715 * x * x * x)))


def _compress_kernel(x_ref, pe_ref, w1pos_ref, w1dup_ref, w2sel_ref, o_ref, nrm_ref):
    nch = o_ref.shape[1]
    a = b = None
    for j in range(CMP_STRIDE):
        xj = x_ref[pl.ds(j, nch, stride=CMP_STRIDE), :].astype(BF16)
        pa, pb = _dot(xj, w1pos_ref[j]), _dot(xj, w1pos_ref[CMP_STRIDE + j])
        a, b = (pa, pb) if a is None else (a + pa, b + pb)
    pe_term = _dot(pe_ref[...], w1dup_ref[...])[0:1, :]
    h = a + pltpu.roll(b, nch - 1, axis=0) + pe_term
    y = _gelu_tanh(h).astype(BF16)
    for g in range(NSA_GROUPS):
        o = _dot(y, w2sel_ref[g]).astype(o_ref.dtype)
        o_ref[g] = o
        of = o.astype(F32)
        sq = 0.5 * jnp.sum(of * of, axis=-1, keepdims=True)
        nrm_ref[g] = jnp.broadcast_to(jnp.max(sq, axis=0, keepdims=True), nrm_ref.shape[1:])


def _compress(pc, cmp_pe, cmp_w1, cmp_w2, *, batch, seq):
    nch = seq // CMP_STRIDE
    dh = HEAD_DIM
    zero = jnp.zeros((2, CMP_BLOCK, dh, dh), F32)
    w1r = cmp_w1.reshape(2, CMP_BLOCK, dh, dh)
    w1pos = jnp.concatenate([jnp.concatenate([w1r, zero], axis=-1),
                             jnp.concatenate([zero, w1r], axis=-1)], axis=-2).astype(BF16)
    w1dup = jnp.concatenate([cmp_w1, cmp_w1], axis=-1).astype(BF16)
    w2dup = jnp.concatenate([cmp_w2, cmp_w2], axis=-1)
    zero2 = jnp.zeros_like(w2dup)
    w2sel = jnp.stack([jnp.concatenate([w2dup, zero2], axis=1),
                       jnp.concatenate([zero2, w2dup], axis=1)], axis=1).astype(BF16)
    pe8 = jnp.broadcast_to(cmp_pe.reshape(2, 1, CMP_BLOCK * dh), (2, 8, CMP_BLOCK * dh)).astype(BF16)
    return pl.pallas_call(
        _compress_kernel,
        out_shape=(jax.ShapeDtypeStruct((batch, 2, NSA_GROUPS, nch, PAIR), BF16),
                   jax.ShapeDtypeStruct((batch, 2, NSA_GROUPS, 8, LANES), F32)),
        grid=(batch, 2),
        in_specs=[pl.BlockSpec((seq, PAIR), lambda b, kv: (b, kv)),
                  pl.BlockSpec((None, 8, CMP_BLOCK * dh), lambda b, kv: (kv, 0, 0)),
                  pl.BlockSpec((None, CMP_BLOCK, PAIR, PAIR), lambda b, kv: (kv, 0, 0, 0)),
                  pl.BlockSpec((None, CMP_BLOCK * dh, PAIR), lambda b, kv: (kv, 0, 0)),
                  pl.BlockSpec((None, NSA_GROUPS, PAIR, PAIR), lambda b, kv: (kv, 0, 0, 0))],
        out_specs=(pl.BlockSpec((None, None, NSA_GROUPS, nch, PAIR), lambda b, kv: (b, kv, 0, 0, 0)),
                   pl.BlockSpec((None, None, NSA_GROUPS, 8, LANES), lambda b, kv: (b, kv, 0, 0, 0))),
        compiler_params=_cparams("parallel", "parallel"),
        name="nsa_compress",
    )(pc, pe8, w1pos, w1dup, w2sel)


def _group_slope(g, r):
    return jnp.where(g == 0, ALIBI_SLOPES_LOG2[r], ALIBI_SLOPES_LOG2[NSA_GROUP_SIZE + r]).astype(F32)


def _split3(x):
    hi = x.astype(BF16)
    r1 = x - hi.astype(F32)
    mid = r1.astype(BF16)
    lo = (r1 - mid.astype(F32)).astype(BF16)
    return hi, mid, lo


def _cmp_select_kernel(first_ref, q_ref, kc_ref, vc_ref, pool_ref, tmap_ref, oc_ref, sel_ref, flag_ref, *,
                       tq, n_sel, nseg, nq):
    b = pl.program_id(0)
    g = pl.program_id(1)
    i = pl.program_id(2)
    chunk = kc_ref.shape[0] // nseg
    first = first_ref[(b * NSA_GROUPS + g) * nq + i]
    n_live = (i * nseg) // nq + 1 - first
    for k in range(1, nseg + 1):
        @pl.when(n_live == k)
        def _(k=k):
            _cmp_select_body(q_ref, kc_ref, vc_ref, pool_ref, tmap_ref, oc_ref, sel_ref, flag_ref,
                             tq=tq, n_sel=n_sel, start=pl.multiple_of(first * chunk, chunk), nch=k * chunk)


def _cmp_select_body(q_ref, kc_ref, vc_ref, pool_ref, tmap_ref, oc_ref, sel_ref, flag_ref, *,
                     tq, n_sel, start, nch):
    g = pl.program_id(1)
    t0 = pl.program_id(2) * tq
    ns = pool_ref.shape[1]
    low = _lane_is_low()
    t = t0 + lax.broadcasted_iota(jnp.int32, (tq, 1), 0)
    cmp_end = (start + lax.broadcasted_iota(jnp.int32, (1, nch), 1)) * CMP_STRIDE + (CMP_BLOCK - 1)
    rel_end = (cmp_end - t0).astype(F32)
    mask_bias = jnp.where(t >= cmp_end, 0.0, NEG_INF)
    row_valid = jnp.where(t >= CMP_BLOCK - 1, 1.0, 0.0)
    kc = kc_ref[pl.ds(start, nch), :]
    vc = vc_ref[pl.ds(start, nch), :]
    imp = jnp.zeros((tq, nch), F32)
    for pair in range(NSA_GROUP_SIZE // 2):
        qh = _split_pair(q_ref[:, pair * PAIR:(pair + 1) * PAIR])
        outs = []
        for h in range(2):
            s = _dot_nt(qh[h], kc) + _group_slope(g, 2 * pair + h) * rel_end + mask_bias
            e = jnp.exp2(s - jnp.max(s, axis=-1, keepdims=True))
            p = e * (row_valid / jnp.maximum(jnp.sum(e, axis=-1, keepdims=True), 1e-30))
            outs.append(_dot(p.astype(BF16), vc))
            imp = imp + p
        oc_ref[:, pair * PAIR:(pair + 1) * PAIR] = jnp.where(low, outs[0], outs[1])

    pool = pool_ref[pl.ds(start, nch), :]
    hi, mid, lo = _split3(imp)
    p_slc = _dot(hi, pool) + _dot(mid, pool) + _dot(lo, pool)

    blk = lax.broadcasted_iota(jnp.int32, (1, ns), 1)
    cur = t // SLC_BLOCK
    forced = (blk == 0) | (blk == cur) | (blk == cur - 1)
    val = jnp.where(forced, REMOVED, jnp.where(blk > cur, NEG_INF, p_slc))
    sel = jnp.where(forced, 1.0, 0.0)
    blk_f = blk.astype(F32)
    for _ in range(n_sel - 3):
        mx = jnp.max(val, axis=-1, keepdims=True)
        pick = jnp.min(jnp.where(val == mx, blk_f, float(ns)), axis=-1, keepdims=True)
        hit = blk_f == pick
        sel = jnp.where(hit, 1.0, sel)
        val = jnp.where(hit, REMOVED, val)
    sel_ref[...] = sel.astype(sel_ref.dtype)
    any_sel = jnp.broadcast_to(jnp.max(sel, axis=0, keepdims=True), (8, ns)).astype(BF16)
    flag_ref[...] = (_dot(any_sel, tmap_ref[...]) > 0.5).astype(jnp.int32)


def _cmp_select(first_chunk, pj, cmp_kv, pool, tmap, *, batch, seq, tq, q_blk, n_sel, nseg):
    nq = seq // tq
    nch = cmp_kv.shape[3]
    ns = pool.shape[1]
    gw = NSA_GROUP_SIZE * HEAD_DIM
    grid_spec = pltpu.PrefetchScalarGridSpec(
        num_scalar_prefetch=1,
        grid=(batch, NSA_GROUPS, nq),
        in_specs=[pl.BlockSpec((tq, gw), lambda b, g, i, f: (b * nq + i, q_blk + g)),
                  pl.BlockSpec((None, None, None, nch, PAIR), lambda b, g, i, f: (b, 0, g, 0, 0)),
                  pl.BlockSpec((None, None, None, nch, PAIR), lambda b, g, i, f: (b, 1, g, 0, 0)),
                  pl.BlockSpec((nch, ns), lambda b, g, i, f: (0, 0)),
                  pl.BlockSpec((ns, LANES), lambda b, g, i, f: (0, 0))],
        out_specs=(pl.BlockSpec((tq, gw), lambda b, g, i, f: (b * nq + i, g)),
                   pl.BlockSpec((None, None, tq, ns), lambda b, g, i, f: (b, g, i, 0)),
                   pl.BlockSpec((None, None, None, 8, LANES), lambda b, g, i, f: (b, g, i, 0, 0))))
    return pl.pallas_call(
        functools.partial(_cmp_select_kernel, tq=tq, n_sel=n_sel, nseg=nseg, nq=nq),
        out_shape=(jax.ShapeDtypeStruct((batch * seq, NSA_HEADS * HEAD_DIM), F32),
                   jax.ShapeDtypeStruct((batch, NSA_GROUPS, seq, ns), BF16),
                   jax.ShapeDtypeStruct((batch, NSA_GROUPS, nq, 8, LANES), jnp.int32)),
        grid_spec=grid_spec,
        compiler_params=_cparams("parallel", "parallel", "parallel"),
        name="nsa_cmp_select",
    )(first_chunk, pj, cmp_kv, cmp_kv, pool, tmap)


def _alibi_dead(q_sq, k_sq, dist, group_slope):
    bound = 2.0 * NORM_SLACK * jnp.sqrt(q_sq * k_sq)
    return group_slope * dist >= UNDERFLOW_BITS + bound


def _nsa_kernel(flags_ref, q_ref, ks_ref, vs_ref, kwp_ref, vwp_ref, kwd_ref, vwd_ref, sel_ref, oc_ref, gate_ref,
                o_ref, m_ref, acc_ref, *, tile, nq):
    b = pl.program_id(0)
    g = pl.program_id(1)
    i = pl.program_id(2)
    nh = NSA_GROUP_SIZE
    ns = sel_ref.shape[1]
    bpt = tile // SLC_BLOCK
    low = _lane_is_low()
    row = lax.broadcasted_iota(jnp.int32, (tile, tile), 0)
    col = lax.broadcasted_iota(jnp.int32, (tile, tile), 1)
    relpos = lax.broadcasted_iota(jnp.int32, (1, tile), 1).astype(F32)
    sel = sel_ref[...]
    blk_of_key = lax.broadcasted_iota(jnp.int32, (ns, tile), 1) // SLC_BLOCK
    blk_row = lax.broadcasted_iota(jnp.int32, (ns, tile), 0)
    qh = []
    for pair in range(nh // 2):
        qh.extend(_split_pair(q_ref[:, pair * PAIR:(pair + 1) * PAIR]))
    slopes = [_group_slope(g, r) for r in range(nh)]
    _flash_reset(m_ref, acc_ref)

    def selected(j):
        expand = jnp.where(blk_row == blk_of_key + j * bpt, 1.0, 0.0).astype(BF16)
        return _dot(sel, expand) > 0.5

    def attend(slot, k_keys, v_keys, rel, bias, r0=0, nr=tile):
        v_aug = _with_ones(v_keys)
        rows = pl.ds(r0, nr)
        for r in range(nh):
            s = _dot_nt(qh[r][r0:r0 + nr], k_keys) + slopes[r] * rel + bias
            _flash_step(s, v_aug, m_ref.at[slot + r, rows], acc_ref.at[slot + r, rows])

    def sel_step(j, keep):
        ks = pl.multiple_of(j * tile, tile)
        attend(0, ks_ref[pl.ds(ks, tile), :], vs_ref[pl.ds(ks, tile), :],
               relpos + ((j - i) * tile).astype(F32), jnp.where(keep, 0.0, NEG_INF))

    half = tile // 2
    ks_i = pl.multiple_of(i * tile, tile)
    bias_diag = jnp.where(selected(i) & (col <= row), 0.0, NEG_INF)
    attend(0, ks_ref[pl.ds(ks_i, half), :], vs_ref[pl.ds(ks_i, half), :], relpos[:, :half],
           bias_diag[:half, :half], 0, half)
    attend(0, ks_ref[pl.ds(ks_i, tile), :], vs_ref[pl.ds(ks_i, tile), :], relpos, bias_diag[half:, :], half, half)

    def sel_body(j, carry):
        @pl.when(flags_ref[((b * NSA_GROUPS + g) * nq + i) * nq + j] != 0)
        def _():
            sel_step(j, selected(j))
        return carry

    lax.fori_loop(0, i, sel_body, 0)

    span = tile + half
    k_win = jnp.concatenate([kwp_ref[...], kwd_ref[...]], axis=0)
    v_win = jnp.concatenate([vwp_ref[...], vwd_ref[...]], axis=0)
    for r0 in (0, half):
        row2 = lax.broadcasted_iota(jnp.int32, (half, span), 0) + r0
        col2 = lax.broadcasted_iota(jnp.int32, (half, span), 1) + r0
        in_window = ((col2 >= tile) & (col2 - tile <= row2)) | ((col2 < tile) & (col2 > row2) & (i > 0))
        rel2 = (lax.broadcasted_iota(jnp.int32, (1, span), 1) + (r0 - tile)).astype(F32)
        attend(nh, k_win[r0:r0 + span], v_win[r0:r0 + span], rel2, jnp.where(in_window, 0.0, NEG_INF), r0, half)

    gates = jax.nn.sigmoid(gate_ref[...])

    def gate_col(r, branch):
        c0 = FOX_HEADS + 3 * r + branch
        c1 = c0 + 3 * nh
        return jnp.where(g == 0, gates[:, c0:c0 + 1], gates[:, c1:c1 + 1])

    for pair in range(nh // 2):
        o_c = oc_ref[:, pair * PAIR:(pair + 1) * PAIR]
        outs = []
        for h in range(2):
            r = 2 * pair + h
            o_s = _flash_out(acc_ref[r])
            o_w = _flash_out(acc_ref[nh + r])
            outs.append(gate_col(r, 0) * o_c + gate_col(r, 1) * o_s + gate_col(r, 2) * o_w)
        o_ref[:, pair * PAIR:(pair + 1) * PAIR] = jnp.where(low, outs[0], outs[1]).astype(o_ref.dtype)


def _nsa_attention(flags, pj, sel, o_c, small, *, batch, seq, tile, q_blk, ks_blk, vs_blk, kw_blk, vw_blk):
    nq = seq // tile
    ns = sel.shape[3]
    gw = NSA_GROUP_SIZE * HEAD_DIM
    slots = 2 * NSA_GROUP_SIZE
    cur = lambda b, g, i, f, base: (b * nq + i, base + g)
    prev = lambda b, g, i, f, base: (b * nq + jnp.maximum(i - 1, 0), base + g)
    grid_spec = pltpu.PrefetchScalarGridSpec(
        num_scalar_prefetch=1,
        grid=(batch, NSA_GROUPS, nq),
        in_specs=[pl.BlockSpec((tile, gw), functools.partial(cur, base=q_blk)),
                  pl.BlockSpec((seq, PAIR), lambda b, g, i, f: (b, ks_blk + g)),
                  pl.BlockSpec((seq, PAIR), lambda b, g, i, f: (b, vs_blk + g)),
                  pl.BlockSpec((tile, PAIR), functools.partial(prev, base=kw_blk)),
                  pl.BlockSpec((tile, PAIR), functools.partial(prev, base=vw_blk)),
                  pl.BlockSpec((tile, PAIR), functools.partial(cur, base=kw_blk)),
                  pl.BlockSpec((tile, PAIR), functools.partial(cur, base=vw_blk)),
                  pl.BlockSpec((None, None, tile, ns), lambda b, g, i, f: (b, g, i, 0)),
                  pl.BlockSpec((tile, gw), functools.partial(cur, base=0)),
                  pl.BlockSpec((tile, small.shape[1]), lambda b, g, i, f: (b * nq + i, 0))],
        out_specs=pl.BlockSpec((tile, gw), functools.partial(cur, base=0)),
        scratch_shapes=[pltpu.VMEM((slots, tile, LANES), F32), pltpu.VMEM((slots, tile, 2 * PAIR), F32)])
    return pl.pallas_call(
        functools.partial(_nsa_kernel, tile=tile, nq=nq),
        out_shape=jax.ShapeDtypeStruct((batch * seq, NSA_HEADS * HEAD_DIM), BF16),
        grid_spec=grid_spec,
        compiler_params=_cparams("parallel", "parallel", "arbitrary"),
        name="nsa_attention",
    )(flags, pj, pj, pj, pj, pj, pj, pj, sel, o_c, small)


def _ffn_kernel(h_ref, gin_ref, gout_ref, wg_ref, wu_ref, wd_ref, *rest, chunk, mixer, tm, tiles_per_seq):
    h = h_ref[...]
    if mixer == "attn":
        a_ref, b_ref, wa_ref, wb_ref, gmix_ref, o_ref, acc_ref = rest
        h = h + _rms(_dot(a_ref[...], wa_ref[...]) + _dot(b_ref[...], wb_ref[...]), gmix_ref[...])
    else:
        halo_ref, pin_ref, pout_ref, pw_ref, sc_ref, o_ref, acc_ref, ext_ref = rest
        h = _pool_mix(h, halo_ref, pin_ref, pout_ref, pw_ref, sc_ref, ext_ref, tm=tm, tiles_per_seq=tiles_per_seq)
    ub = _rms(h, gin_ref[...]).astype(BF16)
    hidden = wg_ref.shape[1]
    for idx, c in enumerate(range(0, hidden, chunk)):
        gate = _dot(ub, wg_ref[:, c:c + chunk])
        up = _dot(ub, wu_ref[:, c:c + chunk])
        act = (gate * jax.nn.sigmoid(gate) * up).astype(BF16)
        part = _dot(act, wd_ref[c:c + chunk, :])
        if idx == 0:
            acc_ref[...] = part
        else:
            acc_ref[...] += part
    o_ref[...] = h + _rms(acc_ref[...], gout_ref[...])


def _layer_tail(h2, g_in, g_out, wg, wu, wd, *, seq, tm, chunk, mixer, mix):
    n, d = h2.shape
    const = lambda i: (0, 0)
    rows = lambda i: (i, 0)
    scratch = [pltpu.VMEM((tm, d), F32)]
    if mixer == "attn":
        a, b, wa, wb, g_mix = mix
        mix_specs = [pl.BlockSpec((tm, a.shape[1]), rows), pl.BlockSpec((tm, b.shape[1]), rows),
                     pl.BlockSpec(wa.shape, const), pl.BlockSpec(wb.shape, const), pl.BlockSpec((1, d), const)]
    else:
        p_in, p_out, w_groups, scale = mix
        ratio = tm // POOL_HALO
        mix = (h2, p_in, p_out, w_groups, scale)
        mix_specs = [pl.BlockSpec((POOL_HALO, d), lambda i: (jnp.maximum(i * ratio - 1, 0), 0)),
                     pl.BlockSpec((1, d), const), pl.BlockSpec((1, d), const),
                     pl.BlockSpec(w_groups.shape, lambda i: (0, 0, 0)), pl.BlockSpec((1, d), const)]
        scratch.append(pltpu.VMEM((tm + POOL_HALO, d), F32))
    return pl.pallas_call(
        functools.partial(_ffn_kernel, chunk=chunk, mixer=mixer, tm=tm, tiles_per_seq=seq // tm),
        out_shape=jax.ShapeDtypeStruct((n, d), F32),
        grid=(n // tm,),
        in_specs=[pl.BlockSpec((tm, d), rows),
                  pl.BlockSpec((1, d), const),
                  pl.BlockSpec((1, d), const),
                  pl.BlockSpec(wg.shape, const),
                  pl.BlockSpec(wu.shape, const),
                  pl.BlockSpec(wd.shape, const)] + mix_specs,
        out_specs=pl.BlockSpec((tm, d), rows),
        scratch_shapes=scratch,
        compiler_params=_cparams("parallel"),
        name=f"{mixer}_tail_swiglu_ffn",
    )(h2, g_in, g_out, wg, wu, wd, *mix)


def _pool_mix(h, halo_ref, gin_ref, gout_ref, w_ref, sc_ref, ext_ref, *, tm, tiles_per_seq):
    halo = POOL_HALO
    i = pl.program_id(0)
    first = (i % tiles_per_seq) == 0
    u = _rms(h, gin_ref[...])
    uh = _rms(halo_ref[...], gin_ref[...])
    ext_ref[0:halo, :] = jnp.where(first, 0.0, uh)
    ext_ref[halo:, :] = u
    t = (i % tiles_per_seq) * tm + lax.broadcasted_iota(jnp.int32, (tm, 1), 0)
    group = h.shape[1] // len(POOL_WINDOWS)
    ys = []
    for gi, w in enumerate(POOL_WINDOWS):
        cols = slice(gi * group, (gi + 1) * group)
        run = ext_ref[:, cols]
        span = 1
        while span < w:
            run = run + pltpu.roll(run, span, axis=0)
            span *= 2
        total = run[halo:, :]
        count = jnp.minimum(t + 1, w).astype(F32)
        pooled = total / count - u[:, cols]
        ys.append(_dot(pooled.astype(BF16), w_ref[gi]))
    y = jnp.concatenate(ys, axis=-1) * sc_ref[...]
    return h + _rms(y, gout_ref[...])


def _pack_in_weights(w_in):
    d = w_in.shape[0]
    hw = FOX_HEADS * HEAD_DIM
    kv = NSA_GROUPS * HEAD_DIM
    sizes = (hw, hw, hw, FOX_HEADS, NSA_HEADS * HEAD_DIM, kv, kv, kv, kv, kv, kv, 3 * NSA_HEADS)
    offs = np.concatenate([[0], np.cumsum(sizes)])
    fq, fk, fv, ff, nq, kc, vc, ks, vs, kw, vw, ng = [w_in[:, offs[j]:offs[j + 1]] for j in range(len(sizes))]
    scale = HEAD_DIM ** -0.5 * LOG2E

    w_main = jnp.concatenate([fq * scale, fk, fv, nq * scale, ks, vs, kw, vw], axis=1)
    w_cmp = jnp.concatenate([kc, vc], axis=1)
    pad = jnp.zeros((d, LANES - FOX_HEADS - 3 * NSA_HEADS), F32)
    w_small = jnp.concatenate([ff, ng, pad], axis=1)
    w_small_hi = w_small.astype(BF16)
    w_small_lo = (w_small - w_small_hi.astype(F32)).astype(BF16)
    return w_main.astype(BF16), w_cmp.astype(BF16), jnp.concatenate([w_small_hi, w_small_lo], axis=1)


def _mixer_fox_nsa(h2, g_in, g_out, w_in, b_f, cmp_pe, cmp_w1, cmp_w2, w_out, *, batch, seq):
    tile = WINDOW
    assert seq % tile == 0 and seq // tile <= LANES
    w_main, w_cmp, w_small = _pack_in_weights(w_in)
    pj, pc, small, nrm = _inproj(h2, g_in, w_main, w_cmp, w_small, tm=tile)
    npair = FOX_HEADS // 2
    q_blk, k_blk, v_blk, nq_blk = 0, npair, 2 * npair, 3 * npair
    ks_blk = nq_blk + NSA_HEADS // 2
    vs_blk, kw_blk, vw_blk = ks_blk + NSA_GROUPS, ks_blk + 2 * NSA_GROUPS, ks_blk + 3 * NSA_GROUPS

    c = _forget_cumsum(small, b_f, batch=batch, seq=seq, tc=tile)
    nq = seq // tile
    fox_tq, fox_tk = min(FOX_TQ, seq), min(FOX_TK, seq)
    c_tiles = c.reshape(batch, npair, 2, seq // fox_tk, fox_tk).transpose(0, 1, 3, 2, 4)
    live = _fox_live_tiles(c, nrm, batch=batch, seq=seq, tq=fox_tq, tk=fox_tk, tm=tile)
    o_fox = _fox_attention(live, pj, c_tiles, batch=batch, seq=seq, tq=fox_tq, tk=fox_tk,
                           q_blk=q_blk, k_blk=k_blk, v_blk=v_blk)

    nch = seq // CMP_STRIDE
    ns = seq // SLC_BLOCK
    n_sel = min(N_SELECT, ns)
    kv = NSA_GROUPS * HEAD_DIM
    cmp_kv, cmp_nrm = _compress(pc, cmp_pe, cmp_w1, cmp_w2, batch=batch, seq=seq)

    group_slope = jnp.asarray([min(ALIBI_SLOPES_LOG2[g * NSA_GROUP_SIZE:(g + 1) * NSA_GROUP_SIZE])
                               for g in range(NSA_GROUPS)], F32)
    qn = nrm[:, NRM_NSA_Q, :NSA_HEADS].reshape(batch, nq, NSA_GROUPS, NSA_GROUP_SIZE).max(axis=3)
    ksn = nrm[:, NRM_NSA_KS, :NSA_GROUPS].reshape(batch, nq, NSA_GROUPS).max(axis=1)
    kcn = cmp_nrm[:, 0, :, 0, 0]
    tiles = jnp.arange(nq)
    nseg = CMP_SEGMENTS if nq % CMP_SEGMENTS == 0 else 1
    chunk = nch // nseg
    last_end = (jnp.arange(nseg) + 1) * chunk * CMP_STRIDE + (CMP_BLOCK - 1 - CMP_STRIDE)
    dist_c = (tiles[:, None] * tile - last_end[None, :] - (CMP_STRIDE - 1)).astype(F32)
    dead_c = _alibi_dead(qn[..., None], kcn[:, None, :, None], dist_c[None, :, None, :],
                         group_slope[None, None, :, None])
    first_chunk = jnp.minimum(dead_c.sum(axis=3), (tiles * nseg // nq)[None, :, None])
    first_chunk = first_chunk.transpose(0, 2, 1).astype(jnp.int32).reshape(-1)
    dist_s = ((tiles[:, None] - tiles[None, :] - 1) * tile + 1).astype(F32)
    dead_s = _alibi_dead(qn[..., None], ksn[:, None, :, None], dist_s[None, :, None, :],
                         group_slope[None, None, :, None])
    live_s = jnp.logical_not(dead_s).transpose(0, 2, 1, 3).astype(jnp.int32)

    ratio = SLC_BLOCK // CMP_STRIDE
    r = CMP_BLOCK // CMP_STRIDE
    n_idx = np.arange(nch)[:, None]
    b_idx = np.arange(ns)[None, :]
    pool_np = ((n_idx >= ratio * b_idx - (r - 1)) & (n_idx <= ratio * b_idx + ratio - 1)
               & (n_idx < nch - r + 1)).astype(np.float32)
    pool = jnp.asarray(pool_np, dtype=BF16)
    tmap_np = (np.arange(ns)[:, None] // (tile // SLC_BLOCK) == np.arange(LANES)[None, :]).astype(np.float32)
    o_c, sel, flag_blocks = _cmp_select(first_chunk, pj, cmp_kv, pool, jnp.asarray(tmap_np, dtype=BF16),
                                        batch=batch, seq=seq, tq=tile, q_blk=nq_blk // 2, n_sel=n_sel, nseg=nseg)
    flags = (flag_blocks[:, :, :, 0, :nq] * live_s).reshape(-1)
    o_nsa = _nsa_attention(flags, pj, sel, o_c, small, batch=batch, seq=seq, tile=tile,
                           q_blk=nq_blk // 2, ks_blk=ks_blk, vs_blk=vs_blk, kw_blk=kw_blk, vw_blk=vw_blk)

    hw = FOX_HEADS * HEAD_DIM
    return o_fox, o_nsa, w_out[:hw].astype(BF16), w_out[hw:].astype(BF16), g_out


def kernel(x, norm_g, attn_w_in, fox_b_f, nsa_cmp_pe, nsa_cmp_w1, nsa_cmp_w2, attn_w_out,
           pool_w, pool_scale, ffn_w_gate, ffn_w_up, ffn_w_down):
    batch, seq, d = x.shape
    depth = norm_g.shape[0]
    tm = 512 if seq % 512 == 0 else seq
    h = x.reshape(batch * seq, d)
    for layer in range(depth):
        g = norm_g[layer].reshape(4, 1, d)
        i = layer // 2
        if layer % 2 == 0:
            mixer = "attn"
            mix = _mixer_fox_nsa(h, g[0], g[1], attn_w_in[i], fox_b_f[i], nsa_cmp_pe[i], nsa_cmp_w1[i],
                                 nsa_cmp_w2[i], attn_w_out[i], batch=batch, seq=seq)
        else:
            mixer = "pool"
            mix = (g[0], g[1], pool_w[i].astype(BF16), pool_scale[i].reshape(1, d))
        h = _layer_tail(h, g[2], g[3], ffn_w_gate[layer].astype(BF16), ffn_w_up[layer].astype(BF16),
                        ffn_w_down[layer].astype(BF16), seq=seq, tm=tm, chunk=FFN_CHUNK, mixer=mixer, mix=mix)
    return h.reshape(batch, seq, d)
```

```python
import functools

import numpy as np
import jax
import jax.numpy as jnp
from jax import lax
from jax.experimental import pallas as pl
from jax.experimental.pallas import tpu as pltpu

F32 = jnp.float32
BF16 = jnp.bfloat16

HEAD_DIM = 64
FOX_HEADS = 8
NSA_HEADS = 8
NSA_GROUPS = 2
NSA_GROUP_SIZE = NSA_HEADS // NSA_GROUPS
CMP_BLOCK = 32
CMP_STRIDE = 16
SLC_BLOCK = 64
N_SELECT = 16
WINDOW = 512
POOL_WINDOWS = (2, 4, 8, 16)
POOL_HALO = max(POOL_WINDOWS)
RMS_EPS = 1e-6
NEG_INF = -1e30
BIG = 1e30
REMOVED = -3e38
LANES = 128
PAIR = 2 * HEAD_DIM
VMEM_LIMIT = 56 * 1024 * 1024
CMP_SEGMENTS = 4
UNDERFLOW_BITS = 154.0
CUMSUM_CHUNK = 1024
NORM_SLACK = 1.02
NORM_CHUNKS = (0, 1, 3, 4)
NRM_FOX_Q, NRM_FOX_K, NRM_NSA_Q, NRM_NSA_KS = range(4)
FFN_CHUNK = 256
FOX_TQ, FOX_TK = 1024, 1024

LOG2E = 1.4426950408889634
ALIBI_SLOPES_LOG2 = tuple(float(2.0 ** (-8.0 * (i + 1.0) / NSA_HEADS)) * LOG2E for i in range(NSA_HEADS))


def _cparams(*sem):
    return pltpu.CompilerParams(dimension_semantics=sem, vmem_limit_bytes=VMEM_LIMIT)


def _rms(x, g):
    return x * lax.rsqrt(jnp.mean(x * x, axis=-1, keepdims=True) + RMS_EPS) * g


def _dot(a, b):
    return jnp.dot(a, b, preferred_element_type=F32)


def _dot_nt(a, b):
    return lax.dot_general(a, b, (((1,), (1,)), ((), ())), preferred_element_type=F32)


def _inproj_kernel(x_ref, g_ref, w_ref, wc_ref, ws_ref, hmap_ref, o_ref, oc_ref, os_ref, nrm_ref, *, fox_width):
    u = _rms(x_ref[...], g_ref[...])
    ub = u.astype(BF16)
    norms = []

    def max_sq_norms(r, ncols):
        rf = r.astype(F32)
        sq = _dot((rf * rf).astype(BF16), hmap_ref[0:ncols, :])
        norms.append(jnp.max(sq, axis=0, keepdims=True))

    heads_width = 4 * fox_width
    for c in range(0, heads_width, fox_width):
        r = _dot(ub, w_ref[:, c:c + fox_width]).astype(BF16)
        o_ref[:, c:c + fox_width] = r
        if c // fox_width in NORM_CHUNKS:
            max_sq_norms(r, fox_width)
    kv = _dot(ub, w_ref[:, heads_width:])
    max_sq_norms(kv[:, :PAIR].astype(BF16), PAIR)
    low = _lane_is_low()
    for blk in range(kv.shape[1] // PAIR):
        x = kv[:, blk * PAIR:(blk + 1) * PAIR]
        swapped = pltpu.roll(x, HEAD_DIM, axis=1)
        base = heads_width + 2 * blk * PAIR
        o_ref[:, base:base + PAIR] = jnp.where(low, x, swapped).astype(BF16)
        o_ref[:, base + PAIR:base + 2 * PAIR] = jnp.where(low, swapped, x).astype(BF16)
    oc_ref[...] = _dot(ub, wc_ref[...])
    width = os_ref.shape[1]
    u_lo = (u - ub.astype(F32)).astype(BF16)
    both = _dot(ub, ws_ref[...])
    os_ref[...] = both[:, :width] + both[:, width:] + _dot(u_lo, ws_ref[:, :width])
    row = lax.broadcasted_iota(jnp.int32, nrm_ref.shape, 0)
    out = jnp.zeros(nrm_ref.shape, F32)
    for k, nk in enumerate(norms):
        out = jnp.where(row == k, nk, out)
    nrm_ref[...] = out


def _inproj(x2, g, w_main, w_cmp, w_small, *, tm):
    n, d = x2.shape
    cin, cc, cs = w_main.shape[1], w_cmp.shape[1], w_small.shape[1] // 2
    fox_width = FOX_HEADS * HEAD_DIM
    cm = 4 * fox_width + 2 * (cin - 4 * fox_width)
    hmap = jnp.asarray((np.arange(fox_width)[:, None] // HEAD_DIM == np.arange(LANES)[None, :]).astype(np.float32),
                       dtype=BF16)
    const = lambda i: (0, 0)
    return pl.pallas_call(
        functools.partial(_inproj_kernel, fox_width=fox_width),
        out_shape=(jax.ShapeDtypeStruct((n, cm), BF16),
                   jax.ShapeDtypeStruct((n, cc), F32),
                   jax.ShapeDtypeStruct((n, cs), F32),
                   jax.ShapeDtypeStruct((n // tm, 8, LANES), F32)),
        grid=(n // tm,),
        in_specs=[pl.BlockSpec((tm, d), lambda i: (i, 0)),
                  pl.BlockSpec((1, d), const),
                  pl.BlockSpec((d, cin), const),
                  pl.BlockSpec((d, cc), const),
                  pl.BlockSpec((d, 2 * cs), const),
                  pl.BlockSpec((fox_width, LANES), const)],
        out_specs=(pl.BlockSpec((tm, cm), lambda i: (i, 0)),
                   pl.BlockSpec((tm, cc), lambda i: (i, 0)),
                   pl.BlockSpec((tm, cs), lambda i: (i, 0)),
                   pl.BlockSpec((None, 8, LANES), lambda i: (i, 0, 0))),
        compiler_params=_cparams("parallel"),
        name="inproj",
    )(x2, g, w_main, w_cmp, w_small, hmap)


def _forget_cumsum_kernel(sm_ref, bf_ref, tri_ref, o_ref, carry_ref):
    @pl.when(pl.program_id(1) == 0)
    def _():
        carry_ref[...] = jnp.zeros_like(carry_ref)

    zt = sm_ref[...].T
    z = zt[0:FOX_HEADS, :] + bf_ref[...]
    log_f = jnp.minimum(z, 0.0) - jnp.log1p(jnp.exp(-jnp.abs(z)))
    c = jnp.dot(log_f, tri_ref[...], preferred_element_type=F32,
                precision=lax.Precision.HIGHEST) + carry_ref[...]
    o_ref[...] = c * LOG2E
    carry_ref[...] = c[:, c.shape[1] - 1:]


def _forget_cumsum(small, b_f, *, batch, seq, tc):
    tri = jnp.asarray(np.triu(np.ones((tc, tc), np.float32)))
    nchunk = seq // tc
    return pl.pallas_call(
        _forget_cumsum_kernel,
        out_shape=jax.ShapeDtypeStruct((batch, FOX_HEADS, seq), F32),
        grid=(batch, nchunk),
        in_specs=[pl.BlockSpec((tc, small.shape[1]), lambda b, j: (b * nchunk + j, 0)),
                  pl.BlockSpec((FOX_HEADS, 1), lambda b, j: (0, 0)),
                  pl.BlockSpec((tc, tc), lambda b, j: (0, 0))],
        out_specs=pl.BlockSpec((None, FOX_HEADS, tc), lambda b, j: (b, 0, j)),
        scratch_shapes=[pltpu.VMEM((FOX_HEADS, 1), F32)],
        compiler_params=_cparams("parallel", "arbitrary"),
        name="forget_cumsum",
    )(small, b_f.reshape(FOX_HEADS, 1), tri)


def _lane_is_low():
    return lax.broadcasted_iota(jnp.int32, (1, PAIR), 1) < HEAD_DIM


def _with_ones(v_tile):
    return jnp.concatenate([v_tile, jnp.ones_like(v_tile)], axis=1)


def _flash_step(s, v, m_ref, acc_ref, l_ref=None):
    m_old = m_ref[...]
    m_new = jnp.maximum(m_old, jnp.max(s, axis=-1, keepdims=True))
    alpha = jnp.exp2(m_old - m_new)
    p = jnp.exp2(s - jnp.concatenate([m_new] * (s.shape[1] // LANES), axis=1))
    if l_ref is None:
        alpha_acc = jnp.concatenate([alpha, alpha], axis=1)
    else:
        alpha_acc = alpha
        l_ref[...] = alpha * l_ref[...] + jnp.sum(p, axis=-1, keepdims=True)
    acc_ref[...] = alpha_acc * acc_ref[...] + _dot(p.astype(BF16), v)
    m_ref[...] = m_new


def _flash_reset(m_ref, *sum_refs):
    m_ref[...] = jnp.full_like(m_ref, NEG_INF)
    for ref in sum_refs:
        ref[...] = jnp.zeros_like(ref)


def _flash_out(acc):
    return acc[:, :PAIR] / acc[:, PAIR:]


def _split_pair(q):
    low = _lane_is_low()
    zero = jnp.zeros_like(q)
    return jnp.where(low, q, zero), jnp.where(low, zero, q)


def _fox_kernel(live_ref, q_ref, k_ref, v_ref, c_ref, o_ref, m_ref, l_ref, acc_ref, *, tq, tk, nq):
    b = pl.program_id(0)
    pair = pl.program_id(1)
    i = pl.program_id(2)
    nk = c_ref.shape[0]
    qh = _split_pair(q_ref[...])
    _flash_reset(m_ref, l_ref, acc_ref)

    def head_step(h, j, mask):
        ks = pl.multiple_of(j * tk, tk)
        s = _dot_nt(qh[h], k_ref[pl.ds(ks, tk), :]) - c_ref[j, h:h + 1, :]
        if mask is not None:
            s = jnp.where(mask, s, NEG_INF)
        _flash_step(s, v_ref[pl.ds(ks, tk), :], m_ref.at[h], acc_ref.at[h], l_ref.at[h])

    def body(j, carry):
        live = [live_ref[(((b * FOX_HEADS) + 2 * pair + h) * nq + i) * nk + j] != 0 for h in range(2)]

        @pl.when(live[0] & live[1])
        def _():
            head_step(0, j, None)
            head_step(1, j, None)

        for h in range(2):
            @pl.when(live[h] & jnp.logical_not(live[1 - h]))
            def _(h=h):
                head_step(h, j, None)
        return carry

    lax.fori_loop(0, i, body, 0)

    def diag_part(h, r0, nr, nkeys):
        ks = pl.multiple_of(i * tq, tq)
        rows = pl.ds(r0, nr)
        row = lax.broadcasted_iota(jnp.int32, (nr, nkeys), 0) + r0
        col = lax.broadcasted_iota(jnp.int32, (nr, nkeys), 1)
        s = _dot_nt(qh[h][r0:r0 + nr], k_ref[pl.ds(ks, nkeys), :]) - c_ref[i, h:h + 1, 0:nkeys]
        _flash_step(jnp.where(col <= row, s, NEG_INF), v_ref[pl.ds(ks, nkeys), :],
                    m_ref.at[h, rows], acc_ref.at[h, rows], l_ref.at[h, rows])

    half = tq // 2
    for h in range(2):
        diag_part(h, 0, half, half)
        diag_part(h, half, half, tq)
    o_ref[...] = jnp.where(_lane_is_low(), acc_ref[0] / l_ref[0], acc_ref[1] / l_ref[1]).astype(o_ref.dtype)


def _fox_live_tiles(c2, nrm, *, batch, seq, tq, tk, tm):
    nq, nk = seq // tq, seq // tk
    qn = nrm[:, 0, :FOX_HEADS].reshape(batch, nq, tq // tm, FOX_HEADS).max(axis=2)
    kn = nrm[:, 1, :FOX_HEADS].reshape(batch, seq // tm, FOX_HEADS).max(axis=1)
    bound = 2.0 * NORM_SLACK * jnp.sqrt(qn * kn[:, None, :])
    c_start = c2[:, :, ::tq]
    c_end = c2[:, :, tk - 1::tk]
    decay = c_end[:, :, None, :] - c_start[:, :, :, None]
    live = decay < UNDERFLOW_BITS + jnp.transpose(bound, (0, 2, 1))[..., None]
    return live.astype(jnp.int32).reshape(-1)


def _fox_attention(live, pj, c_tiles, *, batch, seq, tq, tk, q_blk, k_blk, v_blk):
    assert tq == tk and tq % (2 * LANES) == 0
    nq = seq // tq
    nk = seq // tk
    npair = FOX_HEADS // 2
    grid_spec = pltpu.PrefetchScalarGridSpec(
        num_scalar_prefetch=1,
        grid=(batch, npair, nq),
        in_specs=[pl.BlockSpec((tq, PAIR), lambda b, p, i, f: (b * nq + i, q_blk + p)),
                  pl.BlockSpec((seq, PAIR), lambda b, p, i, f: (b, k_blk + p)),
                  pl.BlockSpec((seq, PAIR), lambda b, p, i, f: (b, v_blk + p)),
                  pl.BlockSpec((None, None, nk, 2, tk), lambda b, p, i, f: (b, p, 0, 0, 0))],
        out_specs=pl.BlockSpec((tq, PAIR), lambda b, p, i, f: (b * nq + i, p)),
        scratch_shapes=[pltpu.VMEM((2, tq, LANES), F32), pltpu.VMEM((2, tq, LANES), F32),
                        pltpu.VMEM((2, tq, PAIR), F32)])
    return pl.pallas_call(
        functools.partial(_fox_kernel, tq=tq, tk=tk, nq=nq),
        out_shape=jax.ShapeDtypeStruct((batch * seq, FOX_HEADS * HEAD_DIM), BF16),
        grid_spec=grid_spec,
        compiler_params=_cparams("parallel", "parallel", "arbitrary"),
        name="fox_attention",
    )(live, pj, pj, pj, c_tiles)


def _gelu_tanh(x):
    return 0.5 * x * (1.0 + jnp.tanh(0.7978845608028654 * (x + 0.044715 * x * x * x)))


def _compress_kernel(x_ref, pe_ref, w1pos_ref, w1dup_ref, w2sel_ref, o_ref, nrm_ref):
    nch = o_ref.shape[1]
    a = b = None
    for j in range(CMP_STRIDE):
        xj = x_ref[pl.ds(j, nch, stride=CMP_STRIDE), :].astype(BF16)
        pa, pb = _dot(xj, w1pos_ref[j]), _dot(xj, w1pos_ref[CMP_STRIDE + j])
        a, b = (pa, pb) if a is None else (a + pa, b + pb)
    pe_term = _dot(pe_ref[...], w1dup_ref[...])[0:1, :]
    h = a + pltpu.roll(b, nch - 1, axis=0) + pe_term
    y = _gelu_tanh(h).astype(BF16)
    for g in range(NSA_GROUPS):
        o = _dot(y, w2sel_ref[g]).astype(o_ref.dtype)
        o_ref[g] = o
        of = o.astype(F32)
        sq = 0.5 * jnp.sum(of * of, axis=-1, keepdims=True)
        nrm_ref[g] = jnp.broadcast_to(jnp.max(sq, axis=0, keepdims=True), nrm_ref.shape[1:])


def _compress(pc, cmp_pe, cmp_w1, cmp_w2, *, batch, seq):
    nch = seq // CMP_STRIDE
    dh = HEAD_DIM
    zero = jnp.zeros((2, CMP_BLOCK, dh, dh), F32)
    w1r = cmp_w1.reshape(2, CMP_BLOCK, dh, dh)
    w1pos = jnp.concatenate([jnp.concatenate([w1r, zero], axis=-1),
                             jnp.concatenate([zero, w1r], axis=-1)], axis=-2).astype(BF16)
    w1dup = jnp.concatenate([cmp_w1, cmp_w1], axis=-1).astype(BF16)
    w2dup = jnp.concatenate([cmp_w2, cmp_w2], axis=-1)
    zero2 = jnp.zeros_like(w2dup)
    w2sel = jnp.stack([jnp.concatenate([w2dup, zero2], axis=1),
                       jnp.concatenate([zero2, w2dup], axis=1)], axis=1).astype(BF16)
    pe8 = jnp.broadcast_to(cmp_pe.reshape(2, 1, CMP_BLOCK * dh), (2, 8, CMP_BLOCK * dh)).astype(BF16)
    return pl.pallas_call(
        _compress_kernel,
        out_shape=(jax.ShapeDtypeStruct((batch, 2, NSA_GROUPS, nch, PAIR), BF16),
                   jax.ShapeDtypeStruct((batch, 2, NSA_GROUPS, 8, LANES), F32)),
        grid=(batch, 2),
        in_specs=[pl.BlockSpec((seq, PAIR), lambda b, kv: (b, kv)),
                  pl.BlockSpec((None, 8, CMP_BLOCK * dh), lambda b, kv: (kv, 0, 0)),
                  pl.BlockSpec((None, CMP_BLOCK, PAIR, PAIR), lambda b, kv: (kv, 0, 0, 0)),
                  pl.BlockSpec((None, CMP_BLOCK * dh, PAIR), lambda b, kv: (kv, 0, 0)),
                  pl.BlockSpec((None, NSA_GROUPS, PAIR, PAIR), lambda b, kv: (kv, 0, 0, 0))],
        out_specs=(pl.BlockSpec((None, None, NSA_GROUPS, nch, PAIR), lambda b, kv: (b, kv, 0, 0, 0)),
                   pl.BlockSpec((None, None, NSA_GROUPS, 8, LANES), lambda b, kv: (b, kv, 0, 0, 0))),
        compiler_params=_cparams("parallel", "parallel"),
        name="nsa_compress",
    )(pc, pe8, w1pos, w1dup, w2sel)


def _group_slope(g, r):
    return jnp.where(g == 0, ALIBI_SLOPES_LOG2[r], ALIBI_SLOPES_LOG2[NSA_GROUP_SIZE + r]).astype(F32)


def _split3(x):
    hi = x.astype(BF16)
    r1 = x - hi.astype(F32)
    mid = r1.astype(BF16)
    lo = (r1 - mid.astype(F32)).astype(BF16)
    return hi, mid, lo


def _cmp_select_kernel(first_ref, q_ref, kc_ref, vc_ref, pool_ref, tmap_ref, oc_ref, sel_ref, flag_ref, *,
                       tq, n_sel, nseg, nq):
    b = pl.program_id(0)
    g = pl.program_id(1)
    i = pl.program_id(2)
    chunk = kc_ref.shape[0] // nseg
    first = first_ref[(b * NSA_GROUPS + g) * nq + i]
    n_live = (i * nseg) // nq + 1 - first
    for k in range(1, nseg + 1):
        @pl.when(n_live == k)
        def _(k=k):
            _cmp_select_body(q_ref, kc_ref, vc_ref, pool_ref, tmap_ref, oc_ref, sel_ref, flag_ref,
                             tq=tq, n_sel=n_sel, start=pl.multiple_of(first * chunk, chunk), nch=k * chunk)


def _cmp_select_body(q_ref, kc_ref, vc_ref, pool_ref, tmap_ref, oc_ref, sel_ref, flag_ref, *,
                     tq, n_sel, start, nch):
    g = pl.program_id(1)
    t0 = pl.program_id(2) * tq
    ns = pool_ref.shape[1]
    low = _lane_is_low()
    t = t0 + lax.broadcasted_iota(jnp.int32, (tq, 1), 0)
    cmp_end = (start + lax.broadcasted_iota(jnp.int32, (1, nch), 1)) * CMP_STRIDE + (CMP_BLOCK - 1)
    rel_end = (cmp_end - t0).astype(F32)
    mask_bias = jnp.where(t >= cmp_end, 0.0, NEG_INF)
    row_valid = jnp.where(t >= CMP_BLOCK - 1, 1.0, 0.0)
    kc = kc_ref[pl.ds(start, nch), :]
    vc = vc_ref[pl.ds(start, nch), :]
    imp = jnp.zeros((tq, nch), F32)
    for pair in range(NSA_GROUP_SIZE // 2):
        qh = _split_pair(q_ref[:, pair * PAIR:(pair + 1) * PAIR])
        outs = []
        for h in range(2):
            s = _dot_nt(qh[h], kc) + _group_slope(g, 2 * pair + h) * rel_end + mask_bias
            e = jnp.exp2(s - jnp.max(s, axis=-1, keepdims=True))
            p = e * (row_valid / jnp.maximum(jnp.sum(e, axis=-1, keepdims=True), 1e-30))
            outs.append(_dot(p.astype(BF16), vc))
            imp = imp + p
        oc_ref[:, pair * PAIR:(pair + 1) * PAIR] = jnp.where(low, outs[0], outs[1])

    pool = pool_ref[pl.ds(start, nch), :]
    hi, mid, lo = _split3(imp)
    p_slc = _dot(hi, pool) + _dot(mid, pool) + _dot(lo, pool)

    blk = lax.broadcasted_iota(jnp.int32, (1, ns), 1)
    cur = t // SLC_BLOCK
    forced = (blk == 0) | (blk == cur) | (blk == cur - 1)
    val = jnp.where(forced, REMOVED, jnp.where(blk > cur, NEG_INF, p_slc))
    sel = jnp.where(forced, 1.0, 0.0)
    blk_f = blk.astype(F32)
    for _ in range(n_sel - 3):
        mx = jnp.max(val, axis=-1, keepdims=True)
        pick = jnp.min(jnp.where(val == mx, blk_f, float(ns)), axis=-1, keepdims=True)
        hit = blk_f == pick
        sel = jnp.where(hit, 1.0, sel)
        val = jnp.where(hit, REMOVED, val)
    sel_ref[...] = sel.astype(sel_ref.dtype)
    any_sel = jnp.broadcast_to(jnp.max(sel, axis=0, keepdims=True), (8, ns)).astype(BF16)
    flag_ref[...] = (_dot(any_sel, tmap_ref[...]) > 0.5).astype(jnp.int32)


def _cmp_select(first_chunk, pj, cmp_kv, pool, tmap, *, batch, seq, tq, q_blk, n_sel, nseg):
    nq = seq // tq
    nch = cmp_kv.shape[3]
    ns = pool.shape[1]
    gw = NSA_GROUP_SIZE * HEAD_DIM
    grid_spec = pltpu.PrefetchScalarGridSpec(
        num_scalar_prefetch=1,
        grid=(batch, NSA_GROUPS, nq),
        in_specs=[pl.BlockSpec((tq, gw), lambda b, g, i, f: (b * nq + i, q_blk + g)),
                  pl.BlockSpec((None, None, None, nch, PAIR), lambda b, g, i, f: (b, 0, g, 0, 0)),
                  pl.BlockSpec((None, None, None, nch, PAIR), lambda b, g, i, f: (b, 1, g, 0, 0)),
                  pl.BlockSpec((nch, ns), lambda b, g, i, f: (0, 0)),
                  pl.BlockSpec((ns, LANES), lambda b, g, i, f: (0, 0))],
        out_specs=(pl.BlockSpec((tq, gw), lambda b, g, i, f: (b * nq + i, g)),
                   pl.BlockSpec((None, None, tq, ns), lambda b, g, i, f: (b, g, i, 0)),
                   pl.BlockSpec((None, None, None, 8, LANES), lambda b, g, i, f: (b, g, i, 0, 0))))
    return pl.pallas_call(
        functools.partial(_cmp_select_kernel, tq=tq, n_sel=n_sel, nseg=nseg, nq=nq),
        out_shape=(jax.ShapeDtypeStruct((batch * seq, NSA_HEADS * HEAD_DIM), F32),
                   jax.ShapeDtypeStruct((batch, NSA_GROUPS, seq, ns), BF16),
                   jax.ShapeDtypeStruct((batch, NSA_GROUPS, nq, 8, LANES), jnp.int32)),
        grid_spec=grid_spec,
        compiler_params=_cparams("parallel", "parallel", "parallel"),
        name="nsa_cmp_select",
    )(first_chunk, pj, cmp_kv, cmp_kv, pool, tmap)


def _alibi_dead(q_sq, k_sq, dist, group_slope):
    bound = 2.0 * NORM_SLACK * jnp.sqrt(q_sq * k_sq)
    return group_slope * dist >= UNDERFLOW_BITS + bound


def _nsa_kernel(flags_ref, q_ref, ks_ref, vs_ref, kwp_ref, vwp_ref, kwd_ref, vwd_ref, sel_ref, oc_ref, gate_ref,
                o_ref, m_ref, acc_ref, *, tile, nq):
    b = pl.program_id(0)
    g = pl.program_id(1)
    i = pl.program_id(2)
    nh = NSA_GROUP_SIZE
    ns = sel_ref.shape[1]
    bpt = tile // SLC_BLOCK
    low = _lane_is_low()
    row = lax.broadcasted_iota(jnp.int32, (tile, tile), 0)
    col = lax.broadcasted_iota(jnp.int32, (tile, tile), 1)
    relpos = lax.broadcasted_iota(jnp.int32, (1, tile), 1).astype(F32)
    sel = sel_ref[...]
    blk_of_key = lax.broadcasted_iota(jnp.int32, (ns, tile), 1) // SLC_BLOCK
    blk_row = lax.broadcasted_iota(jnp.int32, (ns, tile), 0)
    qh = []
    for pair in range(nh // 2):
        qh.extend(_split_pair(q_ref[:, pair * PAIR:(pair + 1) * PAIR]))
    slopes = [_group_slope(g, r) for r in range(nh)]
    _flash_reset(m_ref, acc_ref)

    def selected(j):
        expand = jnp.where(blk_row == blk_of_key + j * bpt, 1.0, 0.0).astype(BF16)
        return _dot(sel, expand) > 0.5

    def attend(slot, k_keys, v_keys, rel, bias, r0=0, nr=tile):
        v_aug = _with_ones(v_keys)
        rows = pl.ds(r0, nr)
        for r in range(nh):
            s = _dot_nt(qh[r][r0:r0 + nr], k_keys) + slopes[r] * rel + bias
            _flash_step(s, v_aug, m_ref.at[slot + r, rows], acc_ref.at[slot + r, rows])

    def sel_step(j, keep):
        ks = pl.multiple_of(j * tile, tile)
        attend(0, ks_ref[pl.ds(ks, tile), :], vs_ref[pl.ds(ks, tile), :],
               relpos + ((j - i) * tile).astype(F32), jnp.where(keep, 0.0, NEG_INF))

    half = tile // 2
    ks_i = pl.multiple_of(i * tile, tile)
    bias_diag = jnp.where(selected(i) & (col <= row), 0.0, NEG_INF)
    attend(0, ks_ref[pl.ds(ks_i, half), :], vs_ref[pl.ds(ks_i, half), :], relpos[:, :half],
           bias_diag[:half, :half], 0, half)
    attend(0, ks_ref[pl.ds(ks_i, tile), :], vs_ref[pl.ds(ks_i, tile), :], relpos, bias_diag[half:, :], half, half)

    def sel_body(j, carry):
        @pl.when(flags_ref[((b * NSA_GROUPS + g) * nq + i) * nq + j] != 0)
        def _():
            sel_step(j, selected(j))
        return carry

    lax.fori_loop(0, i, sel_body, 0)

    span = tile + half
    k_win = jnp.concatenate([kwp_ref[...], kwd_ref[...]], axis=0)
    v_win = jnp.concatenate([vwp_ref[...], vwd_ref[...]], axis=0)
    for r0 in (0, half):
        row2 = lax.broadcasted_iota(jnp.int32, (half, span), 0) + r0
        col2 = lax.broadcasted_iota(jnp.int32, (half, span), 1) + r0
        in_window = ((col2 >= tile) & (col2 - tile <= row2)) | ((col2 < tile) & (col2 > row2) & (i > 0))
        rel2 = (lax.broadcasted_iota(jnp.int32, (1, span), 1) + (r0 - tile)).astype(F32)
        attend(nh, k_win[r0:r0 + span], v_win[r0:r0 + span], rel2, jnp.where(in_window, 0.0, NEG_INF), r0, half)

    gates = jax.nn.sigmoid(gate_ref[...])

    def gate_col(r, branch):
        c0 = FOX_HEADS + 3 * r + branch
        c1 = c0 + 3 * nh
        return jnp.where(g == 0, gates[:, c0:c0 + 1], gates[:, c1:c1 + 1])

    for pair in range(nh // 2):
        o_c = oc_ref[:, pair * PAIR:(pair + 1) * PAIR]
        outs = []
        for h in range(2):
            r = 2 * pair + h
            o_s = _flash_out(acc_ref[r])
            o_w = _flash_out(acc_ref[nh + r])
            outs.append(gate_col(r, 0) * o_c + gate_col(r, 1) * o_s + gate_col(r, 2) * o_w)
        o_ref[:, pair * PAIR:(pair + 1) * PAIR] = jnp.where(low, outs[0], outs[1]).astype(o_ref.dtype)


def _nsa_attention(flags, pj, sel, o_c, small, *, batch, seq, tile, q_blk, ks_blk, vs_blk, kw_blk, vw_blk):
    nq = seq // tile
    ns = sel.shape[3]
    gw = NSA_GROUP_SIZE * HEAD_DIM
    slots = 2 * NSA_GROUP_SIZE
    cur = lambda b, g, i, f, base: (b * nq + i, base + g)
    prev = lambda b, g, i, f, base: (b * nq + jnp.maximum(i - 1, 0), base + g)
    grid_spec = pltpu.PrefetchScalarGridSpec(
        num_scalar_prefetch=1,
        grid=(batch, NSA_GROUPS, nq),
        in_specs=[pl.BlockSpec((tile, gw), functools.partial(cur, base=q_blk)),
                  pl.BlockSpec((seq, PAIR), lambda b, g, i, f: (b, ks_blk + g)),
                  pl.BlockSpec((seq, PAIR), lambda b, g, i, f: (b, vs_blk + g)),
                  pl.BlockSpec((tile, PAIR), functools.partial(prev, base=kw_blk)),
                  pl.BlockSpec((tile, PAIR), functools.partial(prev, base=vw_blk)),
                  pl.BlockSpec((tile, PAIR), functools.partial(cur, base=kw_blk)),
                  pl.BlockSpec((tile, PAIR), functools.partial(cur, base=vw_blk)),
                  pl.BlockSpec((None, None, tile, ns), lambda b, g, i, f: (b, g, i, 0)),
                  pl.BlockSpec((tile, gw), functools.partial(cur, base=0)),
                  pl.BlockSpec((tile, small.shape[1]), lambda b, g, i, f: (b * nq + i, 0))],
        out_specs=pl.BlockSpec((tile, gw), functools.partial(cur, base=0)),
        scratch_shapes=[pltpu.VMEM((slots, tile, LANES), F32), pltpu.VMEM((slots, tile, 2 * PAIR), F32)])
    return pl.pallas_call(
        functools.partial(_nsa_kernel, tile=tile, nq=nq),
        out_shape=jax.ShapeDtypeStruct((batch * seq, NSA_HEADS * HEAD_DIM), BF16),
        grid_spec=grid_spec,
        compiler_params=_cparams("parallel", "parallel", "arbitrary"),
        name="nsa_attention",
    )(flags, pj, pj, pj, pj, pj, pj, pj, sel, o_c, small)


def _ffn_kernel(h_ref, gin_ref, gout_ref, wg_ref, wu_ref, wd_ref, *rest, chunk, mixer, tm, tiles_per_seq):
    h = h_ref[...]
    if mixer == "attn":
        a_ref, b_ref, wa_ref, wb_ref, gmix_ref, o_ref, acc_ref = rest
        h = h + _rms(_dot(a_ref[...], wa_ref[...]) + _dot(b_ref[...], wb_ref[...]), gmix_ref[...])
    else:
        halo_ref, pin_ref, pout_ref, pw_ref, sc_ref, o_ref, acc_ref, ext_ref = rest
        h = _pool_mix(h, halo_ref, pin_ref, pout_ref, pw_ref, sc_ref, ext_ref, tm=tm, tiles_per_seq=tiles_per_seq)
    ub = _rms(h, gin_ref[...]).astype(BF16)
    hidden = wg_ref.shape[1]
    for idx, c in enumerate(range(0, hidden, chunk)):
        gate = _dot(ub, wg_ref[:, c:c + chunk])
        up = _dot(ub, wu_ref[:, c:c + chunk])
        act = (gate * jax.nn.sigmoid(gate) * up).astype(BF16)
        part = _dot(act, wd_ref[c:c + chunk, :])
        if idx == 0:
            acc_ref[...] = part
        else:
            acc_ref[...] += part
    o_ref[...] = h + _rms(acc_ref[...], gout_ref[...])


def _layer_tail(h2, g_in, g_out, wg, wu, wd, *, seq, tm, chunk, mixer, mix):
    n, d = h2.shape
    const = lambda i: (0, 0)
    rows = lambda i: (i, 0)
    scratch = [pltpu.VMEM((tm, d), F32)]
    if mixer == "attn":
        a, b, wa, wb, g_mix = mix
        mix_specs = [pl.BlockSpec((tm, a.shape[1]), rows), pl.BlockSpec((tm, b.shape[1]), rows),
                     pl.BlockSpec(wa.shape, const), pl.BlockSpec(wb.shape, const), pl.BlockSpec((1, d), const)]
    else:
        p_in, p_out, w_groups, scale = mix
        ratio = tm // POOL_HALO
        mix = (h2, p_in, p_out, w_groups, scale)
        mix_specs = [pl.BlockSpec((POOL_HALO, d), lambda i: (jnp.maximum(i * ratio - 1, 0), 0)),
                     pl.BlockSpec((1, d), const), pl.BlockSpec((1, d), const),
                     pl.BlockSpec(w_groups.shape, lambda i: (0, 0, 0)), pl.BlockSpec((1, d), const)]
        scratch.append(pltpu.VMEM((tm + POOL_HALO, d), F32))
    return pl.pallas_call(
        functools.partial(_ffn_kernel, chunk=chunk, mixer=mixer, tm=tm, tiles_per_seq=seq // tm),
        out_shape=jax.ShapeDtypeStruct((n, d), F32),
        grid=(n // tm,),
        in_specs=[pl.BlockSpec((tm, d), rows),
                  pl.BlockSpec((1, d), const),
                  pl.BlockSpec((1, d), const),
                  pl.BlockSpec(wg.shape, const),
                  pl.BlockSpec(wu.shape, const),
                  pl.BlockSpec(wd.shape, const)] + mix_specs,
        out_specs=pl.BlockSpec((tm, d), rows),
        scratch_shapes=scratch,
        compiler_params=_cparams("parallel"),
        name=f"{mixer}_tail_swiglu_ffn",
    )(h2, g_in, g_out, wg, wu, wd, *mix)


def _pool_mix(h, halo_ref, gin_ref, gout_ref, w_ref, sc_ref, ext_ref, *, tm, tiles_per_seq):
    halo = POOL_HALO
    i = pl.program_id(0)
    first = (i % tiles_per_seq) == 0
    u = _rms(h, gin_ref[...])
    uh = _rms(halo_ref[...], gin_ref[...])
    ext_ref[0:halo, :] = jnp.where(first, 0.0, uh)
    ext_ref[halo:, :] = u
    t = (i % tiles_per_seq) * tm + lax.broadcasted_iota(jnp.int32, (tm, 1), 0)
    group = h.shape[1] // len(POOL_WINDOWS)
    ys = []
    for gi, w in enumerate(POOL_WINDOWS):
        cols = slice(gi * group, (gi + 1) * group)
        run = ext_ref[:, cols]
        span = 1
        while span < w:
            run = run + pltpu.roll(run, span, axis=0)
            span *= 2
        total = run[halo:, :]
        count = jnp.minimum(t + 1, w).astype(F32)
        pooled = total / count - u[:, cols]
        ys.append(_dot(pooled.astype(BF16), w_ref[gi]))
    y = jnp.concatenate(ys, axis=-1) * sc_ref[...]
    return h + _rms(y, gout_ref[...])


def _pack_in_weights(w_in):
    d = w_in.shape[0]
    hw = FOX_HEADS * HEAD_DIM
    kv = NSA_GROUPS * HEAD_DIM
    sizes = (hw, hw, hw, FOX_HEADS, NSA_HEADS * HEAD_DIM, kv, kv, kv, kv, kv, kv, 3 * NSA_HEADS)
    offs = np.concatenate([[0], np.cumsum(sizes)])
    fq, fk, fv, ff, nq, kc, vc, ks, vs, kw, vw, ng = [w_in[:, offs[j]:offs[j + 1]] for j in range(len(sizes))]
    scale = HEAD_DIM ** -0.5 * LOG2E

    w_main = jnp.concatenate([fq * scale, fk, fv, nq * scale, ks, vs, kw, vw], axis=1)
    w_cmp = jnp.concatenate([kc, vc], axis=1)
    pad = jnp.zeros((d, LANES - FOX_HEADS - 3 * NSA_HEADS), F32)
    w_small = jnp.concatenate([ff, ng, pad], axis=1)
    w_small_hi = w_small.astype(BF16)
    w_small_lo = (w_small - w_small_hi.astype(F32)).astype(BF16)
    return w_main.astype(BF16), w_cmp.astype(BF16), jnp.concatenate([w_small_hi, w_small_lo], axis=1)


def _mixer_fox_nsa(h2, g_in, g_out, w_in, b_f, cmp_pe, cmp_w1, cmp_w2, w_out, *, batch, seq):
    tile = WINDOW
    assert seq % tile == 0 and seq // tile <= LANES
    w_main, w_cmp, w_small = _pack_in_weights(w_in)
    pj, pc, small, nrm = _inproj(h2, g_in, w_main, w_cmp, w_small, tm=tile)
    npair = FOX_HEADS // 2
    q_blk, k_blk, v_blk, nq_blk = 0, npair, 2 * npair, 3 * npair
    ks_blk = nq_blk + NSA_HEADS // 2
    vs_blk, kw_blk, vw_blk = ks_blk + NSA_GROUPS, ks_blk + 2 * NSA_GROUPS, ks_blk + 3 * NSA_GROUPS

    c = _forget_cumsum(small, b_f, batch=batch, seq=seq, tc=min(CUMSUM_CHUNK, seq))
    nq = seq // tile
    fox_tq, fox_tk = min(FOX_TQ, seq), min(FOX_TK, seq)
    c_tiles = c.reshape(batch, npair, 2, seq // fox_tk, fox_tk).transpose(0, 1, 3, 2, 4)
    live = _fox_live_tiles(c, nrm, batch=batch, seq=seq, tq=fox_tq, tk=fox_tk, tm=tile)
    o_fox = _fox_attention(live, pj, c_tiles, batch=batch, seq=seq, tq=fox_tq, tk=fox_tk,
                           q_blk=q_blk, k_blk=k_blk, v_blk=v_blk)

    nch = seq // CMP_STRIDE
    ns = seq // SLC_BLOCK
    n_sel = min(N_SELECT, ns)
    kv = NSA_GROUPS * HEAD_DIM
    cmp_kv, cmp_nrm = _compress(pc, cmp_pe, cmp_w1, cmp_w2, batch=batch, seq=seq)

    group_slope = jnp.asarray([min(ALIBI_SLOPES_LOG2[g * NSA_GROUP_SIZE:(g + 1) * NSA_GROUP_SIZE])
                               for g in range(NSA_GROUPS)], F32)
    qn = nrm[:, NRM_NSA_Q, :NSA_HEADS].reshape(batch, nq, NSA_GROUPS, NSA_GROUP_SIZE).max(axis=3)
    ksn = nrm[:, NRM_NSA_KS, :NSA_GROUPS].reshape(batch, nq, NSA_GROUPS).max(axis=1)
    kcn = cmp_nrm[:, 0, :, 0, 0]
    tiles = jnp.arange(nq)
    nseg = CMP_SEGMENTS if nq % CMP_SEGMENTS == 0 else 1
    chunk = nch // nseg
    last_end = (jnp.arange(nseg) + 1) * chunk * CMP_STRIDE + (CMP_BLOCK - 1 - CMP_STRIDE)
    dist_c = (tiles[:, None] * tile - last_end[None, :] - (CMP_STRIDE - 1)).astype(F32)
    dead_c = _alibi_dead(qn[..., None], kcn[:, None, :, None], dist_c[None, :, None, :],
                         group_slope[None, None, :, None])
    first_chunk = jnp.minimum(dead_c.sum(axis=3), (tiles * nseg // nq)[None, :, None])
    first_chunk = first_chunk.transpose(0, 2, 1).astype(jnp.int32).reshape(-1)
    dist_s = ((tiles[:, None] - tiles[None, :] - 1) * tile + 1).astype(F32)
    dead_s = _alibi_dead(qn[..., None], ksn[:, None, :, None], dist_s[None, :, None, :],
                         group_slope[None, None, :, None])
    live_s = jnp.logical_not(dead_s).transpose(0, 2, 1, 3).astype(jnp.int32)

    ratio = SLC_BLOCK // CMP_STRIDE
    r = CMP_BLOCK // CMP_STRIDE
    n_idx = np.arange(nch)[:, None]
    b_idx = np.arange(ns)[None, :]
    pool_np = ((n_idx >= ratio * b_idx - (r - 1)) & (n_idx <= ratio * b_idx + ratio - 1)
               & (n_idx < nch - r + 1)).astype(np.float32)
    pool = jnp.asarray(pool_np, dtype=BF16)
    tmap_np = (np.arange(ns)[:, None] // (tile // SLC_BLOCK) == np.arange(LANES)[None, :]).astype(np.float32)
    o_c, sel, flag_blocks = _cmp_select(first_chunk, pj, cmp_kv, pool, jnp.asarray(tmap_np, dtype=BF16),
                                        batch=batch, seq=seq, tq=tile, q_blk=nq_blk // 2, n_sel=n_sel, nseg=nseg)
    flags = (flag_blocks[:, :, :, 0, :nq] * live_s).reshape(-1)
    o_nsa = _nsa_attention(flags, pj, sel, o_c, small, batch=batch, seq=seq, tile=tile,
                           q_blk=nq_blk // 2, ks_blk=ks_blk, vs_blk=vs_blk, kw_blk=kw_blk, vw_blk=vw_blk)

    hw = FOX_HEADS * HEAD_DIM
    return o_fox, o_nsa, w_out[:hw].astype(BF16), w_out[hw:].astype(BF16), g_out


def kernel(x, norm_g, attn_w_in, fox_b_f, nsa_cmp_pe, nsa_cmp_w1, nsa_cmp_w2, attn_w_out,
           pool_w, pool_scale, ffn_w_gate, ffn_w_up, ffn_w_down):
    batch, seq, d = x.shape
    depth = norm_g.shape[0]
    tm = 512 if seq % 512 == 0 else seq
    h = x.reshape(batch * seq, d)
    for layer in range(depth):
        g = norm_g[layer].reshape(4, 1, d)
        i = layer // 2
        if layer % 2 == 0:
            mixer = "attn"
            mix = _mixer_fox_nsa(h, g[0], g[1], attn_w_in[i], fox_b_f[i], nsa_cmp_pe[i], nsa_cmp_w1[i],
                                 nsa_cmp_w2[i], attn_w_out[i], batch=batch, seq=seq)
        else:
            mixer = "pool"
            mix = (g[0], g[1], pool_w[i].astype(BF16), pool_scale[i].reshape(1, d))
        h = _layer_tail(h, g[2], g[3], ffn_w_gate[layer].astype(BF16), ffn_w_up[layer].astype(BF16),
                        ffn_w_down[layer].astype(BF16), seq=seq, tm=tm, chunk=FFN_CHUNK, mixer=mixer, mix=mix)
    return h.reshape(batch, seq, d)
```

```python
import functools

import numpy as np
import jax
import jax.numpy as jnp
from jax import lax
from jax.experimental import pallas as pl
from jax.experimental.pallas import tpu as pltpu

F32 = jnp.float32
BF16 = jnp.bfloat16

HEAD_DIM = 64
FOX_HEADS = 8
NSA_HEADS = 8
NSA_GROUPS = 2
NSA_GROUP_SIZE = NSA_HEADS // NSA_GROUPS
CMP_BLOCK = 32
CMP_STRIDE = 16
SLC_BLOCK = 64
N_SELECT = 16
WINDOW = 512
POOL_WINDOWS = (2, 4, 8, 16)
POOL_HALO = max(POOL_WINDOWS)
RMS_EPS = 1e-6
NEG_INF = -1e30
BIG = 1e30
REMOVED = -3e38
LANES = 128
PAIR = 2 * HEAD_DIM
VMEM_LIMIT = 56 * 1024 * 1024
CMP_SEGMENTS = 4
UNDERFLOW_BITS = 160.0
NORM_SLACK = 1.02
NORM_CHUNKS = (0, 1, 3, 4)
NRM_FOX_Q, NRM_FOX_K, NRM_NSA_Q, NRM_NSA_KS = range(4)
FFN_CHUNK = 256
FOX_TQ, FOX_TK = 1024, 1024

LOG2E = 1.4426950408889634
ALIBI_SLOPES_LOG2 = tuple(float(2.0 ** (-8.0 * (i + 1.0) / NSA_HEADS)) * LOG2E for i in range(NSA_HEADS))


def _cparams(*sem):
    return pltpu.CompilerParams(dimension_semantics=sem, vmem_limit_bytes=VMEM_LIMIT)


def _rms(x, g):
    return x * lax.rsqrt(jnp.mean(x * x, axis=-1, keepdims=True) + RMS_EPS) * g


def _dot(a, b):
    return jnp.dot(a, b, preferred_element_type=F32)


def _dot_nt(a, b):
    return lax.dot_general(a, b, (((1,), (1,)), ((), ())), preferred_element_type=F32)


def _inproj_kernel(x_ref, g_ref, w_ref, wc_ref, ws_ref, hmap_ref, o_ref, oc_ref, os_ref, nrm_ref, *, fox_width):
    u = _rms(x_ref[...], g_ref[...])
    ub = u.astype(BF16)
    norms = []

    def max_sq_norms(r, ncols):
        rf = r.astype(F32)
        sq = _dot((rf * rf).astype(BF16), hmap_ref[0:ncols, :])
        norms.append(jnp.max(sq, axis=0, keepdims=True))

    heads_width = 4 * fox_width
    for c in range(0, heads_width, fox_width):
        r = _dot(ub, w_ref[:, c:c + fox_width]).astype(BF16)
        o_ref[:, c:c + fox_width] = r
        if c // fox_width in NORM_CHUNKS:
            max_sq_norms(r, fox_width)
    kv = _dot(ub, w_ref[:, heads_width:])
    max_sq_norms(kv[:, :PAIR].astype(BF16), PAIR)
    low = _lane_is_low()
    for blk in range(kv.shape[1] // PAIR):
        x = kv[:, blk * PAIR:(blk + 1) * PAIR]
        swapped = pltpu.roll(x, HEAD_DIM, axis=1)
        base = heads_width + 2 * blk * PAIR
        o_ref[:, base:base + PAIR] = jnp.where(low, x, swapped).astype(BF16)
        o_ref[:, base + PAIR:base + 2 * PAIR] = jnp.where(low, swapped, x).astype(BF16)
    oc_ref[...] = _dot(ub, wc_ref[...])
    width = os_ref.shape[1]
    u_lo = (u - ub.astype(F32)).astype(BF16)
    both = _dot(ub, ws_ref[...])
    os_ref[...] = both[:, :width] + both[:, width:] + _dot(u_lo, ws_ref[:, :width])
    row = lax.broadcasted_iota(jnp.int32, nrm_ref.shape, 0)
    out = jnp.zeros(nrm_ref.shape, F32)
    for k, nk in enumerate(norms):
        out = jnp.where(row == k, nk, out)
    nrm_ref[...] = out


def _inproj(x2, g, w_main, w_cmp, w_small, *, tm):
    n, d = x2.shape
    cin, cc, cs = w_main.shape[1], w_cmp.shape[1], w_small.shape[1] // 2
    fox_width = FOX_HEADS * HEAD_DIM
    cm = 4 * fox_width + 2 * (cin - 4 * fox_width)
    hmap = jnp.asarray((np.arange(fox_width)[:, None] // HEAD_DIM == np.arange(LANES)[None, :]).astype(np.float32),
                       dtype=BF16)
    const = lambda i: (0, 0)
    return pl.pallas_call(
        functools.partial(_inproj_kernel, fox_width=fox_width),
        out_shape=(jax.ShapeDtypeStruct((n, cm), BF16),
                   jax.ShapeDtypeStruct((n, cc), F32),
                   jax.ShapeDtypeStruct((n, cs), F32),
                   jax.ShapeDtypeStruct((n // tm, 8, LANES), F32)),
        grid=(n // tm,),
        in_specs=[pl.BlockSpec((tm, d), lambda i: (i, 0)),
                  pl.BlockSpec((1, d), const),
                  pl.BlockSpec((d, cin), const),
                  pl.BlockSpec((d, cc), const),
                  pl.BlockSpec((d, 2 * cs), const),
                  pl.BlockSpec((fox_width, LANES), const)],
        out_specs=(pl.BlockSpec((tm, cm), lambda i: (i, 0)),
                   pl.BlockSpec((tm, cc), lambda i: (i, 0)),
                   pl.BlockSpec((tm, cs), lambda i: (i, 0)),
                   pl.BlockSpec((None, 8, LANES), lambda i: (i, 0, 0))),
        compiler_params=_cparams("parallel"),
        name="inproj",
    )(x2, g, w_main, w_cmp, w_small, hmap)


def _forget_cumsum_kernel(sm_ref, bf_ref, tri_ref, o_ref, carry_ref):
    @pl.when(pl.program_id(1) == 0)
    def _():
        carry_ref[...] = jnp.zeros_like(carry_ref)

    zt = sm_ref[...].T
    z = zt[0:FOX_HEADS, :] + bf_ref[...]
    log_f = jnp.minimum(z, 0.0) - jnp.log1p(jnp.exp(-jnp.abs(z)))
    c = jnp.dot(log_f, tri_ref[...], preferred_element_type=F32,
                precision=lax.Precision.HIGHEST) + carry_ref[...]
    o_ref[...] = c * LOG2E
    carry_ref[...] = c[:, c.shape[1] - 1:]


def _forget_cumsum(small, b_f, *, batch, seq, tc):
    tri = jnp.asarray(np.triu(np.ones((tc, tc), np.float32)))
    nchunk = seq // tc
    return pl.pallas_call(
        _forget_cumsum_kernel,
        out_shape=jax.ShapeDtypeStruct((batch, FOX_HEADS, seq), F32),
        grid=(batch, nchunk),
        in_specs=[pl.BlockSpec((tc, small.shape[1]), lambda b, j: (b * nchunk + j, 0)),
                  pl.BlockSpec((FOX_HEADS, 1), lambda b, j: (0, 0)),
                  pl.BlockSpec((tc, tc), lambda b, j: (0, 0))],
        out_specs=pl.BlockSpec((None, FOX_HEADS, tc), lambda b, j: (b, 0, j)),
        scratch_shapes=[pltpu.VMEM((FOX_HEADS, 1), F32)],
        compiler_params=_cparams("parallel", "arbitrary"),
        name="forget_cumsum",
    )(small, b_f.reshape(FOX_HEADS, 1), tri)


def _lane_is_low():
    return lax.broadcasted_iota(jnp.int32, (1, PAIR), 1) < HEAD_DIM


def _with_ones(v_tile):
    return jnp.concatenate([v_tile, jnp.ones_like(v_tile)], axis=1)


def _flash_step(s, v, m_ref, acc_ref, l_ref=None):
    m_old = m_ref[...]
    m_new = jnp.maximum(m_old, jnp.max(s, axis=-1, keepdims=True))
    alpha = jnp.exp2(m_old - m_new)
    p = jnp.exp2(s - jnp.concatenate([m_new] * (s.shape[1] // LANES), axis=1))
    if l_ref is None:
        alpha_acc = jnp.concatenate([alpha, alpha], axis=1)
    else:
        alpha_acc = alpha
        l_ref[...] = alpha * l_ref[...] + jnp.sum(p, axis=-1, keepdims=True)
    acc_ref[...] = alpha_acc * acc_ref[...] + _dot(p.astype(BF16), v)
    m_ref[...] = m_new


def _flash_reset(m_ref, *sum_refs):
    m_ref[...] = jnp.full_like(m_ref, NEG_INF)
    for ref in sum_refs:
        ref[...] = jnp.zeros_like(ref)


def _flash_out(acc):
    return acc[:, :PAIR] / acc[:, PAIR:]


def _split_pair(q):
    low = _lane_is_low()
    zero = jnp.zeros_like(q)
    return jnp.where(low, q, zero), jnp.where(low, zero, q)


def _fox_kernel(live_ref, q_ref, k_ref, v_ref, c_ref, o_ref, m_ref, l_ref, acc_ref, *, tq, tk, nq):
    b = pl.program_id(0)
    pair = pl.program_id(1)
    i = pl.program_id(2)
    nk = c_ref.shape[0]
    qh = _split_pair(q_ref[...])
    _flash_reset(m_ref, l_ref, acc_ref)

    def head_step(h, j, mask):
        ks = pl.multiple_of(j * tk, tk)
        s = _dot_nt(qh[h], k_ref[pl.ds(ks, tk), :]) - c_ref[j, h:h + 1, :]
        if mask is not None:
            s = jnp.where(mask, s, NEG_INF)
        _flash_step(s, v_ref[pl.ds(ks, tk), :], m_ref.at[h], acc_ref.at[h], l_ref.at[h])

    def body(j, carry):
        live = [live_ref[(((b * FOX_HEADS) + 2 * pair + h) * nq + i) * nk + j] != 0 for h in range(2)]

        @pl.when(live[0] & live[1])
        def _():
            head_step(0, j, None)
            head_step(1, j, None)

        for h in range(2):
            @pl.when(live[h] & jnp.logical_not(live[1 - h]))
            def _(h=h):
                head_step(h, j, None)
        return carry

    lax.fori_loop(0, i, body, 0)

    def diag_part(h, r0, nr, nkeys):
        ks = pl.multiple_of(i * tq, tq)
        rows = pl.ds(r0, nr)
        row = lax.broadcasted_iota(jnp.int32, (nr, nkeys), 0) + r0
        col = lax.broadcasted_iota(jnp.int32, (nr, nkeys), 1)
        s = _dot_nt(qh[h][r0:r0 + nr], k_ref[pl.ds(ks, nkeys), :]) - c_ref[i, h:h + 1, 0:nkeys]
        _flash_step(jnp.where(col <= row, s, NEG_INF), v_ref[pl.ds(ks, nkeys), :],
                    m_ref.at[h, rows], acc_ref.at[h, rows], l_ref.at[h, rows])

    half = tq // 2
    for h in range(2):
        diag_part(h, 0, half, half)
        diag_part(h, half, half, tq)
    o_ref[...] = jnp.where(_lane_is_low(), acc_ref[0] / l_ref[0], acc_ref[1] / l_ref[1]).astype(o_ref.dtype)


def _fox_live_tiles(c2, nrm, *, batch, seq, tq, tk, tm):
    nq, nk = seq // tq, seq // tk
    qn = nrm[:, 0, :FOX_HEADS].reshape(batch, nq, tq // tm, FOX_HEADS).max(axis=2)
    kn = nrm[:, 1, :FOX_HEADS].reshape(batch, seq // tm, FOX_HEADS).max(axis=1)
    bound = 2.0 * NORM_SLACK * jnp.sqrt(qn * kn[:, None, :])
    c_start = c2[:, :, ::tq]
    c_end = c2[:, :, tk - 1::tk]
    decay = c_end[:, :, None, :] - c_start[:, :, :, None]
    live = decay < UNDERFLOW_BITS + jnp.transpose(bound, (0, 2, 1))[..., None]
    return live.astype(jnp.int32).reshape(-1)


def _fox_attention(live, pj, c_tiles, *, batch, seq, tq, tk, q_blk, k_blk, v_blk):
    assert tq == tk and tq % (2 * LANES) == 0
    nq = seq // tq
    nk = seq // tk
    npair = FOX_HEADS // 2
    grid_spec = pltpu.PrefetchScalarGridSpec(
        num_scalar_prefetch=1,
        grid=(batch, npair, nq),
        in_specs=[pl.BlockSpec((tq, PAIR), lambda b, p, i, f: (b * nq + i, q_blk + p)),
                  pl.BlockSpec((seq, PAIR), lambda b, p, i, f: (b, k_blk + p)),
                  pl.BlockSpec((seq, PAIR), lambda b, p, i, f: (b, v_blk + p)),
                  pl.BlockSpec((None, None, nk, 2, tk), lambda b, p, i, f: (b, p, 0, 0, 0))],
        out_specs=pl.BlockSpec((tq, PAIR), lambda b, p, i, f: (b * nq + i, p)),
        scratch_shapes=[pltpu.VMEM((2, tq, LANES), F32), pltpu.VMEM((2, tq, LANES), F32),
                        pltpu.VMEM((2, tq, PAIR), F32)])
    return pl.pallas_call(
        functools.partial(_fox_kernel, tq=tq, tk=tk, nq=nq),
        out_shape=jax.ShapeDtypeStruct((batch * seq, FOX_HEADS * HEAD_DIM), BF16),
        grid_spec=grid_spec,
        compiler_params=_cparams("parallel", "parallel", "arbitrary"),
        name="fox_attention",
    )(live, pj, pj, pj, c_tiles)


def _gelu_tanh(x):
    return 0.5 * x * (1.0 + jnp.tanh(0.7978845608028654 * (x + 0.044715 * x * x * x)))


def _compress_kernel(x_ref, pe_ref, w1pos_ref, w1dup_ref, w2sel_ref, o_ref, nrm_ref):
    nch = o_ref.shape[1]
    a = b = None
    for j in range(CMP_STRIDE):
        xj = x_ref[pl.ds(j, nch, stride=CMP_STRIDE), :].astype(BF16)
        pa, pb = _dot(xj, w1pos_ref[j]), _dot(xj, w1pos_ref[CMP_STRIDE + j])
        a, b = (pa, pb) if a is None else (a + pa, b + pb)
    pe_term = _dot(pe_ref[...], w1dup_ref[...])[0:1, :]
    h = a + pltpu.roll(b, nch - 1, axis=0) + pe_term
    y = _gelu_tanh(h).astype(BF16)
    for g in range(NSA_GROUPS):
        o = _dot(y, w2sel_ref[g]).astype(o_ref.dtype)
        o_ref[g] = o
        of = o.astype(F32)
        sq = 0.5 * jnp.sum(of * of, axis=-1, keepdims=True)
        nrm_ref[g] = jnp.broadcast_to(jnp.max(sq, axis=0, keepdims=True), nrm_ref.shape[1:])


def _compress(pc, cmp_pe, cmp_w1, cmp_w2, *, batch, seq):
    nch = seq // CMP_STRIDE
    dh = HEAD_DIM
    zero = jnp.zeros((2, CMP_BLOCK, dh, dh), F32)
    w1r = cmp_w1.reshape(2, CMP_BLOCK, dh, dh)
    w1pos = jnp.concatenate([jnp.concatenate([w1r, zero], axis=-1),
                             jnp.concatenate([zero, w1r], axis=-1)], axis=-2).astype(BF16)
    w1dup = jnp.concatenate([cmp_w1, cmp_w1], axis=-1).astype(BF16)
    w2dup = jnp.concatenate([cmp_w2, cmp_w2], axis=-1)
    zero2 = jnp.zeros_like(w2dup)
    w2sel = jnp.stack([jnp.concatenate([w2dup, zero2], axis=1),
                       jnp.concatenate([zero2, w2dup], axis=1)], axis=1).astype(BF16)
    pe8 = jnp.broadcast_to(cmp_pe.reshape(2, 1, CMP_BLOCK * dh), (2, 8, CMP_BLOCK * dh)).astype(BF16)
    return pl.pallas_call(
        _compress_kernel,
        out_shape=(jax.ShapeDtypeStruct((batch, 2, NSA_GROUPS, nch, PAIR), BF16),
                   jax.ShapeDtypeStruct((batch, 2, NSA_GROUPS, 8, LANES), F32)),
        grid=(batch, 2),
        in_specs=[pl.BlockSpec((seq, PAIR), lambda b, kv: (b, kv)),
                  pl.BlockSpec((None, 8, CMP_BLOCK * dh), lambda b, kv: (kv, 0, 0)),
                  pl.BlockSpec((None, CMP_BLOCK, PAIR, PAIR), lambda b, kv: (kv, 0, 0, 0)),
                  pl.BlockSpec((None, CMP_BLOCK * dh, PAIR), lambda b, kv: (kv, 0, 0)),
                  pl.BlockSpec((None, NSA_GROUPS, PAIR, PAIR), lambda b, kv: (kv, 0, 0, 0))],
        out_specs=(pl.BlockSpec((None, None, NSA_GROUPS, nch, PAIR), lambda b, kv: (b, kv, 0, 0, 0)),
                   pl.BlockSpec((None, None, NSA_GROUPS, 8, LANES), lambda b, kv: (b, kv, 0, 0, 0))),
        compiler_params=_cparams("parallel", "parallel"),
        name="nsa_compress",
    )(pc, pe8, w1pos, w1dup, w2sel)


def _group_slope(g, r):
    return jnp.where(g == 0, ALIBI_SLOPES_LOG2[r], ALIBI_SLOPES_LOG2[NSA_GROUP_SIZE + r]).astype(F32)


def _split3(x):
    hi = x.astype(BF16)
    r1 = x - hi.astype(F32)
    mid = r1.astype(BF16)
    lo = (r1 - mid.astype(F32)).astype(BF16)
    return hi, mid, lo


def _cmp_select_kernel(first_ref, q_ref, kc_ref, vc_ref, pool_ref, tmap_ref, oc_ref, sel_ref, flag_ref, *,
                       tq, n_sel, nseg, nq):
    b = pl.program_id(0)
    g = pl.program_id(1)
    i = pl.program_id(2)
    chunk = kc_ref.shape[0] // nseg
    first = first_ref[(b * NSA_GROUPS + g) * nq + i]
    n_live = (i * nseg) // nq + 1 - first
    for k in range(1, nseg + 1):
        @pl.when(n_live == k)
        def _(k=k):
            _cmp_select_body(q_ref, kc_ref, vc_ref, pool_ref, tmap_ref, oc_ref, sel_ref, flag_ref,
                             tq=tq, n_sel=n_sel, start=pl.multiple_of(first * chunk, chunk), nch=k * chunk)


def _cmp_select_body(q_ref, kc_ref, vc_ref, pool_ref, tmap_ref, oc_ref, sel_ref, flag_ref, *,
                     tq, n_sel, start, nch):
    g = pl.program_id(1)
    t0 = pl.program_id(2) * tq
    ns = pool_ref.shape[1]
    low = _lane_is_low()
    t = t0 + lax.broadcasted_iota(jnp.int32, (tq, 1), 0)
    cmp_end = (start + lax.broadcasted_iota(jnp.int32, (1, nch), 1)) * CMP_STRIDE + (CMP_BLOCK - 1)
    rel_end = (cmp_end - t0).astype(F32)
    mask_bias = jnp.where(t >= cmp_end, 0.0, NEG_INF)
    row_valid = jnp.where(t >= CMP_BLOCK - 1, 1.0, 0.0)
    kc = kc_ref[pl.ds(start, nch), :]
    vc = vc_ref[pl.ds(start, nch), :]
    imp = jnp.zeros((tq, nch), F32)
    for pair in range(NSA_GROUP_SIZE // 2):
        qh = _split_pair(q_ref[:, pair * PAIR:(pair + 1) * PAIR])
        outs = []
        for h in range(2):
            s = _dot_nt(qh[h], kc) + _group_slope(g, 2 * pair + h) * rel_end + mask_bias
            e = jnp.exp2(s - jnp.max(s, axis=-1, keepdims=True))
            p = e * (row_valid / jnp.maximum(jnp.sum(e, axis=-1, keepdims=True), 1e-30))
            outs.append(_dot(p.astype(BF16), vc))
            imp = imp + p
        oc_ref[:, pair * PAIR:(pair + 1) * PAIR] = jnp.where(low, outs[0], outs[1])

    pool = pool_ref[pl.ds(start, nch), :]
    hi, mid, lo = _split3(imp)
    p_slc = _dot(hi, pool) + _dot(mid, pool) + _dot(lo, pool)

    blk = lax.broadcasted_iota(jnp.int32, (ns, 1), 0)
    cur = (t0 + lax.broadcasted_iota(jnp.int32, (1, tq), 1)) // SLC_BLOCK
    forced = (blk == 0) | (blk == cur) | (blk == cur - 1)
    val = jnp.where(forced, REMOVED, jnp.where(blk > cur, NEG_INF, p_slc.T))
    sel_t = jnp.where(forced, 1.0, 0.0)
    blk_f = blk.astype(F32)
    for _ in range(n_sel - 3):
        mx = jnp.max(val, axis=0, keepdims=True)
        pick = jnp.min(jnp.where(val == mx, blk_f, float(ns)), axis=0, keepdims=True)
        hit = blk_f == pick
        sel_t = jnp.where(hit, 1.0, sel_t)
        val = jnp.where(hit, REMOVED, val)
    sel = sel_t.T
    sel_ref[...] = sel.astype(sel_ref.dtype)
    any_sel = jnp.broadcast_to(jnp.max(sel, axis=0, keepdims=True), (8, ns)).astype(BF16)
    flag_ref[...] = (_dot(any_sel, tmap_ref[...]) > 0.5).astype(jnp.int32)


def _cmp_select(first_chunk, pj, cmp_kv, pool, tmap, *, batch, seq, tq, q_blk, n_sel, nseg):
    nq = seq // tq
    nch = cmp_kv.shape[3]
    ns = pool.shape[1]
    gw = NSA_GROUP_SIZE * HEAD_DIM
    grid_spec = pltpu.PrefetchScalarGridSpec(
        num_scalar_prefetch=1,
        grid=(batch, NSA_GROUPS, nq),
        in_specs=[pl.BlockSpec((tq, gw), lambda b, g, i, f: (b * nq + i, q_blk + g)),
                  pl.BlockSpec((None, None, None, nch, PAIR), lambda b, g, i, f: (b, 0, g, 0, 0)),
                  pl.BlockSpec((None, None, None, nch, PAIR), lambda b, g, i, f: (b, 1, g, 0, 0)),
                  pl.BlockSpec((nch, ns), lambda b, g, i, f: (0, 0)),
                  pl.BlockSpec((ns, LANES), lambda b, g, i, f: (0, 0))],
        out_specs=(pl.BlockSpec((tq, gw), lambda b, g, i, f: (b * nq + i, g)),
                   pl.BlockSpec((None, None, tq, ns), lambda b, g, i, f: (b, g, i, 0)),
                   pl.BlockSpec((None, None, None, 8, LANES), lambda b, g, i, f: (b, g, i, 0, 0))))
    return pl.pallas_call(
        functools.partial(_cmp_select_kernel, tq=tq, n_sel=n_sel, nseg=nseg, nq=nq),
        out_shape=(jax.ShapeDtypeStruct((batch * seq, NSA_HEADS * HEAD_DIM), F32),
                   jax.ShapeDtypeStruct((batch, NSA_GROUPS, seq, ns), BF16),
                   jax.ShapeDtypeStruct((batch, NSA_GROUPS, nq, 8, LANES), jnp.int32)),
        grid_spec=grid_spec,
        compiler_params=_cparams("parallel", "parallel", "parallel"),
        name="nsa_cmp_select",
    )(first_chunk, pj, cmp_kv, cmp_kv, pool, tmap)


def _alibi_dead(q_sq, k_sq, dist, group_slope):
    bound = 2.0 * NORM_SLACK * jnp.sqrt(q_sq * k_sq)
    return group_slope * dist >= UNDERFLOW_BITS + bound


def _nsa_kernel(flags_ref, q_ref, ks_ref, vs_ref, kwp_ref, vwp_ref, kwd_ref, vwd_ref, sel_ref, oc_ref, gate_ref,
                o_ref, m_ref, acc_ref, *, tile, nq):
    b = pl.program_id(0)
    g = pl.program_id(1)
    i = pl.program_id(2)
    nh = NSA_GROUP_SIZE
    ns = sel_ref.shape[1]
    bpt = tile // SLC_BLOCK
    low = _lane_is_low()
    row = lax.broadcasted_iota(jnp.int32, (tile, tile), 0)
    col = lax.broadcasted_iota(jnp.int32, (tile, tile), 1)
    relpos = lax.broadcasted_iota(jnp.int32, (1, tile), 1).astype(F32)
    sel = sel_ref[...]
    blk_of_key = lax.broadcasted_iota(jnp.int32, (ns, tile), 1) // SLC_BLOCK
    blk_row = lax.broadcasted_iota(jnp.int32, (ns, tile), 0)
    qh = []
    for pair in range(nh // 2):
        qh.extend(_split_pair(q_ref[:, pair * PAIR:(pair + 1) * PAIR]))
    slopes = [_group_slope(g, r) for r in range(nh)]
    _flash_reset(m_ref, acc_ref)

    def selected(j):
        expand = jnp.where(blk_row == blk_of_key + j * bpt, 1.0, 0.0).astype(BF16)
        return _dot(sel, expand) > 0.5

    def attend(slot, k_keys, v_keys, rel, bias, r0=0, nr=tile):
        v_aug = _with_ones(v_keys)
        rows = pl.ds(r0, nr)
        for r in range(nh):
            s = _dot_nt(qh[r][r0:r0 + nr], k_keys) + slopes[r] * rel + bias
            _flash_step(s, v_aug, m_ref.at[slot + r, rows], acc_ref.at[slot + r, rows])

    def sel_step(j, keep):
        ks = pl.multiple_of(j * tile, tile)
        attend(0, ks_ref[pl.ds(ks, tile), :], vs_ref[pl.ds(ks, tile), :],
               relpos + ((j - i) * tile).astype(F32), jnp.where(keep, 0.0, NEG_INF))

    half = tile // 2
    ks_i = pl.multiple_of(i * tile, tile)
    bias_diag = jnp.where(selected(i) & (col <= row), 0.0, NEG_INF)
    attend(0, ks_ref[pl.ds(ks_i, half), :], vs_ref[pl.ds(ks_i, half), :], relpos[:, :half],
           bias_diag[:half, :half], 0, half)
    attend(0, ks_ref[pl.ds(ks_i, tile), :], vs_ref[pl.ds(ks_i, tile), :], relpos, bias_diag[half:, :], half, half)

    def sel_body(j, carry):
        @pl.when(flags_ref[((b * NSA_GROUPS + g) * nq + i) * nq + j] != 0)
        def _():
            sel_step(j, selected(j))
        return carry

    lax.fori_loop(0, i, sel_body, 0)

    span = tile + half
    k_win = jnp.concatenate([kwp_ref[...], kwd_ref[...]], axis=0)
    v_win = jnp.concatenate([vwp_ref[...], vwd_ref[...]], axis=0)
    for r0 in (0, half):
        row2 = lax.broadcasted_iota(jnp.int32, (half, span), 0) + r0
        col2 = lax.broadcasted_iota(jnp.int32, (half, span), 1) + r0
        in_window = ((col2 >= tile) & (col2 - tile <= row2)) | ((col2 < tile) & (col2 > row2) & (i > 0))
        rel2 = (lax.broadcasted_iota(jnp.int32, (1, span), 1) + (r0 - tile)).astype(F32)
        attend(nh, k_win[r0:r0 + span], v_win[r0:r0 + span], rel2, jnp.where(in_window, 0.0, NEG_INF), r0, half)

    gates = jax.nn.sigmoid(gate_ref[...])

    def gate_col(r, branch):
        c0 = FOX_HEADS + 3 * r + branch
        c1 = c0 + 3 * nh
        return jnp.where(g == 0, gates[:, c0:c0 + 1], gates[:, c1:c1 + 1])

    for pair in range(nh // 2):
        o_c = oc_ref[:, pair * PAIR:(pair + 1) * PAIR]
        outs = []
        for h in range(2):
            r = 2 * pair + h
            o_s = _flash_out(acc_ref[r])
            o_w = _flash_out(acc_ref[nh + r])
            outs.append(gate_col(r, 0) * o_c + gate_col(r, 1) * o_s + gate_col(r, 2) * o_w)
        o_ref[:, pair * PAIR:(pair + 1) * PAIR] = jnp.where(low, outs[0], outs[1]).astype(o_ref.dtype)


def _nsa_attention(flags, pj, sel, o_c, small, *, batch, seq, tile, q_blk, ks_blk, vs_blk, kw_blk, vw_blk):
    nq = seq // tile
    ns = sel.shape[3]
    gw = NSA_GROUP_SIZE * HEAD_DIM
    slots = 2 * NSA_GROUP_SIZE
    cur = lambda b, g, i, f, base: (b * nq + i, base + g)
    prev = lambda b, g, i, f, base: (b * nq + jnp.maximum(i - 1, 0), base + g)
    grid_spec = pltpu.PrefetchScalarGridSpec(
        num_scalar_prefetch=1,
        grid=(batch, NSA_GROUPS, nq),
        in_specs=[pl.BlockSpec((tile, gw), functools.partial(cur, base=q_blk)),
                  pl.BlockSpec((seq, PAIR), lambda b, g, i, f: (b, ks_blk + g)),
                  pl.BlockSpec((seq, PAIR), lambda b, g, i, f: (b, vs_blk + g)),
                  pl.BlockSpec((tile, PAIR), functools.partial(prev, base=kw_blk)),
                  pl.BlockSpec((tile, PAIR), functools.partial(prev, base=vw_blk)),
                  pl.BlockSpec((tile, PAIR), functools.partial(cur, base=kw_blk)),
                  pl.BlockSpec((tile, PAIR), functools.partial(cur, base=vw_blk)),
                  pl.BlockSpec((None, None, tile, ns), lambda b, g, i, f: (b, g, i, 0)),
                  pl.BlockSpec((tile, gw), functools.partial(cur, base=0)),
                  pl.BlockSpec((tile, small.shape[1]), lambda b, g, i, f: (b * nq + i, 0))],
        out_specs=pl.BlockSpec((tile, gw), functools.partial(cur, base=0)),
        scratch_shapes=[pltpu.VMEM((slots, tile, LANES), F32), pltpu.VMEM((slots, tile, 2 * PAIR), F32)])
    return pl.pallas_call(
        functools.partial(_nsa_kernel, tile=tile, nq=nq),
        out_shape=jax.ShapeDtypeStruct((batch * seq, NSA_HEADS * HEAD_DIM), BF16),
        grid_spec=grid_spec,
        compiler_params=_cparams("parallel", "parallel", "arbitrary"),
        name="nsa_attention",
    )(flags, pj, pj, pj, pj, pj, pj, pj, sel, o_c, small)


def _ffn_kernel(h_ref, gin_ref, gout_ref, wg_ref, wu_ref, wd_ref, *rest, chunk, mixer, tm, tiles_per_seq):
    h = h_ref[...]
    if mixer == "attn":
        a_ref, b_ref, wa_ref, wb_ref, gmix_ref, o_ref, acc_ref = rest
        h = h + _rms(_dot(a_ref[...], wa_ref[...]) + _dot(b_ref[...], wb_ref[...]), gmix_ref[...])
    else:
        halo_ref, pin_ref, pout_ref, pw_ref, sc_ref, o_ref, acc_ref, ext_ref = rest
        h = _pool_mix(h, halo_ref, pin_ref, pout_ref, pw_ref, sc_ref, ext_ref, tm=tm, tiles_per_seq=tiles_per_seq)
    ub = _rms(h, gin_ref[...]).astype(BF16)
    hidden = wg_ref.shape[1]
    for idx, c in enumerate(range(0, hidden, chunk)):
        gate = _dot(ub, wg_ref[:, c:c + chunk])
        up = _dot(ub, wu_ref[:, c:c + chunk])
        act = (gate * jax.nn.sigmoid(gate) * up).astype(BF16)
        part = _dot(act, wd_ref[c:c + chunk, :])
        if idx == 0:
            acc_ref[...] = part
        else:
            acc_ref[...] += part
    o_ref[...] = h + _rms(acc_ref[...], gout_ref[...])


def _layer_tail(h2, g_in, g_out, wg, wu, wd, *, seq, tm, chunk, mixer, mix):
    n, d = h2.shape
    const = lambda i: (0, 0)
    rows = lambda i: (i, 0)
    scratch = [pltpu.VMEM((tm, d), F32)]
    if mixer == "attn":
        a, b, wa, wb, g_mix = mix
        mix_specs = [pl.BlockSpec((tm, a.shape[1]), rows), pl.BlockSpec((tm, b.shape[1]), rows),
                     pl.BlockSpec(wa.shape, const), pl.BlockSpec(wb.shape, const), pl.BlockSpec((1, d), const)]
    else:
        p_in, p_out, w_groups, scale = mix
        ratio = tm // POOL_HALO
        mix = (h2, p_in, p_out, w_groups, scale)
        mix_specs = [pl.BlockSpec((POOL_HALO, d), lambda i: (jnp.maximum(i * ratio - 1, 0), 0)),
                     pl.BlockSpec((1, d), const), pl.BlockSpec((1, d), const),
                     pl.BlockSpec(w_groups.shape, lambda i: (0, 0, 0)), pl.BlockSpec((1, d), const)]
        scratch.append(pltpu.VMEM((tm + POOL_HALO, d), F32))
    return pl.pallas_call(
        functools.partial(_ffn_kernel, chunk=chunk, mixer=mixer, tm=tm, tiles_per_seq=seq // tm),
        out_shape=jax.ShapeDtypeStruct((n, d), F32),
        grid=(n // tm,),
        in_specs=[pl.BlockSpec((tm, d), rows),
                  pl.BlockSpec((1, d), const),
                  pl.BlockSpec((1, d), const),
                  pl.BlockSpec(wg.shape, const),
                  pl.BlockSpec(wu.shape, const),
                  pl.BlockSpec(wd.shape, const)] + mix_specs,
        out_specs=pl.BlockSpec((tm, d), rows),
        scratch_shapes=scratch,
        compiler_params=_cparams("parallel"),
        name=f"{mixer}_tail_swiglu_ffn",
    )(h2, g_in, g_out, wg, wu, wd, *mix)


def _pool_mix(h, halo_ref, gin_ref, gout_ref, w_ref, sc_ref, ext_ref, *, tm, tiles_per_seq):
    halo = POOL_HALO
    i = pl.program_id(0)
    first = (i % tiles_per_seq) == 0
    u = _rms(h, gin_ref[...])
    uh = _rms(halo_ref[...], gin_ref[...])
    ext_ref[0:halo, :] = jnp.where(first, 0.0, uh)
    ext_ref[halo:, :] = u
    t = (i % tiles_per_seq) * tm + lax.broadcasted_iota(jnp.int32, (tm, 1), 0)
    group = h.shape[1] // len(POOL_WINDOWS)
    ys = []
    for gi, w in enumerate(POOL_WINDOWS):
        cols = slice(gi * group, (gi + 1) * group)
        run = ext_ref[:, cols]
        span = 1
        while span < w:
            run = run + pltpu.roll(run, span, axis=0)
            span *= 2
        total = run[halo:, :]
        count = jnp.minimum(t + 1, w).astype(F32)
        pooled = total / count - u[:, cols]
        ys.append(_dot(pooled.astype(BF16), w_ref[gi]))
    y = jnp.concatenate(ys, axis=-1) * sc_ref[...]
    return h + _rms(y, gout_ref[...])


def _pack_in_weights(w_in):
    d = w_in.shape[0]
    hw = FOX_HEADS * HEAD_DIM
    kv = NSA_GROUPS * HEAD_DIM
    sizes = (hw, hw, hw, FOX_HEADS, NSA_HEADS * HEAD_DIM, kv, kv, kv, kv, kv, kv, 3 * NSA_HEADS)
    offs = np.concatenate([[0], np.cumsum(sizes)])
    fq, fk, fv, ff, nq, kc, vc, ks, vs, kw, vw, ng = [w_in[:, offs[j]:offs[j + 1]] for j in range(len(sizes))]
    scale = HEAD_DIM ** -0.5 * LOG2E

    w_main = jnp.concatenate([fq * scale, fk, fv, nq * scale, ks, vs, kw, vw], axis=1)
    w_cmp = jnp.concatenate([kc, vc], axis=1)
    pad = jnp.zeros((d, LANES - FOX_HEADS - 3 * NSA_HEADS), F32)
    w_small = jnp.concatenate([ff, ng, pad], axis=1)
    w_small_hi = w_small.astype(BF16)
    w_small_lo = (w_small - w_small_hi.astype(F32)).astype(BF16)
    return w_main.astype(BF16), w_cmp.astype(BF16), jnp.concatenate([w_small_hi, w_small_lo], axis=1)


def _mixer_fox_nsa(h2, g_in, g_out, w_in, b_f, cmp_pe, cmp_w1, cmp_w2, w_out, *, batch, seq):
    tile = WINDOW
    assert seq % tile == 0 and seq // tile <= LANES
    w_main, w_cmp, w_small = _pack_in_weights(w_in)
    pj, pc, small, nrm = _inproj(h2, g_in, w_main, w_cmp, w_small, tm=tile)
    npair = FOX_HEADS // 2
    q_blk, k_blk, v_blk, nq_blk = 0, npair, 2 * npair, 3 * npair
    ks_blk = nq_blk + NSA_HEADS // 2
    vs_blk, kw_blk, vw_blk = ks_blk + NSA_GROUPS, ks_blk + 2 * NSA_GROUPS, ks_blk + 3 * NSA_GROUPS

    c = _forget_cumsum(small, b_f, batch=batch, seq=seq, tc=tile)
    nq = seq // tile
    fox_tq, fox_tk = min(FOX_TQ, seq), min(FOX_TK, seq)
    c_tiles = c.reshape(batch, npair, 2, seq // fox_tk, fox_tk).transpose(0, 1, 3, 2, 4)
    live = _fox_live_tiles(c, nrm, batch=batch, seq=seq, tq=fox_tq, tk=fox_tk, tm=tile)
    o_fox = _fox_attention(live, pj, c_tiles, batch=batch, seq=seq, tq=fox_tq, tk=fox_tk,
                           q_blk=q_blk, k_blk=k_blk, v_blk=v_blk)

    nch = seq // CMP_STRIDE
    ns = seq // SLC_BLOCK
    n_sel = min(N_SELECT, ns)
    kv = NSA_GROUPS * HEAD_DIM
    cmp_kv, cmp_nrm = _compress(pc, cmp_pe, cmp_w1, cmp_w2, batch=batch, seq=seq)

    group_slope = jnp.asarray([min(ALIBI_SLOPES_LOG2[g * NSA_GROUP_SIZE:(g + 1) * NSA_GROUP_SIZE])
                               for g in range(NSA_GROUPS)], F32)
    qn = nrm[:, NRM_NSA_Q, :NSA_HEADS].reshape(batch, nq, NSA_GROUPS, NSA_GROUP_SIZE).max(axis=3)
    ksn = nrm[:, NRM_NSA_KS, :NSA_GROUPS].reshape(batch, nq, NSA_GROUPS).max(axis=1)
    kcn = cmp_nrm[:, 0, :, 0, 0]
    tiles = jnp.arange(nq)
    nseg = CMP_SEGMENTS if nq % CMP_SEGMENTS == 0 else 1
    chunk = nch // nseg
    last_end = (jnp.arange(nseg) + 1) * chunk * CMP_STRIDE + (CMP_BLOCK - 1 - CMP_STRIDE)
    dist_c = (tiles[:, None] * tile - last_end[None, :] - (CMP_STRIDE - 1)).astype(F32)
    dead_c = _alibi_dead(qn[..., None], kcn[:, None, :, None], dist_c[None, :, None, :],
                         group_slope[None, None, :, None])
    first_chunk = jnp.minimum(dead_c.sum(axis=3), (tiles * nseg // nq)[None, :, None])
    first_chunk = first_chunk.transpose(0, 2, 1).astype(jnp.int32).reshape(-1)
    dist_s = ((tiles[:, None] - tiles[None, :] - 1) * tile + 1).astype(F32)
    dead_s = _alibi_dead(qn[..., None], ksn[:, None, :, None], dist_s[None, :, None, :],
                         group_slope[None, None, :, None])
    live_s = jnp.logical_not(dead_s).transpose(0, 2, 1, 3).astype(jnp.int32)

    ratio = SLC_BLOCK // CMP_STRIDE
    r = CMP_BLOCK // CMP_STRIDE
    n_idx = np.arange(nch)[:, None]
    b_idx = np.arange(ns)[None, :]
    pool_np = ((n_idx >= ratio * b_idx - (r - 1)) & (n_idx <= ratio * b_idx + ratio - 1)
               & (n_idx < nch - r + 1)).astype(np.float32)
    pool = jnp.asarray(pool_np, dtype=BF16)
    tmap_np = (np.arange(ns)[:, None] // (tile // SLC_BLOCK) == np.arange(LANES)[None, :]).astype(np.float32)
    o_c, sel, flag_blocks = _cmp_select(first_chunk, pj, cmp_kv, pool, jnp.asarray(tmap_np, dtype=BF16),
                                        batch=batch, seq=seq, tq=tile, q_blk=nq_blk // 2, n_sel=n_sel, nseg=nseg)
    flags = (flag_blocks[:, :, :, 0, :nq] * live_s).reshape(-1)
    o_nsa = _nsa_attention(flags, pj, sel, o_c, small, batch=batch, seq=seq, tile=tile,
                           q_blk=nq_blk // 2, ks_blk=ks_blk, vs_blk=vs_blk, kw_blk=kw_blk, vw_blk=vw_blk)

    hw = FOX_HEADS * HEAD_DIM
    return o_fox, o_nsa, w_out[:hw].astype(BF16), w_out[hw:].astype(BF16), g_out


def kernel(x, norm_g, attn_w_in, fox_b_f, nsa_cmp_pe, nsa_cmp_w1, nsa_cmp_w2, attn_w_out,
           pool_w, pool_scale, ffn_w_gate, ffn_w_up, ffn_w_down):
    batch, seq, d = x.shape
    depth = norm_g.shape[0]
    tm = 512 if seq % 512 == 0 else seq
    h = x.reshape(batch * seq, d)
    for layer in range(depth):
        g = norm_g[layer].reshape(4, 1, d)
        i = layer // 2
        if layer % 2 == 0:
            mixer = "attn"
            mix = _mixer_fox_nsa(h, g[0], g[1], attn_w_in[i], fox_b_f[i], nsa_cmp_pe[i], nsa_cmp_w1[i],
                                 nsa_cmp_w2[i], attn_w_out[i], batch=batch, seq=seq)
        else:
            mixer = "pool"
            mix = (g[0], g[1], pool_w[i].astype(BF16), pool_scale[i].reshape(1, d))
        h = _layer_tail(h, g[2], g[3], ffn_w_gate[layer].astype(BF16), ffn_w_up[layer].astype(BF16),
                        ffn_w_down[layer].astype(BF16), seq=seq, tm=tm, chunk=FFN_CHUNK, mixer=mixer, mix=mix)
    return h.reshape(batch, seq, d)
```
